```python
import jax, jax.numpy as jnp
from jax import lax
import numpy as np

D_MODEL = 1024
BATCH = 16
SEQ = 4096
DEPTH = 1
DEC_BATCH = 32
DEC_SEQ = 16
PAST_LEN = 4096

CHUNK = 64
D_MIX = D_MODEL
H_A = 4
W_A = D_MIX // 2
DH_A = W_A // H_A
H_B = 4
W_B = D_MIX - W_A
DH_B = W_B // H_B
CONV_W = 4
N_MEM = 256
N_XHEADS = 4
DH_X = D_MODEL // N_XHEADS
D_FF = 4 * D_MODEL
IN_COLS = 4 * W_A + 2 * H_A + 4 * W_B + 2 * H_B
EPS = 1e-6

kernel_name = 'hybrid_mlstm_gdn_streaming_step'


def rms_norm(x, w):
    xf = x.astype(jnp.float32)
    y = xf * lax.rsqrt(jnp.mean(xf * xf, axis=-1, keepdims=True) + EPS)
    return (y * w.astype(jnp.float32)).astype(x.dtype)


def head_rms_norm(h, w):
    return h * lax.rsqrt(jnp.mean(h * h, axis=-1, keepdims=True) + EPS) * w.astype(jnp.float32)


def l2_normalize(x):
    return x * lax.rsqrt(jnp.sum(x * x, axis=-1, keepdims=True) + EPS)


def split_heads(a, n_heads):
    b, t, _ = a.shape
    return a.reshape(b, t, n_heads, -1).transpose(0, 2, 1, 3)


def merge_heads(a):
    b, h, t, dh = a.shape
    return a.transpose(0, 2, 1, 3).reshape(b, t, h * dh)


def to_chunks(a, chunk):
    b, h, t = a.shape[:3]
    a = a.reshape((b, h, t // chunk, chunk) + a.shape[3:])
    return jnp.moveaxis(a, 2, 0)


def from_chunks(a):
    a = jnp.moveaxis(a, 0, 2)
    b, h, nc, l = a.shape[:4]
    return a.reshape((b, h, nc * l) + a.shape[4:])


def mlstm_chunk_step(carry, inp):
    C0, n0, m0 = carry
    q, k, v, ig, lf = inp
    L = q.shape[-2]
    causal = jnp.tril(jnp.ones((L, L), dtype=bool))
    b = jnp.cumsum(lf, axis=-1)
    d = b[..., :, None] - b[..., None, :] + ig[..., None, :]
    d = jnp.where(causal, d, -jnp.inf)
    inter = b + m0[..., None]
    m = jnp.maximum(inter, jnp.max(d, axis=-1))
    w_intra = jnp.exp(d - m[..., None])
    w_inter = jnp.exp(inter - m)
    s = jnp.einsum('bhtd,bhsd->bhts', q, k) * w_intra
    num = w_inter[..., None] * jnp.einsum('bhtk,bhkv->bhtv', q, C0) + jnp.einsum('bhts,bhsv->bhtv', s, v)
    den = w_inter * jnp.einsum('bhtk,bhk->bht', q, n0) + jnp.sum(s, axis=-1)
    h = num / jnp.maximum(jnp.abs(den), jnp.exp(-m))[..., None]
    b_last = b[..., -1]
    m_last = m[..., -1]
    w_s = jnp.exp(b_last[..., None] - b + ig - m_last[..., None])
    decay0 = jnp.exp(b_last + m0 - m_last)
    C1 = decay0[..., None, None] * C0 + jnp.einsum('bhs,bhsk,bhsv->bhkv', w_s, k, v)
    n1 = decay0[..., None] * n0 + jnp.einsum('bhs,bhsk->bhk', w_s, k)
    return (C1, n1, m_last), h


def gdn_chunk_step(S0, inp):
    q, k, v, g, beta = inp
    L = q.shape[-2]
    incl = jnp.tril(jnp.ones((L, L), dtype=bool))
    strict = jnp.tril(jnp.ones((L, L), dtype=bool), -1)
    G = jnp.cumsum(g, axis=-1)
    decay = jnp.exp(jnp.where(incl, G[..., :, None] - G[..., None, :], -jnp.inf))
    kk = jnp.einsum('bhtd,bhsd->bhts', k, k)
    A = jnp.where(strict, beta[..., None] * decay * kk, 0.0)
    gam = jnp.exp(G)
    rhs = beta[..., None] * (v - gam[..., None] * jnp.einsum('bhtk,bhkv->bhtv', k, S0))
    W = lax.linalg.triangular_solve(A + jnp.eye(L, dtype=A.dtype), rhs,
                                    left_side=True, lower=True, unit_diagonal=True)
    qk = jnp.einsum('bhtd,bhsd->bhts', q, k) * decay
    o = gam[..., None] * jnp.einsum('bhtk,bhkv->bhtv', q, S0) + jnp.einsum('bhts,bhsv->bhtv', qk, W)
    G_last = G[..., -1]
    S1 = jnp.exp(G_last)[..., None, None] * S0 + jnp.einsum('bhs,bhsk,bhsv->bhkv', jnp.exp(G_last[..., None] - G), k, W)
    return S1, o


def causal_short_conv(u, buf, w):
    t = u.shape[1]
    up = jnp.concatenate([buf.astype(u.dtype), u], axis=1)
    y = up[:, 0:t] * w[0]
    for j in range(1, CONV_W):
        y = y + up[:, j:j + t] * w[j]
    return y, up[:, up.shape[1] - (CONV_W - 1):]


def hybrid_mixer(xn, C0, n0, m0, S0, conv_buf, chunk, w_in, mlstm_igate_b, mlstm_fgate_b,
                 mlstm_norm_w, gdn_conv_w, gdn_A_log, gdn_dt_bias, gdn_norm_w, w_out):
    f32 = jnp.float32
    proj = xn @ w_in
    sizes = [W_A, W_A, W_A, W_A, H_A, H_A, 3 * W_B, W_B, H_B, H_B]
    idx = np.cumsum(sizes)[:-1].tolist()
    qa, ka, va, oa, ia, fa, qkvb, zb, ab, bb = jnp.split(proj, idx, axis=-1)
    q = split_heads(qa.astype(f32), H_A)
    k = split_heads(ka.astype(f32), H_A) * (DH_A ** -0.5)
    v = split_heads(va.astype(f32), H_A)
    ig = (ia.astype(f32) + mlstm_igate_b.astype(f32)).transpose(0, 2, 1)
    lf = jax.nn.log_sigmoid(fa.astype(f32) + mlstm_fgate_b.astype(f32)).transpose(0, 2, 1)
    carry0 = (C0.astype(f32), n0.astype(f32), m0.astype(f32))
    (C1, n1, m1), h = lax.scan(mlstm_chunk_step, carry0,
                               (to_chunks(q, chunk), to_chunks(k, chunk), to_chunks(v, chunk),
                                to_chunks(ig, chunk), to_chunks(lf, chunk)))
    h = from_chunks(h)
    h_a = jax.nn.sigmoid(oa.astype(f32)) * merge_heads(head_rms_norm(h, mlstm_norm_w))
    c_out, conv_new = causal_short_conv(qkvb, conv_buf, gdn_conv_w)
    c_out = jax.nn.silu(c_out.astype(f32))
    qb, kb, vb = jnp.split(c_out, 3, axis=-1)
    qb = l2_normalize(split_heads(qb, H_B)) * (DH_B ** -0.5)
    kb = l2_normalize(split_heads(kb, H_B))
    vb = split_heads(vb, H_B)
    g = -jnp.exp(gdn_A_log.astype(f32)) * jax.nn.softplus(ab.astype(f32) + gdn_dt_bias.astype(f32))
    beta = jax.nn.sigmoid(bb.astype(f32))
    g = g.transpose(0, 2, 1)
    beta = beta.transpose(0, 2, 1)
    S1, o = lax.scan(gdn_chunk_step, S0.astype(f32),
                     (to_chunks(qb, chunk), to_chunks(kb, chunk), to_chunks(vb, chunk),
                      to_chunks(g, chunk), to_chunks(beta, chunk)))
    o = from_chunks(o)
    h_b = merge_heads(head_rms_norm(o, gdn_norm_w)) * jax.nn.silu(zb.astype(f32))
    y = jnp.concatenate([h_a, h_b], axis=-1).astype(xn.dtype) @ w_out
    return y, C1, n1, m1, S1, conv_new


def memory_kv(mem, norm_mem_w, wk, wv):
    b = mem.shape[0]
    mn = rms_norm(mem, norm_mem_w)
    mk = (mn @ wk).reshape(b, N_MEM, N_XHEADS, DH_X)
    mv = (mn @ wv).reshape(b, N_MEM, N_XHEADS, DH_X)
    return mk, mv


def memory_cross_attention(xn, mem_k, mem_v, wq, wo):
    b, t, _ = xn.shape
    q = (xn @ wq).reshape(b, t, N_XHEADS, DH_X)
    s = jnp.einsum('bthd,bnhd->bhtn', q, mem_k.astype(q.dtype)).astype(jnp.float32) * (DH_X ** -0.5)
    p = jax.nn.softmax(s, axis=-1).astype(xn.dtype)
    o = jnp.einsum('bhtn,bnhd->bthd', p, mem_v.astype(xn.dtype)).reshape(b, t, D_MODEL)
    return o @ wo


def squared_relu_mlp(xn, w1, w2):
    hdn = jax.nn.relu(xn @ w1)
    return (hdn * hdn) @ w2


def trunk_layer(x, mem_k, mem_v, C0, n0, m0, S0, conv_buf, chunk, norm_mix_w, w_in, mlstm_igate_b,
                mlstm_fgate_b, mlstm_norm_w, gdn_conv_w, gdn_A_log, gdn_dt_bias, gdn_norm_w, w_out,
                norm_x_w, wq_x, wo_x, norm_ffn_w, w_ff1, w_ff2):
    y, C1, n1, m1, S1, conv_new = hybrid_mixer(
        rms_norm(x, norm_mix_w), C0, n0, m0, S0, conv_buf, chunk, w_in, mlstm_igate_b, mlstm_fgate_b,
        mlstm_norm_w, gdn_conv_w, gdn_A_log, gdn_dt_bias, gdn_norm_w, w_out)
    x = x + y
    x = x + memory_cross_attention(rms_norm(x, norm_x_w), mem_k, mem_v, wq_x, wo_x)
    x = x + squared_relu_mlp(rms_norm(x, norm_ffn_w), w_ff1, w_ff2)
    return x, C1.astype(x.dtype), n1.astype(x.dtype), m1.astype(x.dtype), S1.astype(x.dtype), conv_new


def setup_inputs(seed: int = 0) -> dict:
    key = jax.random.key(seed)
    keys = list(jax.random.split(key, 48))
    f32 = jnp.float32

    def nrm(shape, scale):
        return jax.random.normal(keys.pop(), shape, f32) * scale

    def gain(shape):
        return 1.0 + 0.1 * jax.random.normal(keys.pop(), shape, f32)

    dt = jnp.exp(jax.random.uniform(keys.pop(), (DEPTH, H_B), f32, np.log(1e-3), np.log(1e-1)))
    inputs = {
        'x_prompt': nrm((BATCH, SEQ, D_MODEL), 1.0),
        'x_sample': nrm((DEC_BATCH, DEC_SEQ, D_MODEL), 1.0),
        'state_mlstm_C': nrm((DEPTH, DEC_BATCH, H_A, DH_A, DH_A), DH_A ** -0.5),
        'state_mlstm_n': nrm((DEPTH, DEC_BATCH, H_A, DH_A), DH_A ** -0.5),
        'state_mlstm_m': nrm((DEPTH, DEC_BATCH, H_A), 1.0),
        'state_gdn_S': nrm((DEPTH, DEC_BATCH, H_B, DH_B, DH_B), DH_B ** -0.5),
        'state_gdn_conv': nrm((DEPTH, DEC_BATCH, CONV_W - 1, 3 * W_B), 1.0),
        'cache_mem_k': nrm((DEPTH, DEC_BATCH, N_MEM, N_XHEADS, DH_X), 1.0),
        'cache_mem_v': nrm((DEPTH, DEC_BATCH, N_MEM, N_XHEADS, DH_X), 1.0),
        'mem_prompt': nrm((BATCH, N_MEM, D_MODEL), 1.0),
        'norm_mix_w': gain((DEPTH, D_MODEL)),
        'w_in': nrm((DEPTH, D_MODEL, IN_COLS), D_MODEL ** -0.5),
        'mlstm_igate_b': nrm((DEPTH, H_A), 0.1),
        'mlstm_fgate_b': jnp.linspace(3.0, 6.0, H_A, dtype=f32)[None, :] + nrm((DEPTH, H_A), 0.1),
        'mlstm_norm_w': gain((DEPTH, DH_A)),
        'gdn_conv_w': nrm((DEPTH, CONV_W, 3 * W_B), CONV_W ** -0.5),
        'gdn_A_log': jnp.log(jax.random.uniform(keys.pop(), (DEPTH, H_B), f32, 1.0, 16.0)),
        'gdn_dt_bias': dt + jnp.log(-jnp.expm1(-dt)),
        'gdn_norm_w': gain((DEPTH, DH_B)),
        'w_out': nrm((DEPTH, D_MIX, D_MODEL), D_MIX ** -0.5),
        'norm_x_w': gain((DEPTH, D_MODEL)),
        'norm_mem_w': gain((DEPTH, D_MODEL)),
        'wq_x': nrm((DEPTH, D_MODEL, D_MODEL), D_MODEL ** -0.5),
        'wk_x': nrm((DEPTH, D_MODEL, D_MODEL), D_MODEL ** -0.5),
        'wv_x': nrm((DEPTH, D_MODEL, D_MODEL), D_MODEL ** -0.5),
        'wo_x': nrm((DEPTH, D_MODEL, D_MODEL), D_MODEL ** -0.5),
        'norm_ffn_w': gain((DEPTH, D_MODEL)),
        'w_ff1': nrm((DEPTH, D_MODEL, D_FF), D_MODEL ** -0.5),
        'w_ff2': nrm((DEPTH, D_FF, D_MODEL), D_FF ** -0.5),
        'norm_final_w': gain((D_MODEL,)),
    }
    return inputs


def reference(x_prompt, x_sample, state_mlstm_C, state_mlstm_n, state_mlstm_m, state_gdn_S,
              state_gdn_conv, cache_mem_k, cache_mem_v, mem_prompt, norm_mix_w, w_in, mlstm_igate_b,
              mlstm_fgate_b, mlstm_norm_w, gdn_conv_w, gdn_A_log, gdn_dt_bias, gdn_norm_w, w_out,
              norm_x_w, norm_mem_w, wq_x, wk_x, wv_x, wo_x, norm_ffn_w, w_ff1, w_ff2, norm_final_w):
    f32 = jnp.float32
    bp = x_prompt.shape[0]
    sample_chunk = x_sample.shape[1]
    hp, hs = x_prompt, x_sample
    pC, pn, pm, pS, pconv, pmk, pmv = [], [], [], [], [], [], []
    sC, sn, sm, sS, sconv = [], [], [], [], []
    for l in range(DEPTH):
        mk_p, mv_p = memory_kv(mem_prompt, norm_mem_w[l], wk_x[l], wv_x[l])
        hp, C1, n1, m1, S1, cv1 = trunk_layer(
            hp, mk_p, mv_p,
            jnp.zeros((bp, H_A, DH_A, DH_A), f32), jnp.zeros((bp, H_A, DH_A), f32),
            jnp.zeros((bp, H_A), f32), jnp.zeros((bp, H_B, DH_B, DH_B), f32),
            jnp.zeros((bp, CONV_W - 1, 3 * W_B), hp.dtype), CHUNK,
            norm_mix_w[l], w_in[l], mlstm_igate_b[l], mlstm_fgate_b[l], mlstm_norm_w[l], gdn_conv_w[l],
            gdn_A_log[l], gdn_dt_bias[l], gdn_norm_w[l], w_out[l], norm_x_w[l], wq_x[l], wo_x[l],
            norm_ffn_w[l], w_ff1[l], w_ff2[l])
        pC.append(C1); pn.append(n1); pm.append(m1); pS.append(S1); pconv.append(cv1)
        pmk.append(mk_p); pmv.append(mv_p)
        hs, C2, n2, m2, S2, cv2 = trunk_layer(
            hs, cache_mem_k[l], cache_mem_v[l], state_mlstm_C[l], state_mlstm_n[l], state_mlstm_m[l],
            state_gdn_S[l], state_gdn_conv[l], sample_chunk,
            norm_mix_w[l], w_in[l], mlstm_igate_b[l], mlstm_fgate_b[l], mlstm_norm_w[l], gdn_conv_w[l],
            gdn_A_log[l], gdn_dt_bias[l], gdn_norm_w[l], w_out[l], norm_x_w[l], wq_x[l], wo_x[l],
            norm_ffn_w[l], w_ff1[l], w_ff2[l])
        sC.append(C2); sn.append(n2); sm.append(m2); sS.append(S2); sconv.append(cv2)
    y_prompt = rms_norm(hp, norm_final_w)
    y_sample = rms_norm(hs, norm_final_w)
    return (y_prompt, y_sample,
            jnp.stack(pC), jnp.stack(pn), jnp.stack(pm), jnp.stack(pS), jnp.stack(pconv),
            jnp.stack(pmk), jnp.stack(pmv),
            jnp.stack(sC), jnp.stack(sn), jnp.stack(sm), jnp.stack(sS), jnp.stack(sconv))
```

```python
import functools
import math

import jax
import jax.numpy as jnp
from jax import lax
from jax.experimental import pallas as pl
from jax.experimental.pallas import tpu as pltpu

F32 = jnp.float32
BF16 = jnp.bfloat16
EPS = 1e-6
N_HEADS = 4
HEAD_DIM = 128
GROUP_W = N_HEADS * HEAD_DIM
CONV_TAPS = 4
N_XHEADS = 4
CHUNK = 64
GATE_LANES = 128
NEG_BIG = -1e30
HIST_ROW = 8

VMEM_LIMIT_BYTES = 56 * 1024 * 1024
ROW_TILE = 512

_NT = (((1,), (1,)), ((), ()))
_TN = (((0,), (0,)), ((), ()))


def _rms(x, w):
    return x * lax.rsqrt(jnp.mean(x * x, axis=-1, keepdims=True) + EPS) * w


def _dot(a, b):
    return jnp.dot(a, b, preferred_element_type=F32)


def _sigmoid(x):
    return 1.0 / (1.0 + jnp.exp(-x))


def _softplus(x):
    return jnp.maximum(x, 0.0) + jnp.log1p(jnp.exp(-jnp.abs(x)))


def _params(sem):
    return pltpu.CompilerParams(dimension_semantics=sem, vmem_limit_bytes=VMEM_LIMIT_BYTES)


def _in_proj_body(x_ref, nw_ref, w_ref, wg_ref, proj_ref, gate_ref, *, col_tile):
    xn = _rms(x_ref[...], nw_ref[...]).astype(BF16)
    for c0 in range(0, w_ref.shape[1], col_tile):
        proj_ref[:, c0:c0 + col_tile] = _dot(xn, w_ref[:, c0:c0 + col_tile]).astype(BF16)
    gate_ref[...] = _dot(xn, wg_ref[...])


def _in_proj(x2d, norm_w, w_main, w_gate):
    rows, d = x2d.shape
    n = w_main.shape[1]
    tm = min(ROW_TILE, rows)
    return pl.pallas_call(
        functools.partial(_in_proj_body, col_tile=512),
        grid=(rows // tm,),
        in_specs=[
            pl.BlockSpec((tm, d), lambda i: (i, 0)),
            pl.BlockSpec((1, d), lambda i: (0, 0)),
            pl.BlockSpec((d, n), lambda i: (0, 0)),
            pl.BlockSpec((d, GATE_LANES), lambda i: (0, 0)),
        ],
        out_specs=[
            pl.BlockSpec((tm, n), lambda i: (i, 0)),
            pl.BlockSpec((tm, GATE_LANES), lambda i: (i, 0)),
        ],
        out_shape=[
            jax.ShapeDtypeStruct((rows, n), BF16),
            jax.ShapeDtypeStruct((rows, GATE_LANES), F32),
        ],
        compiler_params=_params(("arbitrary",)),
        name="in_proj",
    )(x2d, norm_w, w_main, w_gate)


def _unit_lower_inverse(a_strict):
    n = a_strict.shape[0]
    row = lax.broadcasted_iota(jnp.int32, (n, n), 0)
    col = lax.broadcasted_iota(jnp.int32, (n, n), 1)
    p = -a_strict
    t = jnp.where(row == col, 1.0, 0.0) + p
    for _ in range(int(math.log2(n)) - 1):
        pb = p.astype(BF16)
        p = _dot(pb, pb)
        t = t + _dot(t.astype(BF16), p.astype(BF16))
    return t


def _mixer_body(proj_ref, gate_ref, c0_ref, n0_ref, m0_ref, s0_ref, conv0_ref,
                gpar_ref, convw_ref, anw_ref, bnw_ref,
                h_ref, c1_ref, n1_ref, m1_ref, s1_ref, conv1_ref,
                c_sc, n_sc, m_sc, s_sc, u_sc, *, n_chunks, valid):
    L = CHUNK
    j = pl.program_id(1)
    scale = HEAD_DIM ** -0.5

    @pl.when(j == 0)
    def _load_state():
        c_sc[...] = c0_ref[0]
        n_sc[0:N_HEADS, :] = n0_ref[0]
        m_sc[0:1, 0:N_HEADS] = m0_ref[0]
        s_sc[...] = s0_ref[0]
        u_sc[HIST_ROW - (CONV_TAPS - 1):HIST_ROW, :] = conv0_ref[0]

    row_l = lax.broadcasted_iota(jnp.int32, (L, L), 0)
    col_l = lax.broadcasted_iota(jnp.int32, (L, L), 1)
    tri_incl = row_l >= col_l
    tri_strict = row_l > col_l
    tril_ones = jnp.where(tri_incl, 1.0, 0.0)
    pick = jnp.where(lax.broadcasted_iota(jnp.int32, (16, GATE_LANES), 0)
                     == lax.broadcasted_iota(jnp.int32, (16, GATE_LANES), 1), 1.0, 0.0)
    lane = lax.broadcasted_iota(jnp.int32, (L, GATE_LANES), 1)
    tok = lax.broadcasted_iota(jnp.int32, (L, GATE_LANES), 0)

    def chunk_step(c, carry):
        rows = pl.ds(pl.multiple_of(c * L, L), L)

        pre = gate_ref[0, rows, :] + gpar_ref[0:1, :]
        act = jnp.where(lane < 4, pre,
                        jnp.where(lane < 8, -_softplus(-pre),
                                  jnp.where(lane < 12, -jnp.exp(gpar_ref[1:2, :]) * _softplus(pre),
                                            jnp.where(lane < 16, _sigmoid(pre), 0.0))))
        if valid < L:
            act = jnp.where(tok >= valid, jnp.where(lane < 4, NEG_BIG, 0.0), act)
        cum = jnp.dot(tril_ones, act, precision=lax.Precision.HIGHEST, preferred_element_type=F32)
        z = jnp.where((lane >= 4) & (lane < 12), cum, act)
        zt = lax.dot_general(pick, z, _NT, precision=lax.Precision.HIGHEST,
                             preferred_element_type=F32)

        for h in range(N_HEADS):
            lo = h * HEAD_DIM
            ig_col, b_col = z[:, h:h + 1], z[:, 4 + h:5 + h]
            ig_row, b_row = zt[h:h + 1, :], zt[4 + h:5 + h, :]
            m0 = m_sc[0:1, h:h + 1]
            d = jnp.where(tri_incl, b_col - b_row + ig_row, NEG_BIG)
            inter = b_col + m0
            m = jnp.maximum(inter, jnp.max(d, axis=1, keepdims=True))
            w_intra = jnp.exp(d - m)
            w_inter = jnp.exp(inter - m)
            q = proj_ref[0, rows, lo:lo + HEAD_DIM]
            k = proj_ref[0, rows, GROUP_W + lo:GROUP_W + lo + HEAD_DIM]
            v = proj_ref[0, rows, 2 * GROUP_W + lo:2 * GROUP_W + lo + HEAD_DIM]
            og = proj_ref[0, rows, 3 * GROUP_W + lo:3 * GROUP_W + lo + HEAD_DIM].astype(F32)
            s = lax.dot_general(q, k, _NT, preferred_element_type=F32) * scale * w_intra
            c_prev = c_sc[h]
            n_prev = n_sc[h:h + 1, :]
            num = w_inter * _dot(q, c_prev.astype(BF16)) + _dot(s.astype(BF16), v)
            den = (w_inter * jnp.sum(q.astype(F32) * n_prev, axis=-1, keepdims=True)
                   + jnp.sum(s, axis=-1, keepdims=True))
            hh = num / jnp.maximum(jnp.abs(den), jnp.exp(-m))
            hn = _rms(hh, anw_ref[...])
            h_ref[0, rows, lo:lo + HEAD_DIM] = (_sigmoid(og) * hn).astype(BF16)
            b_last, m_last = b_col[L - 1:L, :], m[L - 1:L, :]
            w_s = jnp.exp(b_last - b_col + ig_col - m_last)
            decay0 = jnp.exp(b_last + m0 - m_last)
            kw = k.astype(F32) * scale * w_s
            c_sc[h] = decay0 * c_prev + lax.dot_general(kw.astype(BF16), v, _TN, preferred_element_type=F32)
            n_sc[h:h + 1, :] = decay0 * n_prev + jnp.sum(kw, axis=0, keepdims=True)
            m_sc[0:1, h:h + 1] = m_last

        base = 4 * GROUP_W
        u_sc[HIST_ROW:HIST_ROW + L, :] = proj_ref[0, rows, base:base + 3 * GROUP_W].astype(F32)
        y = convw_ref[0:1, :] * u_sc[HIST_ROW - 3:HIST_ROW - 3 + L, :]
        for t in range(1, CONV_TAPS):
            y = y + convw_ref[t:t + 1, :] * u_sc[HIST_ROW - 3 + t:HIST_ROW - 3 + t + L, :]
        cact = y * _sigmoid(y)
        u_sc[HIST_ROW - 3:HIST_ROW, :] = u_sc[HIST_ROW + valid - 3:HIST_ROW + valid, :]
        for h in range(N_HEADS):
            lo = h * HEAD_DIM
            qh = cact[:, lo:lo + HEAD_DIM]
            kh = cact[:, GROUP_W + lo:GROUP_W + lo + HEAD_DIM]
            vh = cact[:, 2 * GROUP_W + lo:2 * GROUP_W + lo + HEAD_DIM]
            qn = (qh * lax.rsqrt(jnp.sum(qh * qh, axis=-1, keepdims=True) + EPS) * scale).astype(BF16)
            kf = kh * lax.rsqrt(jnp.sum(kh * kh, axis=-1, keepdims=True) + EPS)
            kn = kf.astype(BF16)
            g_col, g_row = z[:, 8 + h:9 + h], zt[8 + h:9 + h, :]
            beta = z[:, 12 + h:13 + h]
            decay = jnp.exp(jnp.where(tri_incl, g_col - g_row, NEG_BIG))
            kk = lax.dot_general(kn, kn, _NT, preferred_element_type=F32)
            a = jnp.where(tri_strict, beta * decay * kk, 0.0)
            t_inv = _unit_lower_inverse(a)
            gam = jnp.exp(g_col)
            s_prev = s_sc[h]
            s_prev_b = s_prev.astype(BF16)
            rhs = beta * (vh - gam * _dot(kn, s_prev_b))
            w = _dot(t_inv.astype(BF16), rhs.astype(BF16)).astype(BF16)
            qk = lax.dot_general(qn, kn, _NT, preferred_element_type=F32) * decay
            o = gam * _dot(qn, s_prev_b) + _dot(qk.astype(BF16), w)
            g_last = g_col[L - 1:L, :]
            kd = (kf * jnp.exp(g_last - g_col)).astype(BF16)
            s_sc[h] = jnp.exp(g_last) * s_prev + lax.dot_general(kd, w, _TN, preferred_element_type=F32)
            zg = proj_ref[0, rows, base + 3 * GROUP_W + lo:base + 3 * GROUP_W + lo + HEAD_DIM].astype(F32)
            h_ref[0, rows, GROUP_W + lo:GROUP_W + lo + HEAD_DIM] = (
                _rms(o, bnw_ref[...]) * (zg * _sigmoid(zg))).astype(BF16)
        return carry

    if n_chunks == 1:
        chunk_step(0, 0)
    else:
        lax.fori_loop(0, n_chunks, chunk_step, 0)

    @pl.when(j == pl.num_programs(1) - 1)
    def _store_state():
        c1_ref[0] = c_sc[...]
        n1_ref[0] = n_sc[0:N_HEADS, :]
        m1_ref[0] = m_sc[0:1, 0:N_HEADS]
        s1_ref[0] = s_sc[...]
        conv1_ref[0] = u_sc[HIST_ROW - 3:HIST_ROW, :]


def _mixers(proj, gates, c0, n0, m0, s0, conv0, gate_par, conv_w, a_norm_w, b_norm_w, *, valid, tile):
    bsz, t, n = proj.shape
    tile = min(tile, t)
    assert t % tile == 0 and tile % CHUNK == 0
    assert valid == CHUNK or (t == CHUNK and CONV_TAPS - 1 <= valid < CHUNK)
    n_chunks = tile // CHUNK
    cw = 3 * GROUP_W
    bj = lambda b, j: (b, j, 0)
    st4 = lambda b, j: (b, 0, 0, 0)
    st3 = lambda b, j: (b, 0, 0)
    cst = lambda b, j: (0, 0)
    state_specs = [
        pl.BlockSpec((1, N_HEADS, HEAD_DIM, HEAD_DIM), st4),
        pl.BlockSpec((1, N_HEADS, HEAD_DIM), st3),
        pl.BlockSpec((1, 1, N_HEADS), st3),
        pl.BlockSpec((1, N_HEADS, HEAD_DIM, HEAD_DIM), st4),
        pl.BlockSpec((1, CONV_TAPS - 1, cw), st3),
    ]
    state_shapes = [
        jax.ShapeDtypeStruct((bsz, N_HEADS, HEAD_DIM, HEAD_DIM), F32),
        jax.ShapeDtypeStruct((bsz, N_HEADS, HEAD_DIM), F32),
        jax.ShapeDtypeStruct((bsz, 1, N_HEADS), F32),
        jax.ShapeDtypeStruct((bsz, N_HEADS, HEAD_DIM, HEAD_DIM), F32),
        jax.ShapeDtypeStruct((bsz, CONV_TAPS - 1, cw), F32),
    ]
    return pl.pallas_call(
        functools.partial(_mixer_body, n_chunks=n_chunks, valid=valid),
        grid=(bsz, t // tile),
        in_specs=[
            pl.BlockSpec((1, tile, n), bj),
            pl.BlockSpec((1, tile, GATE_LANES), bj),
            *state_specs,
            pl.BlockSpec((8, GATE_LANES), cst),
            pl.BlockSpec((CONV_TAPS, cw), cst),
            pl.BlockSpec((1, HEAD_DIM), cst),
            pl.BlockSpec((1, HEAD_DIM), cst),
        ],
        out_specs=[pl.BlockSpec((1, tile, 2 * GROUP_W), bj), *state_specs],
        out_shape=[jax.ShapeDtypeStruct((bsz, t, 2 * GROUP_W), BF16), *state_shapes],
        scratch_shapes=[
            pltpu.VMEM((N_HEADS, HEAD_DIM, HEAD_DIM), F32),
            pltpu.VMEM((8, HEAD_DIM), F32),
            pltpu.VMEM((8, GATE_LANES), F32),
            pltpu.VMEM((N_HEADS, HEAD_DIM, HEAD_DIM), F32),
            pltpu.VMEM((HIST_ROW + CHUNK, cw), F32),
        ],
        compiler_params=_params(("arbitrary", "arbitrary")),
        name="mixers",
    )(proj, gates, c0, n0, m0, s0, conv0, gate_par, conv_w, a_norm_w, b_norm_w)


def _mem_kv_body(mem_ref, nw_ref, wkv_ref, k_ref, v_ref, kb_ref, vb_ref):
    mn = _rms(mem_ref[...], nw_ref[...]).astype(BF16)
    d = k_ref.shape[1]
    k = _dot(mn, wkv_ref[:, 0:d])
    v = _dot(mn, wkv_ref[:, d:2 * d])
    k_ref[...] = k
    v_ref[...] = v
    kb_ref[...] = k.astype(BF16)
    vb_ref[...] = v.astype(BF16)


def _mem_kv(mem2d, norm_w, wkv):
    rows, d = mem2d.shape
    tm = min(ROW_TILE, rows)
    row = lambda i: (i, 0)
    cst = lambda i: (0, 0)
    return pl.pallas_call(
        _mem_kv_body,
        grid=(rows // tm,),
        in_specs=[pl.BlockSpec((tm, d), row), pl.BlockSpec((1, d), cst), pl.BlockSpec((d, 2 * d), cst)],
        out_specs=[pl.BlockSpec((tm, d), row)] * 4,
        out_shape=[jax.ShapeDtypeStruct((rows, d), F32)] * 2 + [jax.ShapeDtypeStruct((rows, d), BF16)] * 2,
        compiler_params=_params(("arbitrary",)),
        name="mem_kv",
    )(mem2d, norm_w, wkv)


def _attn_body(h_ref, x_ref, wout_ref, nx_ref, wq_ref, mk_ref, mv_ref, wo_ref, o_ref):
    d = x_ref.shape[2]
    dh = d // N_XHEADS
    x1 = x_ref[0] + _dot(h_ref[0], wout_ref[...])
    xn = _rms(x1, nx_ref[...]).astype(BF16)
    q = _dot(xn, wq_ref[...]).astype(BF16)
    acc = x1
    for h in range(N_XHEADS):
        cols = slice(h * dh, (h + 1) * dh)
        s = lax.dot_general(q[:, cols], mk_ref[0, :, cols], _NT, preferred_element_type=F32) * (dh ** -0.5)
        e = jnp.exp(s - jnp.max(s, axis=-1, keepdims=True))
        p = (e / jnp.sum(e, axis=-1, keepdims=True)).astype(BF16)
        oh = _dot(p, mv_ref[0, :, cols]).astype(BF16)
        acc = acc + _dot(oh, wo_ref[cols, :])
    o_ref[0] = acc


def _out_proj_attn(hcat, x, w_out, norm_w, wq, mk, mv, wo):
    bsz, t, d = x.shape
    n_mem = mk.shape[1]
    tm = min(ROW_TILE, t)
    bj = lambda b, j: (b, j, 0)
    cst = lambda b, j: (0, 0)
    mem = lambda b, j: (b, 0, 0)
    return pl.pallas_call(
        _attn_body,
        grid=(bsz, t // tm),
        in_specs=[
            pl.BlockSpec((1, tm, d), bj),
            pl.BlockSpec((1, tm, d), bj),
            pl.BlockSpec((d, d), cst),
            pl.BlockSpec((1, d), cst),
            pl.BlockSpec((d, d), cst),
            pl.BlockSpec((1, n_mem, d), mem),
            pl.BlockSpec((1, n_mem, d), mem),
            pl.BlockSpec((d, d), cst),
        ],
        out_specs=pl.BlockSpec((1, tm, d), bj),
        out_shape=jax.ShapeDtypeStruct((bsz, t, d), F32),
        compiler_params=_params(("arbitrary", "arbitrary")),
        name="out_proj_attn",
    )(hcat, x, w_out, norm_w, wq, mk, mv, wo)


def _ffn_body(x_ref, nf_ref, w1_ref, w2_ref, nfin_ref, o_ref, *, hid_tile, final_norm):
    x = x_ref[...]
    xn = _rms(x, nf_ref[...]).astype(BF16)
    acc = x
    for c0 in range(0, w1_ref.shape[1], hid_tile):
        hdn = jnp.maximum(_dot(xn, w1_ref[:, c0:c0 + hid_tile]), 0.0)
        acc = acc + _dot((hdn * hdn).astype(BF16), w2_ref[c0:c0 + hid_tile, :])
    o_ref[...] = _rms(acc, nfin_ref[...]) if final_norm else acc


def _ffn(x2d, norm_w, w1, w2, final_w, *, final_norm):
    rows, d = x2d.shape
    dff = w1.shape[1]
    tm = min(ROW_TILE, rows)
    row = lambda i: (i, 0)
    cst = lambda i: (0, 0)
    return pl.pallas_call(
        functools.partial(_ffn_body, hid_tile=1024, final_norm=final_norm),
        grid=(rows // tm,),
        in_specs=[
            pl.BlockSpec((tm, d), row),
            pl.BlockSpec((1, d), cst),
            pl.BlockSpec((d, dff), cst, pipeline_mode=pl.Buffered(1)),
            pl.BlockSpec((dff, d), cst, pipeline_mode=pl.Buffered(1)),
            pl.BlockSpec((1, d), cst),
        ],
        out_specs=pl.BlockSpec((tm, d), row),
        out_shape=jax.ShapeDtypeStruct((rows, d), F32),
        compiler_params=_params(("arbitrary",)),
        name="ffn",
    )(x2d, norm_w, w1, w2, final_w)


def _split_w_in(w_in):
    w = GROUP_W
    g0 = 4 * w
    b0 = g0 + 2 * N_HEADS
    g1 = b0 + 4 * w
    main = jnp.concatenate([w_in[:, :g0], w_in[:, b0:g1]], axis=1).astype(BF16)
    gate = jnp.concatenate([w_in[:, g0:b0], w_in[:, g1:g1 + 2 * N_HEADS]], axis=1)
    gate = jnp.pad(gate, ((0, 0), (0, GATE_LANES - 4 * N_HEADS))).astype(BF16)
    return main, gate


def _gate_params(igate_b, fgate_b, a_log, dt_bias):
    zeros = jnp.zeros((N_HEADS,), F32)
    bias = jnp.concatenate([igate_b, fgate_b, dt_bias, zeros]).astype(F32)
    alog = jnp.concatenate([zeros, zeros, a_log.astype(F32), zeros])
    par = jnp.zeros((8, GATE_LANES), F32)
    return par.at[0, :4 * N_HEADS].set(bias).at[1, :4 * N_HEADS].set(alog)


def _trunk_layer(x, mem_k, mem_v, c0, n0, m0, s0, conv0, *, valid, w_main, w_gate, gate_par, lw, final_w,
                 final_norm):
    bsz, t, d = x.shape
    tp = CHUNK if valid < CHUNK else t
    xp = x if tp == t else jnp.pad(x, ((0, 0), (0, tp - t), (0, 0)))
    proj, gates = _in_proj(xp.reshape(bsz * tp, d), lw["norm_mix_w"], w_main, w_gate)
    hcat, c1, n1, m1, s1, conv1 = _mixers(
        proj.reshape(bsz, tp, -1), gates.reshape(bsz, tp, GATE_LANES),
        c0, n0, m0.reshape(bsz, 1, N_HEADS), s0, conv0,
        gate_par, lw["gdn_conv_w"], lw["mlstm_norm_w"], lw["gdn_norm_w"], valid=valid, tile=256)
    x2 = _out_proj_attn(hcat, x, lw["w_out"], lw["norm_x_w"], lw["wq_x"], mem_k, mem_v, lw["wo_x"])
    y = _ffn(x2.reshape(bsz * t, d), lw["norm_ffn_w"], lw["w_ff1"], lw["w_ff2"], final_w,
             final_norm=final_norm)
    return y.reshape(bsz, t, d), c1, n1, m1.reshape(bsz, N_HEADS), s1, conv1


def kernel(x_prompt, x_sample, state_mlstm_C, state_mlstm_n, state_mlstm_m, state_gdn_S, state_gdn_conv, cache_mem_k, cache_mem_v, mem_prompt, norm_mix_w, w_in, mlstm_igate_b, mlstm_fgate_b, mlstm_norm_w, gdn_conv_w, gdn_A_log, gdn_dt_bias, gdn_norm_w, w_out, norm_x_w, norm_mem_w, wq_x, wk_x, wv_x, wo_x, norm_ffn_w, w_ff1, w_ff2, norm_final_w):
    depth = w_in.shape[0]
    bp, _, d = x_prompt.shape
    bs, ts, _ = x_sample.shape
    n_mem = mem_prompt.shape[1]
    row = lambda a: a.reshape(1, -1).astype(F32)
    hp, hs = x_prompt, x_sample
    outs_p = [[] for _ in range(7)]
    outs_s = [[] for _ in range(5)]
    for l in range(depth):
        w_main, w_gate = _split_w_in(w_in[l])
        gate_par = _gate_params(mlstm_igate_b[l], mlstm_fgate_b[l], gdn_A_log[l], gdn_dt_bias[l])
        lw = dict(
            norm_mix_w=row(norm_mix_w[l]), gdn_conv_w=gdn_conv_w[l].astype(F32),
            mlstm_norm_w=row(mlstm_norm_w[l]), gdn_norm_w=row(gdn_norm_w[l]),
            w_out=w_out[l].astype(BF16), norm_x_w=row(norm_x_w[l]), wq_x=wq_x[l].astype(BF16),
            wo_x=wo_x[l].astype(BF16), norm_ffn_w=row(norm_ffn_w[l]),
            w_ff1=w_ff1[l].astype(BF16), w_ff2=w_ff2[l].astype(BF16))
        common = dict(w_main=w_main, w_gate=w_gate, gate_par=gate_par, lw=lw, final_w=row(norm_final_w),
                      final_norm=(l == depth - 1))
        wkv = jnp.concatenate([wk_x[l], wv_x[l]], axis=1).astype(BF16)
        mk, mv, mk_b, mv_b = _mem_kv(mem_prompt.reshape(bp * n_mem, d), row(norm_mem_w[l]), wkv)
        hp, c1, n1, m1, s1, cv1 = _trunk_layer(
            hp, mk_b.reshape(bp, n_mem, d), mv_b.reshape(bp, n_mem, d),
            jnp.zeros((bp, N_HEADS, HEAD_DIM, HEAD_DIM), F32), jnp.zeros((bp, N_HEADS, HEAD_DIM), F32),
            jnp.zeros((bp, N_HEADS), F32), jnp.zeros((bp, N_HEADS, HEAD_DIM, HEAD_DIM), F32),
            jnp.zeros((bp, CONV_TAPS - 1, 3 * GROUP_W), F32), valid=CHUNK, **common)
        for acc, val in zip(outs_p, (c1, n1, m1, s1, cv1,
                                     mk.reshape(bp, n_mem, N_XHEADS, d // N_XHEADS),
                                     mv.reshape(bp, n_mem, N_XHEADS, d // N_XHEADS))):
            acc.append(val)
        hs, c2, n2, m2, s2, cv2 = _trunk_layer(
            hs, cache_mem_k[l].reshape(bs, n_mem, d).astype(BF16), cache_mem_v[l].reshape(bs, n_mem, d).astype(BF16),
            state_mlstm_C[l], state_mlstm_n[l], state_mlstm_m[l], state_gdn_S[l], state_gdn_conv[l],
            valid=ts, **common)
        for acc, val in zip(outs_s, (c2, n2, m2, s2, cv2)):
            acc.append(val)
    return (hp, hs, *[jnp.stack(a) for a in outs_p], *[jnp.stack(a) for a in outs_s])
```

```python
import functools
import math

import jax
import jax.numpy as jnp
from jax import lax
from jax.experimental import pallas as pl
from jax.experimental.pallas import tpu as pltpu

F32 = jnp.float32
BF16 = jnp.bfloat16
EPS = 1e-6
N_HEADS = 4
HEAD_DIM = 128
GROUP_W = N_HEADS * HEAD_DIM
CONV_TAPS = 4
N_XHEADS = 4
CHUNK = 64
GATE_LANES = 128
NEG_BIG = -1e30
HIST_ROW = 8

VMEM_LIMIT_BYTES = 56 * 1024 * 1024
ROW_TILE = 512

_NT = (((1,), (1,)), ((), ()))
_TN = (((0,), (0,)), ((), ()))


def _rms(x, w):
    return x * lax.rsqrt(jnp.mean(x * x, axis=-1, keepdims=True) + EPS) * w


def _dot(a, b):
    return jnp.dot(a, b, preferred_element_type=F32)


def _sigmoid(x):
    return 1.0 / (1.0 + jnp.exp(-x))


def _softplus(x):
    return jnp.maximum(x, 0.0) + jnp.log1p(jnp.exp(-jnp.abs(x)))


def _params(sem):
    return pltpu.CompilerParams(dimension_semantics=sem, vmem_limit_bytes=VMEM_LIMIT_BYTES)


def _in_proj_body(x_ref, nw_ref, w_ref, wg_ref, proj_ref, gate_ref, *, col_tile):
    xn = _rms(x_ref[...], nw_ref[...]).astype(BF16)
    for c0 in range(0, w_ref.shape[1], col_tile):
        proj_ref[:, c0:c0 + col_tile] = _dot(xn, w_ref[:, c0:c0 + col_tile]).astype(BF16)
    gate_ref[...] = _dot(xn, wg_ref[...])


def _in_proj(x2d, norm_w, w_main, w_gate):
    rows, d = x2d.shape
    n = w_main.shape[1]
    tm = min(ROW_TILE, rows)
    return pl.pallas_call(
        functools.partial(_in_proj_body, col_tile=512),
        grid=(rows // tm,),
        in_specs=[
            pl.BlockSpec((tm, d), lambda i: (i, 0)),
            pl.BlockSpec((1, d), lambda i: (0, 0)),
            pl.BlockSpec((d, n), lambda i: (0, 0)),
            pl.BlockSpec((d, GATE_LANES), lambda i: (0, 0)),
        ],
        out_specs=[
            pl.BlockSpec((tm, n), lambda i: (i, 0)),
            pl.BlockSpec((tm, GATE_LANES), lambda i: (i, 0)),
        ],
        out_shape=[
            jax.ShapeDtypeStruct((rows, n), BF16),
            jax.ShapeDtypeStruct((rows, GATE_LANES), F32),
        ],
        compiler_params=_params(("arbitrary",)),
        name="in_proj",
    )(x2d, norm_w, w_main, w_gate)


def _mixer_body(proj_ref, gate_ref, c0_ref, n0_ref, m0_ref, s0_ref, conv0_ref,
                gpar_ref, convw_ref, anw_ref, bnw_ref,
                h_ref, c1_ref, n1_ref, m1_ref, s1_ref, conv1_ref,
                c_sc, n_sc, m_sc, s_sc, u_sc, qn_sc, kf_sc, v_sc, dc_sc, sv_sc, nc_sc, lb_sc, qu_sc,
                *, n_chunks, valid):
    L = CHUNK
    T = n_chunks * L
    j = pl.program_id(1)
    scale = HEAD_DIM ** -0.5
    heads = range(N_HEADS)
    chunks = range(n_chunks)

    @pl.when(j == 0)
    def _load_state():
        c_sc[...] = c0_ref[0]
        n_sc[0:N_HEADS, :] = n0_ref[0]
        m_sc[0:1, 0:N_HEADS] = m0_ref[0]
        s_sc[...] = s0_ref[0]
        u_sc[HIST_ROW - (CONV_TAPS - 1):HIST_ROW, :] = conv0_ref[0]

    lane = lax.broadcasted_iota(jnp.int32, (L, 2 * L), 1)
    tok = lax.broadcasted_iota(jnp.int32, (L, 2 * L), 0)
    src = lane & (L - 1)
    left = lane < L
    tri_incl = tok >= src
    tri_strict = tok > src
    eye_left = jnp.where(left & (tok == src), 1.0, 0.0)
    pick = jnp.where(lax.broadcasted_iota(jnp.int32, (16, GATE_LANES), 0)
                     == lax.broadcasted_iota(jnp.int32, (16, GATE_LANES), 1), 1.0, 0.0)

    def two(x):
        return jnp.concatenate([x, x], axis=0)

    lane_t = lax.broadcasted_iota(jnp.int32, (T, GATE_LANES), 1)
    pre = gate_ref[0] + gpar_ref[0:1, :]
    act = jnp.where(lane_t < 4, pre,
                    jnp.where(lane_t < 8, -_softplus(-pre),
                              jnp.where(lane_t < 12, -jnp.exp(gpar_ref[1:2, :]) * _softplus(pre),
                                        jnp.where(lane_t < 16, _sigmoid(pre), 0.0))))
    if valid < L:
        tok_t = lax.broadcasted_iota(jnp.int32, (T, GATE_LANES), 0)
        act = jnp.where(tok_t >= valid, jnp.where(lane_t < 4, NEG_BIG, 0.0), act)
    r_t = lax.broadcasted_iota(jnp.int32, (T, T), 0)
    c_t = lax.broadcasted_iota(jnp.int32, (T, T), 1)
    same_chunk = (r_t >> int(math.log2(L))) == (c_t >> int(math.log2(L)))
    cum = jnp.dot(jnp.where((r_t >= c_t) & same_chunk, 1.0, 0.0), act,
                  precision=lax.Precision.HIGHEST, preferred_element_type=F32)
    z_t = jnp.where((lane_t >= 4) & (lane_t < 12), cum, act)
    z = [z_t[c * L:(c + 1) * L, :] for c in chunks]
    zt = [lax.dot_general(pick, two(z[c]), _NT, precision=lax.Precision.HIGHEST,
                          preferred_element_type=F32) for c in chunks]

    base = 4 * GROUP_W
    u_sc[HIST_ROW:HIST_ROW + T, :] = proj_ref[0, :, base:base + 3 * GROUP_W].astype(F32)
    for blk in range(3 * N_HEADS):
        cols = slice(blk * HEAD_DIM, (blk + 1) * HEAD_DIM)
        y = convw_ref[0:1, cols] * u_sc[HIST_ROW - 3:HIST_ROW - 3 + T, cols]
        for t in range(1, CONV_TAPS):
            y = y + convw_ref[t:t + 1, cols] * u_sc[HIST_ROW - 3 + t:HIST_ROW - 3 + t + T, cols]
        y = y * _sigmoid(y)
        kind, h = divmod(blk, N_HEADS)
        if kind == 0:
            qn_sc[h] = (y * lax.rsqrt(jnp.sum(y * y, axis=-1, keepdims=True) + EPS) * scale).astype(BF16)
        elif kind == 1:
            kf_sc[h] = y * lax.rsqrt(jnp.sum(y * y, axis=-1, keepdims=True) + EPS)
        else:
            v_sc[h] = y
    t_valid = T if valid == L else valid
    u_sc[HIST_ROW - 3:HIST_ROW, :] = u_sc[HIST_ROW + t_valid - 3:HIST_ROW + t_valid, :]

    def rows(c):
        return slice(c * L, (c + 1) * L)

    def a_q(c, h):
        return proj_ref[0, rows(c), h * HEAD_DIM:(h + 1) * HEAD_DIM]

    def a_k(c, h):
        return proj_ref[0, rows(c), GROUP_W + h * HEAD_DIM:GROUP_W + (h + 1) * HEAD_DIM]

    def a_v(c, h):
        return proj_ref[0, rows(c), 2 * GROUP_W + h * HEAD_DIM:2 * GROUP_W + (h + 1) * HEAD_DIM]

    ch = [(c, h) for c in chunks for h in heads]
    idx = {p: i for i, p in enumerate(ch)}

    qk_a = {p: lax.dot_general(a_q(*p), two(a_k(*p)), _NT, preferred_element_type=F32) for p in ch}
    qk_b = {}
    for c, h in ch:
        kn = kf_sc[h, rows(c), :].astype(BF16)
        qk_b[c, h] = lax.dot_general(jnp.concatenate([qn_sc[h, rows(c), :], kn], axis=0), two(kn), _NT,
                                     preferred_element_type=F32)

    mi, rs, dn, s_a, kw_a = {}, {}, {}, {}, {}
    for c, h in ch:
        ig_col, b_col = z[c][:, h:h + 1], z[c][:, 4 + h:5 + h]
        ig_row, b_row = zt[c][h:h + 1, :], zt[c][4 + h:5 + h, :]
        d = jnp.where(tri_incl, b_col - b_row + ig_row, NEG_BIG)
        mi[c, h] = jnp.max(d, axis=1, keepdims=True)
        s = jnp.where(left, qk_a[c, h] * scale * jnp.exp(d - mi[c, h]), 0.0)
        rs[c, h] = jnp.sum(s, axis=1, keepdims=True)
        s_a[c, h] = s.astype(BF16)
        w_s = jnp.exp(b_col[L - 1:L, :] - b_col + ig_col - mi[c, h][L - 1:L, :])
        kw = a_k(c, h).astype(F32) * scale * w_s
        dn[c, h] = jnp.sum(kw, axis=0, keepdims=True)
        kw_a[c, h] = kw.astype(BF16)
    for p in ch:
        v = a_v(*p)
        sv_sc[idx[p]] = _dot(s_a[p], two(v))
        dc_sc[idx[p]] = lax.dot_general(kw_a[p], v, _TN, preferred_element_type=F32)

    decay, r_b = {}, {}
    for c, h in ch:
        g_col, g_row = z[c][:, 8 + h:9 + h], zt[c][8 + h:9 + h, :]
        decay[c, h] = jnp.exp(jnp.where(tri_incl, g_col - g_row, NEG_BIG))
        a2 = jnp.where(tri_strict, z[c][:, 12 + h:13 + h] * decay[c, h] * qk_b[c, h][L:, :], 0.0)
        r_b[c, h] = jnp.where(left, eye_left, -a2)
    for _ in range(int(math.log2(L))):
        for p in ch:
            r = r_b[p]
            out = _dot(jnp.where(left, 0.0, r).astype(BF16), two(r.astype(BF16)))
            r_b[p] = jnp.where(left, r + out, out)
    uk = {}
    for c, h in ch:
        g_col, beta = z[c][:, 8 + h:9 + h], z[c][:, 12 + h:13 + h]
        rhs = jnp.concatenate([beta * v_sc[h, rows(c), :],
                               beta * jnp.exp(g_col) * kf_sc[h, rows(c), :]], axis=1).astype(BF16)
        uk[c, h] = _dot(jnp.where(left, r_b[c, h], 0.0).astype(BF16), two(rhs)).astype(BF16)
    egl = {}
    for c, h in ch:
        g_col = z[c][:, 8 + h:9 + h]
        g_last = g_col[L - 1:L, :]
        egl[c, h] = jnp.exp(g_last)
        qkd = jnp.where(left, qk_b[c, h][:L, :] * decay[c, h], 0.0).astype(BF16)
        qu = _dot(qkd, two(uk[c, h]))
        kd = (kf_sc[h, rows(c), :] * jnp.exp(g_last - g_col)).astype(BF16)
        ku = lax.dot_general(kd, uk[c, h], _TN, preferred_element_type=F32)
        q_eff = jnp.exp(g_col) * qn_sc[h, rows(c), :].astype(F32) - qu[:, HEAD_DIM:]
        i = idx[c, h]
        qu_sc[i] = qu[:, :HEAD_DIM]
        nc_sc[i] = ku[:, :HEAD_DIM]
        lb_sc[i] = jnp.concatenate([ku[:, HEAD_DIM:], q_eff], axis=0).astype(BF16)

    for c in chunks:
        qc, rm = {}, {}
        for h in heads:
            qc[h] = _dot(a_q(c, h), c_sc[h].astype(BF16))
        for h in heads:
            rm[h] = _dot(lb_sc[idx[c, h]], s_sc[h].astype(BF16))
        for h in heads:
            lo = h * HEAD_DIM
            i = idx[c, h]
            b_col = z[c][:, 4 + h:5 + h]
            m0 = m_sc[0:1, h:h + 1]
            inter = b_col + m0
            m = jnp.maximum(inter, mi[c, h])
            w_inter = jnp.exp(inter - m)
            w_intra = jnp.exp(mi[c, h] - m)
            n_prev = n_sc[h:h + 1, :]
            num = w_inter * qc[h] + w_intra * sv_sc[i]
            den = (w_inter * jnp.sum(a_q(c, h).astype(F32) * n_prev, axis=-1, keepdims=True)
                   + w_intra * rs[c, h])
            hh = num / jnp.maximum(jnp.abs(den), jnp.exp(-m))
            og = proj_ref[0, rows(c), 3 * GROUP_W + lo:3 * GROUP_W + lo + HEAD_DIM].astype(F32)
            h_ref[0, rows(c), lo:lo + HEAD_DIM] = (_sigmoid(og) * _rms(hh, anw_ref[...])).astype(BF16)
            m_last = m[L - 1:L, :]
            decay0 = jnp.exp(b_col[L - 1:L, :] + m0 - m_last)
            f_new = jnp.exp(mi[c, h][L - 1:L, :] - m_last)
            c_sc[h] = decay0 * c_sc[h] + f_new * dc_sc[i]
            n_sc[h:h + 1, :] = decay0 * n_prev + f_new * dn[c, h]
            m_sc[0:1, h:h + 1] = m_last
        for h in heads:
            lo = h * HEAD_DIM
            i = idx[c, h]
            o = rm[h][HEAD_DIM:, :] + qu_sc[i]
            s_sc[h] = egl[c, h] * s_sc[h] + nc_sc[i] - rm[h][:HEAD_DIM, :]
            zg = proj_ref[0, rows(c), base + 3 * GROUP_W + lo:base + 3 * GROUP_W + lo + HEAD_DIM].astype(F32)
            h_ref[0, rows(c), GROUP_W + lo:GROUP_W + lo + HEAD_DIM] = (
                _rms(o, bnw_ref[...]) * (zg * _sigmoid(zg))).astype(BF16)

    @pl.when(j == pl.num_programs(1) - 1)
    def _store_state():
        c1_ref[0] = c_sc[...]
        n1_ref[0] = n_sc[0:N_HEADS, :]
        m1_ref[0] = m_sc[0:1, 0:N_HEADS]
        s1_ref[0] = s_sc[...]
        conv1_ref[0] = u_sc[HIST_ROW - 3:HIST_ROW, :]


def _mixers(proj, gates, c0, n0, m0, s0, conv0, gate_par, conv_w, a_norm_w, b_norm_w, *, valid, tile):
    bsz, t, n = proj.shape
    tile = min(tile, t)
    assert t % tile == 0 and tile % CHUNK == 0
    assert valid == CHUNK or (t == CHUNK and CONV_TAPS - 1 <= valid < CHUNK)
    n_chunks = tile // CHUNK
    cw = 3 * GROUP_W
    bj = lambda b, j: (b, j, 0)
    st4 = lambda b, j: (b, 0, 0, 0)
    st3 = lambda b, j: (b, 0, 0)
    cst = lambda b, j: (0, 0)
    state_specs = [
        pl.BlockSpec((1, N_HEADS, HEAD_DIM, HEAD_DIM), st4),
        pl.BlockSpec((1, N_HEADS, HEAD_DIM), st3),
        pl.BlockSpec((1, 1, N_HEADS), st3),
        pl.BlockSpec((1, N_HEADS, HEAD_DIM, HEAD_DIM), st4),
        pl.BlockSpec((1, CONV_TAPS - 1, cw), st3),
    ]
    state_shapes = [
        jax.ShapeDtypeStruct((bsz, N_HEADS, HEAD_DIM, HEAD_DIM), F32),
        jax.ShapeDtypeStruct((bsz, N_HEADS, HEAD_DIM), F32),
        jax.ShapeDtypeStruct((bsz, 1, N_HEADS), F32),
        jax.ShapeDtypeStruct((bsz, N_HEADS, HEAD_DIM, HEAD_DIM), F32),
        jax.ShapeDtypeStruct((bsz, CONV_TAPS - 1, cw), F32),
    ]
    return pl.pallas_call(
        functools.partial(_mixer_body, n_chunks=n_chunks, valid=valid),
        grid=(bsz, t // tile),
        in_specs=[
            pl.BlockSpec((1, tile, n), bj),
            pl.BlockSpec((1, tile, GATE_LANES), bj),
            *state_specs,
            pl.BlockSpec((8, GATE_LANES), cst),
            pl.BlockSpec((CONV_TAPS, cw), cst),
            pl.BlockSpec((1, HEAD_DIM), cst),
            pl.BlockSpec((1, HEAD_DIM), cst),
        ],
        out_specs=[pl.BlockSpec((1, tile, 2 * GROUP_W), bj), *state_specs],
        out_shape=[jax.ShapeDtypeStruct((bsz, t, 2 * GROUP_W), BF16), *state_shapes],
        scratch_shapes=[
            pltpu.VMEM((N_HEADS, HEAD_DIM, HEAD_DIM), F32),
            pltpu.VMEM((8, HEAD_DIM), F32),
            pltpu.VMEM((8, GATE_LANES), F32),
            pltpu.VMEM((N_HEADS, HEAD_DIM, HEAD_DIM), F32),
            pltpu.VMEM((HIST_ROW + tile, cw), F32),
            pltpu.VMEM((N_HEADS, tile, HEAD_DIM), BF16),
            pltpu.VMEM((N_HEADS, tile, HEAD_DIM), F32),
            pltpu.VMEM((N_HEADS, tile, HEAD_DIM), F32),
            pltpu.VMEM((n_chunks * N_HEADS, HEAD_DIM, HEAD_DIM), F32),
            pltpu.VMEM((n_chunks * N_HEADS, CHUNK, HEAD_DIM), F32),
            pltpu.VMEM((n_chunks * N_HEADS, HEAD_DIM, HEAD_DIM), F32),
            pltpu.VMEM((n_chunks * N_HEADS, HEAD_DIM + CHUNK, HEAD_DIM), BF16),
            pltpu.VMEM((n_chunks * N_HEADS, CHUNK, HEAD_DIM), F32),
        ],
        compiler_params=_params(("arbitrary", "arbitrary")),
        name="mixers",
    )(proj, gates, c0, n0, m0, s0, conv0, gate_par, conv_w, a_norm_w, b_norm_w)


def _mem_kv_body(mem_ref, nw_ref, wkv_ref, k_ref, v_ref, kb_ref, vb_ref):
    mn = _rms(mem_ref[...], nw_ref[...]).astype(BF16)
    d = k_ref.shape[1]
    k = _dot(mn, wkv_ref[:, 0:d])
    v = _dot(mn, wkv_ref[:, d:2 * d])
    k_ref[...] = k
    v_ref[...] = v
    kb_ref[...] = k.astype(BF16)
    vb_ref[...] = v.astype(BF16)


def _mem_kv(mem2d, norm_w, wkv):
    rows, d = mem2d.shape
    tm = min(ROW_TILE, rows)
    row = lambda i: (i, 0)
    cst = lambda i: (0, 0)
    return pl.pallas_call(
        _mem_kv_body,
        grid=(rows // tm,),
        in_specs=[pl.BlockSpec((tm, d), row), pl.BlockSpec((1, d), cst), pl.BlockSpec((d, 2 * d), cst)],
        out_specs=[pl.BlockSpec((tm, d), row)] * 4,
        out_shape=[jax.ShapeDtypeStruct((rows, d), F32)] * 2 + [jax.ShapeDtypeStruct((rows, d), BF16)] * 2,
        compiler_params=_params(("arbitrary",)),
        name="mem_kv",
    )(mem2d, norm_w, wkv)


def _attn_body(h_ref, x_ref, wout_ref, nx_ref, wq_ref, mk_ref, mv_ref, wo_ref, o_ref):
    d = x_ref.shape[2]
    dh = d // N_XHEADS
    x1 = x_ref[0] + _dot(h_ref[0], wout_ref[...])
    xn = _rms(x1, nx_ref[...]).astype(BF16)
    q = _dot(xn, wq_ref[...]).astype(BF16)
    acc = x1
    for h in range(N_XHEADS):
        cols = slice(h * dh, (h + 1) * dh)
        s = lax.dot_general(q[:, cols], mk_ref[0, :, cols], _NT, preferred_element_type=F32) * (dh ** -0.5)
        e = jnp.exp(s - jnp.max(s, axis=-1, keepdims=True))
        p = (e / jnp.sum(e, axis=-1, keepdims=True)).astype(BF16)
        oh = _dot(p, mv_ref[0, :, cols]).astype(BF16)
        acc = acc + _dot(oh, wo_ref[cols, :])
    o_ref[0] = acc


def _out_proj_attn(hcat, x, w_out, norm_w, wq, mk, mv, wo):
    bsz, t, d = x.shape
    n_mem = mk.shape[1]
    tm = min(ROW_TILE, t)
    bj = lambda b, j: (b, j, 0)
    cst = lambda b, j: (0, 0)
    mem = lambda b, j: (b, 0, 0)
    return pl.pallas_call(
        _attn_body,
        grid=(bsz, t // tm),
        in_specs=[
            pl.BlockSpec((1, tm, d), bj),
            pl.BlockSpec((1, tm, d), bj),
            pl.BlockSpec((d, d), cst),
            pl.BlockSpec((1, d), cst),
            pl.BlockSpec((d, d), cst),
            pl.BlockSpec((1, n_mem, d), mem),
            pl.BlockSpec((1, n_mem, d), mem),
            pl.BlockSpec((d, d), cst),
        ],
        out_specs=pl.BlockSpec((1, tm, d), bj),
        out_shape=jax.ShapeDtypeStruct((bsz, t, d), F32),
        compiler_params=_params(("arbitrary", "arbitrary")),
        name="out_proj_attn",
    )(hcat, x, w_out, norm_w, wq, mk, mv, wo)


def _ffn_body(x_ref, nf_ref, w1_ref, w2_ref, nfin_ref, o_ref, *, hid_tile, final_norm):
    x = x_ref[...]
    xn = _rms(x, nf_ref[...]).astype(BF16)
    acc = x
    for c0 in range(0, w1_ref.shape[1], hid_tile):
        hdn = jnp.maximum(_dot(xn, w1_ref[:, c0:c0 + hid_tile]), 0.0)
        acc = acc + _dot((hdn * hdn).astype(BF16), w2_ref[c0:c0 + hid_tile, :])
    o_ref[...] = _rms(acc, nfin_ref[...]) if final_norm else acc


def _ffn(x2d, norm_w, w1, w2, final_w, *, final_norm):
    rows, d = x2d.shape
    dff = w1.shape[1]
    tm = min(ROW_TILE, rows)
    row = lambda i: (i, 0)
    cst = lambda i: (0, 0)
    return pl.pallas_call(
        functools.partial(_ffn_body, hid_tile=1024, final_norm=final_norm),
        grid=(rows // tm,),
        in_specs=[
            pl.BlockSpec((tm, d), row),
            pl.BlockSpec((1, d), cst),
            pl.BlockSpec((d, dff), cst, pipeline_mode=pl.Buffered(1)),
            pl.BlockSpec((dff, d), cst, pipeline_mode=pl.Buffered(1)),
            pl.BlockSpec((1, d), cst),
        ],
        out_specs=pl.BlockSpec((tm, d), row),
        out_shape=jax.ShapeDtypeStruct((rows, d), F32),
        compiler_params=_params(("arbitrary",)),
        name="ffn",
    )(x2d, norm_w, w1, w2, final_w)


def _split_w_in(w_in):
    w = GROUP_W
    g0 = 4 * w
    b0 = g0 + 2 * N_HEADS
    g1 = b0 + 4 * w
    main = jnp.concatenate([w_in[:, :g0], w_in[:, b0:g1]], axis=1).astype(BF16)
    gate = jnp.concatenate([w_in[:, g0:b0], w_in[:, g1:g1 + 2 * N_HEADS]], axis=1)
    gate = jnp.pad(gate, ((0, 0), (0, GATE_LANES - 4 * N_HEADS))).astype(BF16)
    return main, gate


def _gate_params(igate_b, fgate_b, a_log, dt_bias):
    zeros = jnp.zeros((N_HEADS,), F32)
    bias = jnp.concatenate([igate_b, fgate_b, dt_bias, zeros]).astype(F32)
    alog = jnp.concatenate([zeros, zeros, a_log.astype(F32), zeros])
    par = jnp.zeros((8, GATE_LANES), F32)
    return par.at[0, :4 * N_HEADS].set(bias).at[1, :4 * N_HEADS].set(alog)


def _trunk_layer(x, mem_k, mem_v, c0, n0, m0, s0, conv0, *, valid, w_main, w_gate, gate_par, lw, final_w,
                 final_norm):
    bsz, t, d = x.shape
    tp = CHUNK if valid < CHUNK else t
    xp = x if tp == t else jnp.pad(x, ((0, 0), (0, tp - t), (0, 0)))
    proj, gates = _in_proj(xp.reshape(bsz * tp, d), lw["norm_mix_w"], w_main, w_gate)
    hcat, c1, n1, m1, s1, conv1 = _mixers(
        proj.reshape(bsz, tp, -1), gates.reshape(bsz, tp, GATE_LANES),
        c0, n0, m0.reshape(bsz, 1, N_HEADS), s0, conv0,
        gate_par, lw["gdn_conv_w"], lw["mlstm_norm_w"], lw["gdn_norm_w"], valid=valid, tile=256)
    x2 = _out_proj_attn(hcat, x, lw["w_out"], lw["norm_x_w"], lw["wq_x"], mem_k, mem_v, lw["wo_x"])
    y = _ffn(x2.reshape(bsz * t, d), lw["norm_ffn_w"], lw["w_ff1"], lw["w_ff2"], final_w,
             final_norm=final_norm)
    return y.reshape(bsz, t, d), c1, n1, m1.reshape(bsz, N_HEADS), s1, conv1


def kernel(x_prompt, x_sample, state_mlstm_C, state_mlstm_n, state_mlstm_m, state_gdn_S, state_gdn_conv, cache_mem_k, cache_mem_v, mem_prompt, norm_mix_w, w_in, mlstm_igate_b, mlstm_fgate_b, mlstm_norm_w, gdn_conv_w, gdn_A_log, gdn_dt_bias, gdn_norm_w, w_out, norm_x_w, norm_mem_w, wq_x, wk_x, wv_x, wo_x, norm_ffn_w, w_ff1, w_ff2, norm_final_w):
    depth = w_in.shape[0]
    bp, _, d = x_prompt.shape
    bs, ts, _ = x_sample.shape
    n_mem = mem_prompt.shape[1]
    row = lambda a: a.reshape(1, -1).astype(F32)
    hp, hs = x_prompt, x_sample
    outs_p = [[] for _ in range(7)]
    outs_s = [[] for _ in range(5)]
    for l in range(depth):
        w_main, w_gate = _split_w_in(w_in[l])
        gate_par = _gate_params(mlstm_igate_b[l], mlstm_fgate_b[l], gdn_A_log[l], gdn_dt_bias[l])
        lw = dict(
            norm_mix_w=row(norm_mix_w[l]), gdn_conv_w=gdn_conv_w[l].astype(F32),
            mlstm_norm_w=row(mlstm_norm_w[l]), gdn_norm_w=row(gdn_norm_w[l]),
            w_out=w_out[l].astype(BF16), norm_x_w=row(norm_x_w[l]), wq_x=wq_x[l].astype(BF16),
            wo_x=wo_x[l].astype(BF16), norm_ffn_w=row(norm_ffn_w[l]),
            w_ff1=w_ff1[l].astype(BF16), w_ff2=w_ff2[l].astype(BF16))
        common = dict(w_main=w_main, w_gate=w_gate, gate_par=gate_par, lw=lw, final_w=row(norm_final_w),
                      final_norm=(l == depth - 1))
        wkv = jnp.concatenate([wk_x[l], wv_x[l]], axis=1).astype(BF16)
        mk, mv, mk_b, mv_b = _mem_kv(mem_prompt.reshape(bp * n_mem, d), row(norm_mem_w[l]), wkv)
        hp, c1, n1, m1, s1, cv1 = _trunk_layer(
            hp, mk_b.reshape(bp, n_mem, d), mv_b.reshape(bp, n_mem, d),
            jnp.zeros((bp, N_HEADS, HEAD_DIM, HEAD_DIM), F32), jnp.zeros((bp, N_HEADS, HEAD_DIM), F32),
            jnp.zeros((bp, N_HEADS), F32), jnp.zeros((bp, N_HEADS, HEAD_DIM, HEAD_DIM), F32),
            jnp.zeros((bp, CONV_TAPS - 1, 3 * GROUP_W), F32), valid=CHUNK, **common)
        for acc, val in zip(outs_p, (c1, n1, m1, s1, cv1,
                                     mk.reshape(bp, n_mem, N_XHEADS, d // N_XHEADS),
                                     mv.reshape(bp, n_mem, N_XHEADS, d // N_XHEADS))):
            acc.append(val)
        hs, c2, n2, m2, s2, cv2 = _trunk_layer(
            hs, cache_mem_k[l].reshape(bs, n_mem, d).astype(BF16), cache_mem_v[l].reshape(bs, n_mem, d).astype(BF16),
            state_mlstm_C[l], state_mlstm_n[l], state_mlstm_m[l], state_gdn_S[l], state_gdn_conv[l],
            valid=ts, **common)
        for acc, val in zip(outs_s, (c2, n2, m2, s2, cv2)):
            acc.append(val)
    return (hp, hs, *[jnp.stack(a) for a in outs_p], *[jnp.stack(a) for a in outs_s])
```

```python
import functools
import math

import jax
import jax.numpy as jnp
from jax import lax
from jax.experimental import pallas as pl
from jax.experimental.pallas import tpu as pltpu

F32 = jnp.float32
BF16 = jnp.bfloat16
EPS = 1e-6
N_HEADS = 4
HEAD_DIM = 128
GROUP_W = N_HEADS * HEAD_DIM
CONV_TAPS = 4
N_XHEADS = 4
CHUNK = 64
GATE_LANES = 128
NEG_BIG = -1e30
HIST_ROW = 8

VMEM_LIMIT_BYTES = 56 * 1024 * 1024
ROW_TILE = 512
MIXER_TILE = 256

_NT = (((1,), (1,)), ((), ()))
_TN = (((0,), (0,)), ((), ()))


def _rms(x, w):
    return x * lax.rsqrt(jnp.mean(x * x, axis=-1, keepdims=True) + EPS) * w


def _dot(a, b):
    return jnp.dot(a, b, preferred_element_type=F32)


def _sigmoid(x):
    return 1.0 / (1.0 + jnp.exp(-x))


def _params(sem):
    return pltpu.CompilerParams(dimension_semantics=sem, vmem_limit_bytes=VMEM_LIMIT_BYTES)


def _in_proj_body(x_ref, nw_ref, w_ref, wg_ref, proj_ref, gate_ref, *, col_tile):
    xn = _rms(x_ref[...], nw_ref[...]).astype(BF16)
    for c0 in range(0, w_ref.shape[1], col_tile):
        proj_ref[:, c0:c0 + col_tile] = _dot(xn, w_ref[:, c0:c0 + col_tile]).astype(BF16)
    gate_ref[...] = lax.dot_general(wg_ref[...], xn, _NT, preferred_element_type=F32)


def _in_proj(x2d, norm_w, w_main, w_gate):
    rows, d = x2d.shape
    n = w_main.shape[1]
    tm = min(ROW_TILE, rows)
    assert rows % tm == 0
    return pl.pallas_call(
        functools.partial(_in_proj_body, col_tile=512),
        grid=(rows // tm,),
        in_specs=[
            pl.BlockSpec((tm, d), lambda i: (i, 0)),
            pl.BlockSpec((1, d), lambda i: (0, 0)),
            pl.BlockSpec((d, n), lambda i: (0, 0)),
            pl.BlockSpec((4 * N_HEADS, d), lambda i: (0, 0)),
        ],
        out_specs=[
            pl.BlockSpec((tm, n), lambda i: (i, 0)),
            pl.BlockSpec((4 * N_HEADS, tm), lambda i: (0, i)),
        ],
        out_shape=[
            jax.ShapeDtypeStruct((rows, n), BF16),
            jax.ShapeDtypeStruct((4 * N_HEADS, rows), F32),
        ],
        compiler_params=_params(("arbitrary",)),
        name="in_proj",
    )(x2d, norm_w, w_main, w_gate)


def _mixer_body(proj_ref, gate_ref, c0_ref, n0_ref, m0_ref, s0_ref, conv0_ref,
                gpar_ref, convw_ref, anw_ref, bnw_ref,
                h_ref, c1_ref, n1_ref, m1_ref, s1_ref, conv1_ref,
                c_sc, n_sc, m_sc, s_sc, u_sc, qn_sc, kf_sc, v_sc,
                z_sc, st_sc, dn_sc, q_sc, og_sc, zg_sc, dc_sc, sv_sc, nc_sc, lb_sc, qu_sc,
                *, n_chunks, valid, pipelined):
    L = CHUNK
    T = n_chunks * L
    n_ch = n_chunks * N_HEADS
    j = pl.program_id(1)
    last = pl.num_programs(1) - 1
    scale = HEAD_DIM ** -0.5
    heads = range(N_HEADS)
    chunks = range(n_chunks)
    slot_a = (j & 1) if pipelined else 0
    slot_b = (1 - slot_a) if pipelined else 0
    handoff = (z_sc, st_sc, dn_sc, q_sc, og_sc, zg_sc, dc_sc, sv_sc, nc_sc, lb_sc, qu_sc)

    def load_state():
        c_sc[...] = c0_ref[0]
        for h in heads:
            n_sc[h, 0:1, :] = n0_ref[0, h:h + 1, :]
        m_sc[...] = jnp.zeros(m_sc.shape, F32)
        m_sc[0:1, N_HEADS:2 * N_HEADS] = m0_ref[0]
        s_sc[...] = s0_ref[0]

    if pipelined:
        @pl.when((pl.program_id(0) == 0) & (j == 0))
        def _zero_handoff():
            for ref in handoff:
                ref[...] = jnp.zeros(ref.shape, ref.dtype)

    @pl.when(j == 0)
    def _first_tile():
        load_state()
        u_sc[HIST_ROW - (CONV_TAPS - 1):HIST_ROW, :] = conv0_ref[0]

    lane = lax.broadcasted_iota(jnp.int32, (L, 2 * L), 1)
    tok = lax.broadcasted_iota(jnp.int32, (L, 2 * L), 0)
    src = lane & (L - 1)
    left = lane < L
    tri_incl = tok >= src
    tri_strict = tok > src
    eye_left = jnp.where(left & (tok == src), 1.0, 0.0)
    lane_row = lax.broadcasted_iota(jnp.int32, (1, GATE_LANES), 1)
    head_lanes = (lane_row >= N_HEADS) & (lane_row < 2 * N_HEADS)
    base = 4 * GROUP_W
    ch = [(c, h) for c in chunks for h in heads]
    idx = {p: i for i, p in enumerate(ch)}

    def two(x):
        return jnp.concatenate([x, x], axis=0)

    def rows(c):
        return slice(c * L, (c + 1) * L)

    def hcols(h, group=0):
        return slice(group * GROUP_W + h * HEAD_DIM, group * GROUP_W + (h + 1) * HEAD_DIM)

    def highest(a, b, dims=None):
        if dims is None:
            return jnp.dot(a, b, precision=lax.Precision.HIGHEST, preferred_element_type=F32)
        return lax.dot_general(a, b, dims, precision=lax.Precision.HIGHEST, preferred_element_type=F32)

    def phase_a():
        n_g = 4 * N_HEADS
        row_g = lax.broadcasted_iota(jnp.int32, (n_g, T), 0)
        pre = gate_ref[0] + gpar_ref[0]
        e = jnp.exp(-jnp.abs(pre))
        t = jnp.log1p(e)
        act = jnp.where(row_g < 4, pre,
                        jnp.where(row_g < 8, -(jnp.maximum(-pre, 0.0) + t),
                                  jnp.where(row_g < 12, -jnp.exp(gpar_ref[1]) * (jnp.maximum(pre, 0.0) + t),
                                            jnp.where(pre >= 0.0, 1.0, e) / (1.0 + e))))
        if valid < L:
            tok_g = lax.broadcasted_iota(jnp.int32, (n_g, T), 1)
            act = jnp.where(tok_g >= valid, jnp.where(row_g < 4, NEG_BIG, 0.0), act)
        r_t = lax.broadcasted_iota(jnp.int32, (T, T), 0)
        c_t = lax.broadcasted_iota(jnp.int32, (T, T), 1)
        shift = int(math.log2(L))
        within = jnp.where((r_t <= c_t) & ((r_t >> shift) == (c_t >> shift)), 1.0, 0.0)
        z_g = jnp.where((row_g >= 4) & (row_g < 12), highest(act, within), act)
        pick = jnp.where(lax.broadcasted_iota(jnp.int32, (n_g, GATE_LANES), 0)
                         == lax.broadcasted_iota(jnp.int32, (n_g, GATE_LANES), 1), 1.0, 0.0)
        z_t = highest(z_g, pick, _TN)
        z_sc[slot_a] = z_t
        z = [z_t[rows(c), :] for c in chunks]
        r_d = lax.broadcasted_iota(jnp.int32, (T, 2 * L), 0)
        c_d = lax.broadcasted_iota(jnp.int32, (T, 2 * L), 1)
        zt = [highest(z_g, jnp.where(r_d == c * L + (c_d & (L - 1)), 1.0, 0.0)) for c in chunks]
        q_sc[slot_a] = proj_ref[0, :, 0:GROUP_W]
        og_sc[slot_a] = proj_ref[0, :, 3 * GROUP_W:4 * GROUP_W]
        zg_sc[slot_a] = proj_ref[0, :, base + 3 * GROUP_W:base + 4 * GROUP_W]
        yield

        u_sc[HIST_ROW:HIST_ROW + T, :] = proj_ref[0, :, base:base + 3 * GROUP_W].astype(F32)
        for blk in range(3 * N_HEADS):
            cols = slice(blk * HEAD_DIM, (blk + 1) * HEAD_DIM)
            y = convw_ref[0:1, cols] * u_sc[HIST_ROW - 3:HIST_ROW - 3 + T, cols]
            for tap in range(1, CONV_TAPS):
                y = y + convw_ref[tap:tap + 1, cols] * u_sc[HIST_ROW - 3 + tap:HIST_ROW - 3 + tap + T, cols]
            y = y * _sigmoid(y)
            kind, h = divmod(blk, N_HEADS)
            if kind == 0:
                qn_sc[h] = (y * lax.rsqrt(jnp.sum(y * y, axis=-1, keepdims=True) + EPS) * scale).astype(BF16)
            elif kind == 1:
                kf_sc[h] = y * lax.rsqrt(jnp.sum(y * y, axis=-1, keepdims=True) + EPS)
            else:
                v_sc[h] = y
        t_valid = T if valid == L else valid
        u_sc[HIST_ROW - 3:HIST_ROW, :] = u_sc[HIST_ROW + t_valid - 3:HIST_ROW + t_valid, :]
        yield

        def a_k(c, h):
            return proj_ref[0, rows(c), hcols(h, 1)]

        qk_a = {(c, h): lax.dot_general(proj_ref[0, rows(c), hcols(h)], two(a_k(c, h)), _NT,
                                        preferred_element_type=F32) for c, h in ch}
        qk_b = {}
        for c, h in ch:
            kn = kf_sc[h, rows(c), :].astype(BF16)
            qk_b[c, h] = lax.dot_general(jnp.concatenate([qn_sc[h, rows(c), :], kn], axis=0), two(kn), _NT,
                                         preferred_element_type=F32)
        yield

        d_a, m_a, s_a, kw_a = {}, {}, {}, {}
        for c, h in ch:
            b_col = z[c][:, 4 + h:5 + h]
            ig_row, b_row = zt[c][h:h + 1, :], zt[c][4 + h:5 + h, :]
            d_a[c, h] = jnp.where(tri_incl, b_col - b_row + ig_row, NEG_BIG)
        for c, h in ch:
            m_a[c, h] = jnp.max(d_a[c, h], axis=1, keepdims=True)
            st_sc[slot_a, rows(c), 4 + h:5 + h] = m_a[c, h]
        yield
        for c, h in ch:
            s_a[c, h] = jnp.where(left, qk_a[c, h] * scale * jnp.exp(d_a[c, h] - m_a[c, h]), 0.0)
        for c, h in ch:
            st_sc[slot_a, rows(c), h:h + 1] = jnp.sum(s_a[c, h], axis=1, keepdims=True)
        for c, h in ch:
            ig_col, b_col = z[c][:, h:h + 1], z[c][:, 4 + h:5 + h]
            w_s = jnp.exp(b_col[L - 1:L, :] - b_col + ig_col - m_a[c, h][L - 1:L, :])
            kw = a_k(c, h).astype(F32) * scale * w_s
            dn_sc[slot_a, idx[c, h]:idx[c, h] + 1, :] = jnp.sum(kw, axis=0, keepdims=True)
            kw_a[c, h] = kw.astype(BF16)
        yield
        for c, h in ch:
            v = proj_ref[0, rows(c), hcols(h, 2)]
            sv_sc[slot_a * n_ch + idx[c, h]] = _dot(s_a[c, h].astype(BF16), two(v))
            dc_sc[slot_a * n_ch + idx[c, h]] = lax.dot_general(kw_a[c, h], v, _TN, preferred_element_type=F32)
        yield

        decay, r_b = {}, {}
        for c, h in ch:
            g_col, g_row = z[c][:, 8 + h:9 + h], zt[c][8 + h:9 + h, :]
            decay[c, h] = jnp.exp(jnp.where(tri_incl, g_col - g_row, NEG_BIG))
            a2 = jnp.where(tri_strict, z[c][:, 12 + h:13 + h] * decay[c, h] * qk_b[c, h][L:, :], 0.0)
            r_b[c, h] = jnp.where(left, eye_left, -a2)
        yield
        for _ in range(int(math.log2(L))):
            for p in ch:
                r = r_b[p]
                out = _dot(jnp.where(left, 0.0, r).astype(BF16), two(r.astype(BF16)))
                r_b[p] = jnp.where(left, r + out, out)
            yield
        uk = {}
        for c, h in ch:
            g_col, beta = z[c][:, 8 + h:9 + h], z[c][:, 12 + h:13 + h]
            rhs = jnp.concatenate([beta * v_sc[h, rows(c), :],
                                   beta * jnp.exp(g_col) * kf_sc[h, rows(c), :]], axis=1).astype(BF16)
            uk[c, h] = _dot(jnp.where(left, r_b[c, h], 0.0).astype(BF16), two(rhs)).astype(BF16)
        yield
        for c, h in ch:
            g_col = z[c][:, 8 + h:9 + h]
            g_last = g_col[L - 1:L, :]
            qkd = jnp.where(left, qk_b[c, h][:L, :] * decay[c, h], 0.0).astype(BF16)
            qu = _dot(qkd, two(uk[c, h]))
            kd = (kf_sc[h, rows(c), :] * jnp.exp(g_last - g_col)).astype(BF16)
            ku = lax.dot_general(kd, uk[c, h], _TN, preferred_element_type=F32)
            q_eff = jnp.exp(g_col) * qn_sc[h, rows(c), :].astype(F32) - qu[:, HEAD_DIM:]
            i = slot_a * n_ch + idx[c, h]
            qu_sc[i] = qu[:, :HEAD_DIM]
            nc_sc[i] = ku[:, :HEAD_DIM]
            lb_sc[i] = jnp.concatenate([ku[:, HEAD_DIM:], q_eff], axis=0).astype(BF16)
        yield

    def phase_b():
        for c in chunks:
            zc = z_sc[slot_b, rows(c), :]
            stc = st_sc[slot_b, rows(c), :]
            base_i = slot_b * n_ch + c * N_HEADS
            c_prev = [c_sc[h] for h in heads]
            s_prev = [s_sc[h] for h in heads]
            n_prev = [n_sc[h, 0:1, :] for h in heads]
            qc = [_dot(q_sc[slot_b, rows(c), hcols(h)], c_prev[h].astype(BF16)) for h in heads]
            rm = [_dot(lb_sc[base_i + h], s_prev[h].astype(BF16)) for h in heads]
            m_prev = m_sc[0:1, :]
            inter = zc + m_prev
            m = jnp.maximum(inter, stc)
            m_last = jnp.where(head_lanes, m[L - 1:L, :], 0.0)
            decay0_t = jnp.exp(zc[L - 1:L, :] + m_prev - m_last)
            f_new_t = jnp.exp(stc[L - 1:L, :] - m_last)
            m_sc[0:1, :] = m_last
            for h in heads:
                hl = slice(N_HEADS + h, N_HEADS + h + 1)
                c_sc[h] = decay0_t[:, hl] * c_prev[h] + f_new_t[:, hl] * dc_sc[base_i + h]
                n_sc[h, 0:1, :] = (decay0_t[:, hl] * n_prev[h]
                                   + f_new_t[:, hl] * dn_sc[slot_b, c * N_HEADS + h:c * N_HEADS + h + 1, :])
            for h in heads:
                s_sc[h] = jnp.exp(zc[L - 1:L, 8 + h:9 + h]) * s_prev[h] + nc_sc[base_i + h] - rm[h][:HEAD_DIM, :]
            yield
            w_inter_t = jnp.exp(inter - m)
            w_intra_t = jnp.exp(stc - m)
            inv_cap_t = jnp.exp(-m)
            qn = [jnp.sum(q_sc[slot_b, rows(c), hcols(h)].astype(F32) * n_prev[h], axis=-1, keepdims=True)
                  for h in heads]
            hh, o_b = [], []
            for h in heads:
                hl = slice(N_HEADS + h, N_HEADS + h + 1)
                w_inter, w_intra = w_inter_t[:, hl], w_intra_t[:, hl]
                num = w_inter * qc[h] + w_intra * sv_sc[base_i + h]
                den = w_inter * qn[h] + w_intra * stc[:, h:h + 1]
                hh.append(num / jnp.maximum(jnp.abs(den), inv_cap_t[:, hl]))
                o_b.append(rm[h][HEAD_DIM:, :] + qu_sc[base_i + h])
            ms_a = [jnp.mean(x * x, axis=-1, keepdims=True) for x in hh]
            ms_b = [jnp.mean(x * x, axis=-1, keepdims=True) for x in o_b]
            for h in heads:
                og = og_sc[slot_b, rows(c), hcols(h)].astype(F32)
                hn = hh[h] * lax.rsqrt(ms_a[h] + EPS) * anw_ref[...]
                h_ref[0, rows(c), hcols(h)] = (_sigmoid(og) * hn).astype(BF16)
            for h in heads:
                zg = zg_sc[slot_b, rows(c), hcols(h)].astype(F32)
                on = o_b[h] * lax.rsqrt(ms_b[h] + EPS) * bnw_ref[...]
                h_ref[0, rows(c), hcols(h, 1)] = (on * (zg * _sigmoid(zg))).astype(BF16)
            yield

    steps_a, steps_b = 9 + int(math.log2(L)), 2 * n_chunks
    gen_a, gen_b = phase_a(), phase_b()
    if pipelined:
        done_b = 0
        for k in range(steps_a):
            while done_b < steps_b and done_b * steps_a <= k * steps_b:
                next(gen_b)
                done_b += 1
            next(gen_a)
        for _ in range(steps_b - done_b):
            next(gen_b)
        assert next(gen_a, None) is None and next(gen_b, None) is None
    else:
        for _ in gen_a:
            pass
        for _ in gen_b:
            pass

    if pipelined:
        @pl.when(j == 0)
        def _discard_phase_b():
            load_state()

    @pl.when(j == last)
    def _store_state():
        c1_ref[0] = c_sc[...]
        for h in heads:
            n1_ref[0, h:h + 1, :] = n_sc[h, 0:1, :]
        m1_ref[0] = m_sc[0:1, N_HEADS:2 * N_HEADS]
        s1_ref[0] = s_sc[...]
        conv1_ref[0] = u_sc[HIST_ROW - 3:HIST_ROW, :]


def _mixers(proj, gates, c0, n0, m0, s0, conv0, gate_par, conv_w, a_norm_w, b_norm_w, *, valid, tile):
    bsz, t, n = proj.shape
    tile = min(tile, t)
    assert t % tile == 0 and tile % CHUNK == 0
    assert valid == CHUNK or (t == CHUNK and CONV_TAPS - 1 <= valid < CHUNK)
    n_chunks = tile // CHUNK
    n_tiles = t // tile
    pipelined = n_tiles > 1
    slots = 2 if pipelined else 1
    n_ch = n_chunks * N_HEADS
    cw = 3 * GROUP_W
    if pipelined:
        a_tile = lambda j: jnp.minimum(j, n_tiles - 1)
        b_tile = lambda j: jnp.maximum(j - 1, 0)
    else:
        a_tile = b_tile = lambda j: j
    st4 = lambda b, j: (b, 0, 0, 0)
    st3 = lambda b, j: (b, 0, 0)
    cst2 = lambda b, j: (0, 0)
    cst3 = lambda b, j: (0, 0, 0)
    state_specs = [
        pl.BlockSpec((1, N_HEADS, HEAD_DIM, HEAD_DIM), st4),
        pl.BlockSpec((1, N_HEADS, HEAD_DIM), st3),
        pl.BlockSpec((1, 1, N_HEADS), st3),
        pl.BlockSpec((1, N_HEADS, HEAD_DIM, HEAD_DIM), st4),
        pl.BlockSpec((1, CONV_TAPS - 1, cw), st3),
    ]
    state_shapes = [
        jax.ShapeDtypeStruct((bsz, N_HEADS, HEAD_DIM, HEAD_DIM), F32),
        jax.ShapeDtypeStruct((bsz, N_HEADS, HEAD_DIM), F32),
        jax.ShapeDtypeStruct((bsz, 1, N_HEADS), F32),
        jax.ShapeDtypeStruct((bsz, N_HEADS, HEAD_DIM, HEAD_DIM), F32),
        jax.ShapeDtypeStruct((bsz, CONV_TAPS - 1, cw), F32),
    ]
    return pl.pallas_call(
        functools.partial(_mixer_body, n_chunks=n_chunks, valid=valid, pipelined=pipelined),
        grid=(bsz, n_tiles + (1 if pipelined else 0)),
        in_specs=[
            pl.BlockSpec((1, tile, n), lambda b, j: (b, a_tile(j), 0)),
            pl.BlockSpec((1, 4 * N_HEADS, tile), lambda b, j: (b, 0, a_tile(j))),
            *state_specs,
            pl.BlockSpec((2, 4 * N_HEADS, tile), cst3),
            pl.BlockSpec((CONV_TAPS, cw), cst2),
            pl.BlockSpec((1, HEAD_DIM), cst2),
            pl.BlockSpec((1, HEAD_DIM), cst2),
        ],
        out_specs=[pl.BlockSpec((1, tile, 2 * GROUP_W), lambda b, j: (b, b_tile(j), 0)), *state_specs],
        out_shape=[jax.ShapeDtypeStruct((bsz, t, 2 * GROUP_W), BF16), *state_shapes],
        scratch_shapes=[
            pltpu.VMEM((N_HEADS, HEAD_DIM, HEAD_DIM), F32),
            pltpu.VMEM((N_HEADS, 8, HEAD_DIM), F32),
            pltpu.VMEM((8, GATE_LANES), F32),
            pltpu.VMEM((N_HEADS, HEAD_DIM, HEAD_DIM), F32),
            pltpu.VMEM((HIST_ROW + tile, cw), F32),
            pltpu.VMEM((N_HEADS, tile, HEAD_DIM), BF16),
            pltpu.VMEM((N_HEADS, tile, HEAD_DIM), F32),
            pltpu.VMEM((N_HEADS, tile, HEAD_DIM), F32),
            pltpu.VMEM((slots, tile, GATE_LANES), F32),
            pltpu.VMEM((slots, tile, GATE_LANES), F32),
            pltpu.VMEM((slots, n_ch, HEAD_DIM), F32),
            pltpu.VMEM((slots, tile, GROUP_W), BF16),
            pltpu.VMEM((slots, tile, GROUP_W), BF16),
            pltpu.VMEM((slots, tile, GROUP_W), BF16),
            pltpu.VMEM((slots * n_ch, HEAD_DIM, HEAD_DIM), F32),
            pltpu.VMEM((slots * n_ch, CHUNK, HEAD_DIM), F32),
            pltpu.VMEM((slots * n_ch, HEAD_DIM, HEAD_DIM), F32),
            pltpu.VMEM((slots * n_ch, HEAD_DIM + CHUNK, HEAD_DIM), BF16),
            pltpu.VMEM((slots * n_ch, CHUNK, HEAD_DIM), F32),
        ],
        compiler_params=_params(("arbitrary", "arbitrary")),
        name="mixers",
    )(proj, gates, c0, n0, m0, s0, conv0, gate_par, conv_w, a_norm_w, b_norm_w)


def _mem_kv_body(mem_ref, nw_ref, wkv_ref, k_ref, v_ref, kb_ref, vb_ref):
    mn = _rms(mem_ref[...], nw_ref[...]).astype(BF16)
    d = k_ref.shape[1]
    k = _dot(mn, wkv_ref[:, 0:d])
    v = _dot(mn, wkv_ref[:, d:2 * d])
    k_ref[...] = k
    v_ref[...] = v
    kb_ref[...] = k.astype(BF16)
    vb_ref[...] = v.astype(BF16)


def _mem_kv(mem2d, norm_w, wkv):
    rows, d = mem2d.shape
    tm = min(ROW_TILE, rows)
    assert rows % tm == 0
    row = lambda i: (i, 0)
    cst = lambda i: (0, 0)
    return pl.pallas_call(
        _mem_kv_body,
        grid=(rows // tm,),
        in_specs=[pl.BlockSpec((tm, d), row), pl.BlockSpec((1, d), cst), pl.BlockSpec((d, 2 * d), cst)],
        out_specs=[pl.BlockSpec((tm, d), row)] * 4,
        out_shape=[jax.ShapeDtypeStruct((rows, d), F32)] * 2 + [jax.ShapeDtypeStruct((rows, d), BF16)] * 2,
        compiler_params=_params(("arbitrary",)),
        name="mem_kv",
    )(mem2d, norm_w, wkv)


def _attn_body(h_ref, x_ref, wout_ref, nx_ref, wq_ref, mk_ref, mv_ref, wo_ref, o_ref):
    d = x_ref.shape[2]
    dh = d // N_XHEADS
    x1 = x_ref[0] + _dot(h_ref[0], wout_ref[...])
    xn = _rms(x1, nx_ref[...]).astype(BF16)
    q = _dot(xn, wq_ref[...]).astype(BF16)
    acc = x1
    for h in range(N_XHEADS):
        cols = slice(h * dh, (h + 1) * dh)
        s = lax.dot_general(q[:, cols], mk_ref[0, :, cols], _NT, preferred_element_type=F32) * (dh ** -0.5)
        e = jnp.exp(s - jnp.max(s, axis=-1, keepdims=True))
        p = (e / jnp.sum(e, axis=-1, keepdims=True)).astype(BF16)
        oh = _dot(p, mv_ref[0, :, cols]).astype(BF16)
        acc = acc + _dot(oh, wo_ref[cols, :])
    o_ref[0] = acc


def _out_proj_attn(hcat, x, w_out, norm_w, wq, mk, mv, wo):
    bsz, t, d = x.shape
    n_mem = mk.shape[1]
    tm = min(ROW_TILE, t)
    assert t % tm == 0
    bj = lambda b, j: (b, j, 0)
    cst = lambda b, j: (0, 0)
    mem = lambda b, j: (b, 0, 0)
    return pl.pallas_call(
        _attn_body,
        grid=(bsz, t // tm),
        in_specs=[
            pl.BlockSpec((1, tm, d), bj),
            pl.BlockSpec((1, tm, d), bj),
            pl.BlockSpec((d, d), cst),
            pl.BlockSpec((1, d), cst),
            pl.BlockSpec((d, d), cst),
            pl.BlockSpec((1, n_mem, d), mem),
            pl.BlockSpec((1, n_mem, d), mem),
            pl.BlockSpec((d, d), cst),
        ],
        out_specs=pl.BlockSpec((1, tm, d), bj),
        out_shape=jax.ShapeDtypeStruct((bsz, t, d), F32),
        compiler_params=_params(("arbitrary", "arbitrary")),
        name="out_proj_attn",
    )(hcat, x, w_out, norm_w, wq, mk, mv, wo)


def _ffn_body(x_ref, nf_ref, w1_ref, w2_ref, nfin_ref, o_ref, *, hid_tile, final_norm):
    x = x_ref[...]
    xn = _rms(x, nf_ref[...]).astype(BF16)
    acc = x
    for c0 in range(0, w1_ref.shape[1], hid_tile):
        hdn = jnp.maximum(_dot(xn, w1_ref[:, c0:c0 + hid_tile]), 0.0)
        acc = acc + _dot((hdn * hdn).astype(BF16), w2_ref[c0:c0 + hid_tile, :])
    o_ref[...] = _rms(acc, nfin_ref[...]) if final_norm else acc


def _ffn(x2d, norm_w, w1, w2, final_w, *, final_norm):
    rows, d = x2d.shape
    dff = w1.shape[1]
    tm = min(ROW_TILE, rows)
    assert rows % tm == 0
    row = lambda i: (i, 0)
    cst = lambda i: (0, 0)
    return pl.pallas_call(
        functools.partial(_ffn_body, hid_tile=1024, final_norm=final_norm),
        grid=(rows // tm,),
        in_specs=[
            pl.BlockSpec((tm, d), row),
            pl.BlockSpec((1, d), cst),
            pl.BlockSpec((d, dff), cst, pipeline_mode=pl.Buffered(1)),
            pl.BlockSpec((dff, d), cst, pipeline_mode=pl.Buffered(1)),
            pl.BlockSpec((1, d), cst),
        ],
        out_specs=pl.BlockSpec((tm, d), row),
        out_shape=jax.ShapeDtypeStruct((rows, d), F32),
        compiler_params=_params(("arbitrary",)),
        name="ffn",
    )(x2d, norm_w, w1, w2, final_w)


def _split_w_in(w_in):
    w = GROUP_W
    g0 = 4 * w
    b0 = g0 + 2 * N_HEADS
    g1 = b0 + 4 * w
    main = jnp.concatenate([w_in[:, :g0], w_in[:, b0:g1]], axis=1).astype(BF16)
    gate = jnp.concatenate([w_in[:, g0:b0], w_in[:, g1:g1 + 2 * N_HEADS]], axis=1)
    return main, gate.T.astype(BF16)


def _gate_params(igate_b, fgate_b, a_log, dt_bias, width):
    zeros = jnp.zeros((N_HEADS,), F32)
    bias = jnp.concatenate([igate_b, fgate_b, dt_bias, zeros]).astype(F32)
    alog = jnp.concatenate([zeros, zeros, a_log.astype(F32), zeros])
    return jnp.broadcast_to(jnp.stack([bias, alog])[:, :, None], (2, 4 * N_HEADS, width))


def _trunk_layer(x, mem_k, mem_v, c0, n0, m0, s0, conv0, *, valid, w_main, w_gate, gate_vecs, lw, final_w,
                 final_norm):
    bsz, t, d = x.shape
    tp = CHUNK if valid < CHUNK else t
    xp = x if tp == t else jnp.pad(x, ((0, 0), (0, tp - t), (0, 0)))
    proj, gates = _in_proj(xp.reshape(bsz * tp, d), lw["norm_mix_w"], w_main, w_gate)
    hcat, c1, n1, m1, s1, conv1 = _mixers(
        proj.reshape(bsz, tp, -1), gates.reshape(-1, bsz, tp).transpose(1, 0, 2),
        c0, n0, m0.reshape(bsz, 1, N_HEADS), s0, conv0,
        _gate_params(*gate_vecs, width=min(MIXER_TILE, tp)), lw["gdn_conv_w"], lw["mlstm_norm_w"],
        lw["gdn_norm_w"], valid=valid, tile=MIXER_TILE)
    x2 = _out_proj_attn(hcat, x, lw["w_out"], lw["norm_x_w"], lw["wq_x"], mem_k, mem_v, lw["wo_x"])
    y = _ffn(x2.reshape(bsz * t, d), lw["norm_ffn_w"], lw["w_ff1"], lw["w_ff2"], final_w,
             final_norm=final_norm)
    return y.reshape(bsz, t, d), c1, n1, m1.reshape(bsz, N_HEADS), s1, conv1


def kernel(x_prompt, x_sample, state_mlstm_C, state_mlstm_n, state_mlstm_m, state_gdn_S, state_gdn_conv, cache_mem_k, cache_mem_v, mem_prompt, norm_mix_w, w_in, mlstm_igate_b, mlstm_fgate_b, mlstm_norm_w, gdn_conv_w, gdn_A_log, gdn_dt_bias, gdn_norm_w, w_out, norm_x_w, norm_mem_w, wq_x, wk_x, wv_x, wo_x, norm_ffn_w, w_ff1, w_ff2, norm_final_w):
    depth = w_in.shape[0]
    bp, _, d = x_prompt.shape
    bs, ts, _ = x_sample.shape
    n_mem = mem_prompt.shape[1]
    row = lambda a: a.reshape(1, -1).astype(F32)
    hp, hs = x_prompt, x_sample
    outs_p = [[] for _ in range(7)]
    outs_s = [[] for _ in range(5)]
    for l in range(depth):
        w_main, w_gate = _split_w_in(w_in[l])
        gate_vecs = (mlstm_igate_b[l], mlstm_fgate_b[l], gdn_A_log[l], gdn_dt_bias[l])
        lw = dict(
            norm_mix_w=row(norm_mix_w[l]), gdn_conv_w=gdn_conv_w[l].astype(F32),
            mlstm_norm_w=row(mlstm_norm_w[l]), gdn_norm_w=row(gdn_norm_w[l]),
            w_out=w_out[l].astype(BF16), norm_x_w=row(norm_x_w[l]), wq_x=wq_x[l].astype(BF16),
            wo_x=wo_x[l].astype(BF16), norm_ffn_w=row(norm_ffn_w[l]),
            w_ff1=w_ff1[l].astype(BF16), w_ff2=w_ff2[l].astype(BF16))
        common = dict(w_main=w_main, w_gate=w_gate, gate_vecs=gate_vecs, lw=lw, final_w=row(norm_final_w),
                      final_norm=(l == depth - 1))
        wkv = jnp.concatenate([wk_x[l], wv_x[l]], axis=1).astype(BF16)
        mk, mv, mk_b, mv_b = _mem_kv(mem_prompt.reshape(bp * n_mem, d), row(norm_mem_w[l]), wkv)
        hp, c1, n1, m1, s1, cv1 = _trunk_layer(
            hp, mk_b.reshape(bp, n_mem, d), mv_b.reshape(bp, n_mem, d),
            jnp.zeros((bp, N_HEADS, HEAD_DIM, HEAD_DIM), F32), jnp.zeros((bp, N_HEADS, HEAD_DIM), F32),
            jnp.zeros((bp, N_HEADS), F32), jnp.zeros((bp, N_HEADS, HEAD_DIM, HEAD_DIM), F32),
            jnp.zeros((bp, CONV_TAPS - 1, 3 * GROUP_W), F32), valid=CHUNK, **common)
        for acc, val in zip(outs_p, (c1, n1, m1, s1, cv1,
                                     mk.reshape(bp, n_mem, N_XHEADS, d // N_XHEADS),
                                     mv.reshape(bp, n_mem, N_XHEADS, d // N_XHEADS))):
            acc.append(val)
        hs, c2, n2, m2, s2, cv2 = _trunk_layer(
            hs, cache_mem_k[l].reshape(bs, n_mem, d).astype(BF16), cache_mem_v[l].reshape(bs, n_mem, d).astype(BF16),
            state_mlstm_C[l], state_mlstm_n[l], state_mlstm_m[l], state_gdn_S[l], state_gdn_conv[l],
            valid=ts, **common)
        for acc, val in zip(outs_s, (c2, n2, m2, s2, cv2)):
            acc.append(val)
    return (hp, hs, *[jnp.stack(a) for a in outs_p], *[jnp.stack(a) for a in outs_s])
```

```python
import functools
import math

import jax
import jax.numpy as jnp
from jax import lax
from jax.experimental import pallas as pl
from jax.experimental.pallas import tpu as pltpu

F32 = jnp.float32
BF16 = jnp.bfloat16
EPS = 1e-6
N_HEADS = 4
HEAD_DIM = 128
GROUP_W = N_HEADS * HEAD_DIM
CONV_TAPS = 4
N_XHEADS = 4
CHUNK = 64
GATE_LANES = 128
NEG_BIG = -1e30
HIST_ROW = 8

VMEM_LIMIT_BYTES = 56 * 1024 * 1024
ROW_TILE = 512
MIXER_TILE = 256

_NT = (((1,), (1,)), ((), ()))
_TN = (((0,), (0,)), ((), ()))


def _rms(x, w):
    return x * lax.rsqrt(jnp.mean(x * x, axis=-1, keepdims=True) + EPS) * w


def _dot(a, b):
    return jnp.dot(a, b, preferred_element_type=F32)


def _sigmoid(x):
    return 1.0 / (1.0 + jnp.exp(-x))


def _params(sem):
    return pltpu.CompilerParams(dimension_semantics=sem, vmem_limit_bytes=VMEM_LIMIT_BYTES)


def _in_proj_body(x_ref, nw_ref, w_ref, wg_ref, proj_ref, gate_ref, *, col_tile):
    xn = _rms(x_ref[...], nw_ref[...]).astype(BF16)
    for c0 in range(0, w_ref.shape[1], col_tile):
        proj_ref[:, c0:c0 + col_tile] = _dot(xn, w_ref[:, c0:c0 + col_tile]).astype(BF16)
    gate_ref[...] = lax.dot_general(wg_ref[...], xn, _NT, preferred_element_type=F32).reshape(gate_ref.shape)


def _in_proj(x2d, norm_w, w_main, w_gate, seq_len):
    rows, d = x2d.shape
    n = w_main.shape[1]
    n_g = w_gate.shape[0]
    tm = min(ROW_TILE, rows)
    assert rows % tm == 0 and rows % seq_len == 0
    bsz = rows // seq_len
    direct = seq_len % tm == 0
    if direct:
        gate_spec = pl.BlockSpec((1, n_g, tm), lambda i: (i // (seq_len // tm), 0, i % (seq_len // tm)))
        gate_shape = jax.ShapeDtypeStruct((bsz, n_g, seq_len), F32)
    else:
        gate_spec = pl.BlockSpec((n_g, tm), lambda i: (0, i))
        gate_shape = jax.ShapeDtypeStruct((n_g, rows), F32)
    proj, gates = pl.pallas_call(
        functools.partial(_in_proj_body, col_tile=512),
        grid=(rows // tm,),
        in_specs=[
            pl.BlockSpec((tm, d), lambda i: (i, 0)),
            pl.BlockSpec((1, d), lambda i: (0, 0)),
            pl.BlockSpec((d, n), lambda i: (0, 0)),
            pl.BlockSpec((n_g, d), lambda i: (0, 0)),
        ],
        out_specs=[pl.BlockSpec((tm, n), lambda i: (i, 0)), gate_spec],
        out_shape=[jax.ShapeDtypeStruct((rows, n), BF16), gate_shape],
        compiler_params=_params(("arbitrary",)),
        name="in_proj",
    )(x2d, norm_w, w_main, w_gate)
    if not direct:
        gates = gates.reshape(n_g, bsz, seq_len).transpose(1, 0, 2)
    return proj, gates


def _mixer_body(proj_ref, gate_ref, c0_ref, n0_ref, m0_ref, s0_ref, conv0_ref,
                gpar_ref, convw_ref, anw_ref, bnw_ref,
                h_ref, c1_ref, n1_ref, m1_ref, s1_ref, conv1_ref,
                c_sc, n_sc, m_sc, s_sc, u_sc, qn_sc, kf_sc, v_sc,
                z_sc, st_sc, dn_sc, q_sc, og_sc, zg_sc, dc_sc, sv_sc, nc_sc, lb_sc, qu_sc,
                *, n_chunks, valid, pipelined):
    L = CHUNK
    T = n_chunks * L
    n_ch = n_chunks * N_HEADS
    j = pl.program_id(1)
    last = pl.num_programs(1) - 1
    scale = HEAD_DIM ** -0.5
    heads = range(N_HEADS)
    chunks = range(n_chunks)
    slot_a = (j & 1) if pipelined else 0
    slot_b = (1 - slot_a) if pipelined else 0
    handoff = (z_sc, st_sc, dn_sc, q_sc, og_sc, zg_sc, dc_sc, sv_sc, nc_sc, lb_sc, qu_sc)

    def load_state():
        c_sc[...] = c0_ref[0]
        for h in heads:
            n_sc[h, 0:1, :] = n0_ref[0, h:h + 1, :]
        m_sc[...] = jnp.zeros(m_sc.shape, F32)
        m_sc[0:1, N_HEADS:2 * N_HEADS] = m0_ref[0]
        s_sc[...] = s0_ref[0]

    if pipelined:
        @pl.when((pl.program_id(0) == 0) & (j == 0))
        def _zero_handoff():
            for ref in handoff:
                ref[...] = jnp.zeros(ref.shape, ref.dtype)

    @pl.when(j == 0)
    def _first_tile():
        load_state()
        u_sc[HIST_ROW - (CONV_TAPS - 1):HIST_ROW, :] = conv0_ref[0]

    lane = lax.broadcasted_iota(jnp.int32, (L, 2 * L), 1)
    tok = lax.broadcasted_iota(jnp.int32, (L, 2 * L), 0)
    src = lane & (L - 1)
    left = lane < L
    tri_incl = tok >= src
    tri_strict = tok > src
    eye_left = jnp.where(left & (tok == src), 1.0, 0.0)
    lane_row = lax.broadcasted_iota(jnp.int32, (1, GATE_LANES), 1)
    head_lanes = (lane_row >= N_HEADS) & (lane_row < 2 * N_HEADS)
    base = 4 * GROUP_W
    ch = [(c, h) for c in chunks for h in heads]
    idx = {p: i for i, p in enumerate(ch)}

    def two(x):
        return jnp.concatenate([x, x], axis=0)

    def rows(c):
        return slice(c * L, (c + 1) * L)

    def hcols(h, group=0):
        return slice(group * GROUP_W + h * HEAD_DIM, group * GROUP_W + (h + 1) * HEAD_DIM)

    def highest(a, b, dims=None):
        if dims is None:
            return jnp.dot(a, b, precision=lax.Precision.HIGHEST, preferred_element_type=F32)
        return lax.dot_general(a, b, dims, precision=lax.Precision.HIGHEST, preferred_element_type=F32)

    def phase_a():
        n_g = 4 * N_HEADS
        row_g = lax.broadcasted_iota(jnp.int32, (n_g, T), 0)
        pre = gate_ref[0] + gpar_ref[0]
        e = jnp.exp(-jnp.abs(pre))
        t = jnp.log1p(e)
        act = jnp.where(row_g < 4, pre,
                        jnp.where(row_g < 8, -(jnp.maximum(-pre, 0.0) + t),
                                  jnp.where(row_g < 12, -jnp.exp(gpar_ref[1]) * (jnp.maximum(pre, 0.0) + t),
                                            jnp.where(pre >= 0.0, 1.0, e) / (1.0 + e))))
        if valid < L:
            tok_g = lax.broadcasted_iota(jnp.int32, (n_g, T), 1)
            act = jnp.where(tok_g >= valid, jnp.where(row_g < 4, NEG_BIG, 0.0), act)
        r_t = lax.broadcasted_iota(jnp.int32, (T, T), 0)
        c_t = lax.broadcasted_iota(jnp.int32, (T, T), 1)
        shift = int(math.log2(L))
        within = jnp.where((r_t <= c_t) & ((r_t >> shift) == (c_t >> shift)), 1.0, 0.0)
        z_g = jnp.where((row_g >= 4) & (row_g < 12), highest(act, within), act)
        pick = jnp.where(lax.broadcasted_iota(jnp.int32, (n_g, GATE_LANES), 0)
                         == lax.broadcasted_iota(jnp.int32, (n_g, GATE_LANES), 1), 1.0, 0.0)
        z_t = highest(z_g, pick, _TN)
        z_sc[slot_a] = z_t
        z = [z_t[rows(c), :] for c in chunks]
        r_d = lax.broadcasted_iota(jnp.int32, (T, 2 * L), 0)
        c_d = lax.broadcasted_iota(jnp.int32, (T, 2 * L), 1)
        zt = [highest(z_g, jnp.where(r_d == c * L + (c_d & (L - 1)), 1.0, 0.0)) for c in chunks]
        q_sc[slot_a] = proj_ref[0, :, 0:GROUP_W]
        og_sc[slot_a] = proj_ref[0, :, 3 * GROUP_W:4 * GROUP_W]
        zg_sc[slot_a] = proj_ref[0, :, base + 3 * GROUP_W:base + 4 * GROUP_W]
        yield

        u_sc[HIST_ROW:HIST_ROW + T, :] = proj_ref[0, :, base:base + 3 * GROUP_W].astype(F32)
        for blk in range(3 * N_HEADS):
            cols = slice(blk * HEAD_DIM, (blk + 1) * HEAD_DIM)
            y = convw_ref[0:1, cols] * u_sc[HIST_ROW - 3:HIST_ROW - 3 + T, cols]
            for tap in range(1, CONV_TAPS):
                y = y + convw_ref[tap:tap + 1, cols] * u_sc[HIST_ROW - 3 + tap:HIST_ROW - 3 + tap + T, cols]
            y = y * _sigmoid(y)
            kind, h = divmod(blk, N_HEADS)
            if kind == 0:
                qn_sc[h] = (y * lax.rsqrt(jnp.sum(y * y, axis=-1, keepdims=True) + EPS) * scale).astype(BF16)
            elif kind == 1:
                kf_sc[h] = y * lax.rsqrt(jnp.sum(y * y, axis=-1, keepdims=True) + EPS)
            else:
                v_sc[h] = y
        t_valid = T if valid == L else valid
        u_sc[HIST_ROW - 3:HIST_ROW, :] = u_sc[HIST_ROW + t_valid - 3:HIST_ROW + t_valid, :]
        yield

        def a_k(c, h):
            return proj_ref[0, rows(c), hcols(h, 1)]

        qk_a = {(c, h): lax.dot_general(proj_ref[0, rows(c), hcols(h)], two(a_k(c, h)), _NT,
                                        preferred_element_type=F32) for c, h in ch}
        qk_b = {}
        for c, h in ch:
            kn = kf_sc[h, rows(c), :].astype(BF16)
            qk_b[c, h] = lax.dot_general(jnp.concatenate([qn_sc[h, rows(c), :], kn], axis=0), two(kn), _NT,
                                         preferred_element_type=F32)
        yield

        d_a, m_a, s_a, kw_a = {}, {}, {}, {}
        for c, h in ch:
            b_col = z[c][:, 4 + h:5 + h]
            ig_row, b_row = zt[c][h:h + 1, :], zt[c][4 + h:5 + h, :]
            d_a[c, h] = jnp.where(tri_incl, b_col - b_row + ig_row, NEG_BIG)
        for c, h in ch:
            m_a[c, h] = jnp.max(d_a[c, h], axis=1, keepdims=True)
            st_sc[slot_a, rows(c), 4 + h:5 + h] = m_a[c, h]
        yield
        for c, h in ch:
            s_a[c, h] = jnp.where(left, qk_a[c, h] * scale * jnp.exp(d_a[c, h] - m_a[c, h]), 0.0)
        for c, h in ch:
            st_sc[slot_a, rows(c), h:h + 1] = jnp.sum(s_a[c, h], axis=1, keepdims=True)
        for c, h in ch:
            ig_col, b_col = z[c][:, h:h + 1], z[c][:, 4 + h:5 + h]
            w_s = jnp.exp(b_col[L - 1:L, :] - b_col + ig_col - m_a[c, h][L - 1:L, :])
            kw = a_k(c, h).astype(F32) * scale * w_s
            dn_sc[slot_a, idx[c, h]:idx[c, h] + 1, :] = jnp.sum(kw, axis=0, keepdims=True)
            kw_a[c, h] = kw.astype(BF16)
        yield
        for c, h in ch:
            v = proj_ref[0, rows(c), hcols(h, 2)]
            sv_sc[slot_a * n_ch + idx[c, h]] = _dot(s_a[c, h].astype(BF16), two(v))
            dc_sc[slot_a * n_ch + idx[c, h]] = lax.dot_general(kw_a[c, h], v, _TN, preferred_element_type=F32)
        yield

        decay, r_b = {}, {}
        for c, h in ch:
            g_col, g_row = z[c][:, 8 + h:9 + h], zt[c][8 + h:9 + h, :]
            decay[c, h] = jnp.exp(jnp.where(tri_incl, g_col - g_row, NEG_BIG))
            a2 = jnp.where(tri_strict, z[c][:, 12 + h:13 + h] * decay[c, h] * qk_b[c, h][L:, :], 0.0)
            r_b[c, h] = jnp.where(left, eye_left, -a2)
        yield
        for _ in range(int(math.log2(L))):
            for p in ch:
                r = r_b[p]
                out = _dot(jnp.where(left, 0.0, r).astype(BF16), two(r.astype(BF16)))
                r_b[p] = jnp.where(left, r + out, out)
            yield
        uk = {}
        for c, h in ch:
            g_col, beta = z[c][:, 8 + h:9 + h], z[c][:, 12 + h:13 + h]
            rhs = jnp.concatenate([beta * v_sc[h, rows(c), :],
                                   beta * jnp.exp(g_col) * kf_sc[h, rows(c), :]], axis=1).astype(BF16)
            uk[c, h] = _dot(jnp.where(left, r_b[c, h], 0.0).astype(BF16), two(rhs)).astype(BF16)
        yield
        for c, h in ch:
            g_col = z[c][:, 8 + h:9 + h]
            g_last = g_col[L - 1:L, :]
            qkd = jnp.where(left, qk_b[c, h][:L, :] * decay[c, h], 0.0).astype(BF16)
            qu = _dot(qkd, two(uk[c, h]))
            kd = (kf_sc[h, rows(c), :] * jnp.exp(g_last - g_col)).astype(BF16)
            ku = lax.dot_general(kd, uk[c, h], _TN, preferred_element_type=F32)
            q_eff = jnp.exp(g_col) * qn_sc[h, rows(c), :].astype(F32) - qu[:, HEAD_DIM:]
            i = slot_a * n_ch + idx[c, h]
            qu_sc[i] = qu[:, :HEAD_DIM]
            nc_sc[i] = ku[:, :HEAD_DIM]
            lb_sc[i] = jnp.concatenate([ku[:, HEAD_DIM:], q_eff], axis=0).astype(BF16)
        yield

    def phase_b():
        for c in chunks:
            zc = z_sc[slot_b, rows(c), :]
            stc = st_sc[slot_b, rows(c), :]
            base_i = slot_b * n_ch + c * N_HEADS
            c_prev = [c_sc[h] for h in heads]
            s_prev = [s_sc[h] for h in heads]
            n_prev = [n_sc[h, 0:1, :] for h in heads]
            qc = [_dot(q_sc[slot_b, rows(c), hcols(h)], c_prev[h].astype(BF16)) for h in heads]
            rm = [_dot(lb_sc[base_i + h], s_prev[h].astype(BF16)) for h in heads]
            m_prev = m_sc[0:1, :]
            inter = zc + m_prev
            m = jnp.maximum(inter, stc)
            m_last = jnp.where(head_lanes, m[L - 1:L, :], 0.0)
            decay0_t = jnp.exp(zc[L - 1:L, :] + m_prev - m_last)
            f_new_t = jnp.exp(stc[L - 1:L, :] - m_last)
            m_sc[0:1, :] = m_last
            for h in heads:
                hl = slice(N_HEADS + h, N_HEADS + h + 1)
                c_sc[h] = decay0_t[:, hl] * c_prev[h] + f_new_t[:, hl] * dc_sc[base_i + h]
                n_sc[h, 0:1, :] = (decay0_t[:, hl] * n_prev[h]
                                   + f_new_t[:, hl] * dn_sc[slot_b, c * N_HEADS + h:c * N_HEADS + h + 1, :])
            for h in heads:
                s_sc[h] = jnp.exp(zc[L - 1:L, 8 + h:9 + h]) * s_prev[h] + nc_sc[base_i + h] - rm[h][:HEAD_DIM, :]
            yield
            w_inter_t = jnp.exp(inter - m)
            w_intra_t = jnp.exp(stc - m)
            inv_cap_t = jnp.exp(-m)
            qn = [jnp.sum(q_sc[slot_b, rows(c), hcols(h)].astype(F32) * n_prev[h], axis=-1, keepdims=True)
                  for h in heads]
            hh, o_b = [], []
            for h in heads:
                hl = slice(N_HEADS + h, N_HEADS + h + 1)
                w_inter, w_intra = w_inter_t[:, hl], w_intra_t[:, hl]
                num = w_inter * qc[h] + w_intra * sv_sc[base_i + h]
                den = w_inter * qn[h] + w_intra * stc[:, h:h + 1]
                hh.append(num / jnp.maximum(jnp.abs(den), inv_cap_t[:, hl]))
                o_b.append(rm[h][HEAD_DIM:, :] + qu_sc[base_i + h])
            ms_a = [jnp.mean(x * x, axis=-1, keepdims=True) for x in hh]
            ms_b = [jnp.mean(x * x, axis=-1, keepdims=True) for x in o_b]
            for h in heads:
                og = og_sc[slot_b, rows(c), hcols(h)].astype(F32)
                hn = hh[h] * lax.rsqrt(ms_a[h] + EPS) * anw_ref[...]
                h_ref[0, rows(c), hcols(h)] = (_sigmoid(og) * hn).astype(BF16)
            for h in heads:
                zg = zg_sc[slot_b, rows(c), hcols(h)].astype(F32)
                on = o_b[h] * lax.rsqrt(ms_b[h] + EPS) * bnw_ref[...]
                h_ref[0, rows(c), hcols(h, 1)] = (on * (zg * _sigmoid(zg))).astype(BF16)
            yield

    steps_a, steps_b = 9 + int(math.log2(L)), 2 * n_chunks
    gen_a, gen_b = phase_a(), phase_b()
    if pipelined:
        done_b = 0
        for k in range(steps_a):
            while done_b < steps_b and done_b * steps_a <= k * steps_b:
                next(gen_b)
                done_b += 1
            next(gen_a)
        for _ in range(steps_b - done_b):
            next(gen_b)
        assert next(gen_a, None) is None and next(gen_b, None) is None
    else:
        for _ in gen_a:
            pass
        for _ in gen_b:
            pass

    if pipelined:
        @pl.when(j == 0)
        def _discard_phase_b():
            load_state()

    @pl.when(j == last)
    def _store_state():
        c1_ref[0] = c_sc[...]
        for h in heads:
            n1_ref[0, h:h + 1, :] = n_sc[h, 0:1, :]
        m1_ref[0] = m_sc[0:1, N_HEADS:2 * N_HEADS]
        s1_ref[0] = s_sc[...]
        conv1_ref[0] = u_sc[HIST_ROW - 3:HIST_ROW, :]


def _mixers(proj, gates, c0, n0, m0, s0, conv0, gate_par, conv_w, a_norm_w, b_norm_w, *, valid, tile):
    bsz, t, n = proj.shape
    tile = min(tile, t)
    assert t % tile == 0 and tile % CHUNK == 0
    assert valid == CHUNK or (t == CHUNK and CONV_TAPS - 1 <= valid < CHUNK)
    n_chunks = tile // CHUNK
    n_tiles = t // tile
    pipelined = n_tiles > 1
    slots = 2 if pipelined else 1
    n_ch = n_chunks * N_HEADS
    cw = 3 * GROUP_W
    if pipelined:
        a_tile = lambda j: jnp.minimum(j, n_tiles - 1)
        b_tile = lambda j: jnp.maximum(j - 1, 0)
    else:
        a_tile = b_tile = lambda j: j
    st4 = lambda b, j: (b, 0, 0, 0)
    st3 = lambda b, j: (b, 0, 0)
    cst2 = lambda b, j: (0, 0)
    cst3 = lambda b, j: (0, 0, 0)
    state_specs = [
        pl.BlockSpec((1, N_HEADS, HEAD_DIM, HEAD_DIM), st4),
        pl.BlockSpec((1, N_HEADS, HEAD_DIM), st3),
        pl.BlockSpec((1, 1, N_HEADS), st3),
        pl.BlockSpec((1, N_HEADS, HEAD_DIM, HEAD_DIM), st4),
        pl.BlockSpec((1, CONV_TAPS - 1, cw), st3),
    ]
    state_shapes = [
        jax.ShapeDtypeStruct((bsz, N_HEADS, HEAD_DIM, HEAD_DIM), F32),
        jax.ShapeDtypeStruct((bsz, N_HEADS, HEAD_DIM), F32),
        jax.ShapeDtypeStruct((bsz, 1, N_HEADS), F32),
        jax.ShapeDtypeStruct((bsz, N_HEADS, HEAD_DIM, HEAD_DIM), F32),
        jax.ShapeDtypeStruct((bsz, CONV_TAPS - 1, cw), F32),
    ]
    return pl.pallas_call(
        functools.partial(_mixer_body, n_chunks=n_chunks, valid=valid, pipelined=pipelined),
        grid=(bsz, n_tiles + (1 if pipelined else 0)),
        in_specs=[
            pl.BlockSpec((1, tile, n), lambda b, j: (b, a_tile(j), 0)),
            pl.BlockSpec((1, 4 * N_HEADS, tile), lambda b, j: (b, 0, a_tile(j))),
            *state_specs,
            pl.BlockSpec((2, 4 * N_HEADS, tile), cst3),
            pl.BlockSpec((CONV_TAPS, cw), cst2),
            pl.BlockSpec((1, HEAD_DIM), cst2),
            pl.BlockSpec((1, HEAD_DIM), cst2),
        ],
        out_specs=[pl.BlockSpec((1, tile, 2 * GROUP_W), lambda b, j: (b, b_tile(j), 0)), *state_specs],
        out_shape=[jax.ShapeDtypeStruct((bsz, t, 2 * GROUP_W), BF16), *state_shapes],
        scratch_shapes=[
            pltpu.VMEM((N_HEADS, HEAD_DIM, HEAD_DIM), F32),
            pltpu.VMEM((N_HEADS, 8, HEAD_DIM), F32),
            pltpu.VMEM((8, GATE_LANES), F32),
            pltpu.VMEM((N_HEADS, HEAD_DIM, HEAD_DIM), F32),
            pltpu.VMEM((HIST_ROW + tile, cw), F32),
            pltpu.VMEM((N_HEADS, tile, HEAD_DIM), BF16),
            pltpu.VMEM((N_HEADS, tile, HEAD_DIM), F32),
            pltpu.VMEM((N_HEADS, tile, HEAD_DIM), F32),
            pltpu.VMEM((slots, tile, GATE_LANES), F32),
            pltpu.VMEM((slots, tile, GATE_LANES), F32),
            pltpu.VMEM((slots, n_ch, HEAD_DIM), F32),
            pltpu.VMEM((slots, tile, GROUP_W), BF16),
            pltpu.VMEM((slots, tile, GROUP_W), BF16),
            pltpu.VMEM((slots, tile, GROUP_W), BF16),
            pltpu.VMEM((slots * n_ch, HEAD_DIM, HEAD_DIM), F32),
            pltpu.VMEM((slots * n_ch, CHUNK, HEAD_DIM), F32),
            pltpu.VMEM((slots * n_ch, HEAD_DIM, HEAD_DIM), F32),
            pltpu.VMEM((slots * n_ch, HEAD_DIM + CHUNK, HEAD_DIM), BF16),
            pltpu.VMEM((slots * n_ch, CHUNK, HEAD_DIM), F32),
        ],
        compiler_params=_params(("arbitrary", "arbitrary")),
        name="mixers",
    )(proj, gates, c0, n0, m0, s0, conv0, gate_par, conv_w, a_norm_w, b_norm_w)


def _mem_kv_body(mem_ref, nw_ref, wkv_ref, k_ref, v_ref, kb_ref, vb_ref):
    nb, n_mem, d = mem_ref.shape
    dh = d // N_XHEADS
    mn = _rms(mem_ref[...].reshape(nb * n_mem, d), nw_ref[...]).astype(BF16)
    for w0, f_ref, b_ref in ((0, k_ref, kb_ref), (d, v_ref, vb_ref)):
        y = _dot(mn, wkv_ref[:, w0:w0 + d])
        b_ref[...] = y.astype(BF16).reshape(nb, n_mem, d)
        for h in range(N_XHEADS):
            f_ref[:, :, h, :] = y[:, h * dh:(h + 1) * dh].reshape(nb, n_mem, dh)


def _mem_kv(mem, norm_w, wkv):
    bsz, n_mem, d = mem.shape
    nb = max(1, min(bsz, ROW_TILE // n_mem))
    assert bsz % nb == 0
    b3 = lambda i: (i, 0, 0)
    b4 = lambda i: (i, 0, 0, 0)
    cst = lambda i: (0, 0)
    dh = d // N_XHEADS
    return pl.pallas_call(
        _mem_kv_body,
        grid=(bsz // nb,),
        in_specs=[pl.BlockSpec((nb, n_mem, d), b3), pl.BlockSpec((1, d), cst), pl.BlockSpec((d, 2 * d), cst)],
        out_specs=[pl.BlockSpec((nb, n_mem, N_XHEADS, dh), b4)] * 2 + [pl.BlockSpec((nb, n_mem, d), b3)] * 2,
        out_shape=([jax.ShapeDtypeStruct((bsz, n_mem, N_XHEADS, dh), F32)] * 2
                   + [jax.ShapeDtypeStruct((bsz, n_mem, d), BF16)] * 2),
        compiler_params=_params(("arbitrary",)),
        name="mem_kv",
    )(mem, norm_w, wkv)


def _attn_body(h_ref, x_ref, wout_ref, nx_ref, wq_ref, mk_ref, mv_ref, wo_ref, o_ref, *, n_sub):
    tm, d = x_ref.shape[1], x_ref.shape[2]
    dh = d // N_XHEADS
    subs = [slice(i * (tm // n_sub), (i + 1) * (tm // n_sub)) for i in range(n_sub)]
    hcols = [slice(h * dh, (h + 1) * dh) for h in range(N_XHEADS)]

    def heads_of(ref):
        if len(ref.shape) == 4:
            return [ref[0, :, h, :].astype(BF16) for h in range(N_XHEADS)]
        return [ref[0, :, c] for c in hcols]

    mk, mv = heads_of(mk_ref), heads_of(mv_ref)
    x1 = [x_ref[0, r, :] + _dot(h_ref[0, r, :], wout_ref[...]) for r in subs]
    xn = [_rms(x, nx_ref[...]).astype(BF16) for x in x1]
    q = [_dot(x, wq_ref[...]).astype(BF16) for x in xn]
    s = [[lax.dot_general(qi[:, c], kh, _NT, preferred_element_type=F32) * (dh ** -0.5)
          for c, kh in zip(hcols, mk)] for qi in q]
    mx = [[jnp.max(sh, axis=-1, keepdims=True) for sh in si] for si in s]
    e = [[jnp.exp(sh - mh) for sh, mh in zip(si, mi)] for si, mi in zip(s, mx)]
    den = [[jnp.sum(eh, axis=-1, keepdims=True) for eh in ei] for ei in e]
    p = [[(eh / dh_).astype(BF16) for eh, dh_ in zip(ei, di)] for ei, di in zip(e, den)]
    o = [[_dot(ph, vh).astype(BF16) for ph, vh in zip(pi, mv)] for pi in p]
    for r, x, oi in zip(subs, x1, o):
        acc = x
        for oh, c in zip(oi, hcols):
            acc = acc + _dot(oh, wo_ref[c, :])
        o_ref[0, r, :] = acc


def _out_proj_attn(hcat, x, w_out, norm_w, wq, mk, mv, wo):
    bsz, t, d = x.shape
    tm = min(ROW_TILE, t)
    assert t % tm == 0
    bj = lambda b, j: (b, j, 0)
    cst = lambda b, j: (0, 0)
    mem_spec = pl.BlockSpec((1,) + mk.shape[1:], lambda b, j: (b,) + (0,) * (mk.ndim - 1))
    return pl.pallas_call(
        functools.partial(_attn_body, n_sub=2 if tm >= 256 else 1),
        grid=(bsz, t // tm),
        in_specs=[
            pl.BlockSpec((1, tm, d), bj),
            pl.BlockSpec((1, tm, d), bj),
            pl.BlockSpec((d, d), cst),
            pl.BlockSpec((1, d), cst),
            pl.BlockSpec((d, d), cst),
            mem_spec,
            mem_spec,
            pl.BlockSpec((d, d), cst),
        ],
        out_specs=pl.BlockSpec((1, tm, d), bj),
        out_shape=jax.ShapeDtypeStruct((bsz, t, d), F32),
        compiler_params=_params(("arbitrary", "arbitrary")),
        name="out_proj_attn",
    )(hcat, x, w_out, norm_w, wq, mk, mv, wo)


def _ffn_body(x_ref, nf_ref, w1_ref, w2_ref, nfin_ref, o_ref, *, hid_tile, final_norm):
    x = x_ref[...]
    xn = _rms(x, nf_ref[...]).astype(BF16)
    acc = x
    for c0 in range(0, w1_ref.shape[1], hid_tile):
        hdn = jnp.maximum(_dot(xn, w1_ref[:, c0:c0 + hid_tile]), 0.0)
        acc = acc + _dot((hdn * hdn).astype(BF16), w2_ref[c0:c0 + hid_tile, :])
    o_ref[...] = _rms(acc, nfin_ref[...]) if final_norm else acc


def _ffn(x2d, norm_w, w1, w2, final_w, *, final_norm):
    rows, d = x2d.shape
    dff = w1.shape[1]
    tm = min(ROW_TILE, rows)
    assert rows % tm == 0
    row = lambda i: (i, 0)
    cst = lambda i: (0, 0)
    return pl.pallas_call(
        functools.partial(_ffn_body, hid_tile=1024, final_norm=final_norm),
        grid=(rows // tm,),
        in_specs=[
            pl.BlockSpec((tm, d), row),
            pl.BlockSpec((1, d), cst),
            pl.BlockSpec((d, dff), cst, pipeline_mode=pl.Buffered(1)),
            pl.BlockSpec((dff, d), cst, pipeline_mode=pl.Buffered(1)),
            pl.BlockSpec((1, d), cst),
        ],
        out_specs=pl.BlockSpec((tm, d), row),
        out_shape=jax.ShapeDtypeStruct((rows, d), F32),
        compiler_params=_params(("arbitrary",)),
        name="ffn",
    )(x2d, norm_w, w1, w2, final_w)


def _split_w_in(w_in):
    w = GROUP_W
    g0 = 4 * w
    b0 = g0 + 2 * N_HEADS
    g1 = b0 + 4 * w
    main = jnp.concatenate([w_in[:, :g0], w_in[:, b0:g1]], axis=1).astype(BF16)
    gate = jnp.concatenate([w_in[:, g0:b0], w_in[:, g1:g1 + 2 * N_HEADS]], axis=1)
    return main, gate.T.astype(BF16)


def _gate_params(igate_b, fgate_b, a_log, dt_bias, width):
    zeros = jnp.zeros((N_HEADS,), F32)
    bias = jnp.concatenate([igate_b, fgate_b, dt_bias, zeros]).astype(F32)
    alog = jnp.concatenate([zeros, zeros, a_log.astype(F32), zeros])
    return jnp.broadcast_to(jnp.stack([bias, alog])[:, :, None], (2, 4 * N_HEADS, width))


def _trunk_layer(x, mem_k, mem_v, c0, n0, m0, s0, conv0, *, valid, w_main, w_gate, gate_vecs, lw, final_w,
                 final_norm):
    bsz, t, d = x.shape
    tp = CHUNK if valid < CHUNK else t
    xp = x if tp == t else jnp.pad(x, ((0, 0), (0, tp - t), (0, 0)))
    proj, gates = _in_proj(xp.reshape(bsz * tp, d), lw["norm_mix_w"], w_main, w_gate, tp)
    hcat, c1, n1, m1, s1, conv1 = _mixers(
        proj.reshape(bsz, tp, -1), gates, c0, n0, m0.reshape(bsz, 1, N_HEADS), s0, conv0,
        _gate_params(*gate_vecs, width=min(MIXER_TILE, tp)), lw["gdn_conv_w"], lw["mlstm_norm_w"],
        lw["gdn_norm_w"], valid=valid, tile=MIXER_TILE)
    x2 = _out_proj_attn(hcat, x, lw["w_out"], lw["norm_x_w"], lw["wq_x"], mem_k, mem_v, lw["wo_x"])
    y = _ffn(x2.reshape(bsz * t, d), lw["norm_ffn_w"], lw["w_ff1"], lw["w_ff2"], final_w,
             final_norm=final_norm)
    return y.reshape(bsz, t, d), c1, n1, m1.reshape(bsz, N_HEADS), s1, conv1


def kernel(x_prompt, x_sample, state_mlstm_C, state_mlstm_n, state_mlstm_m, state_gdn_S, state_gdn_conv, cache_mem_k, cache_mem_v, mem_prompt, norm_mix_w, w_in, mlstm_igate_b, mlstm_fgate_b, mlstm_norm_w, gdn_conv_w, gdn_A_log, gdn_dt_bias, gdn_norm_w, w_out, norm_x_w, norm_mem_w, wq_x, wk_x, wv_x, wo_x, norm_ffn_w, w_ff1, w_ff2, norm_final_w):
    depth = w_in.shape[0]
    bp = x_prompt.shape[0]
    ts = x_sample.shape[1]
    row = lambda a: a.reshape(1, -1).astype(F32)
    hp, hs = x_prompt, x_sample
    outs_p = [[] for _ in range(7)]
    outs_s = [[] for _ in range(5)]
    for l in range(depth):
        w_main, w_gate = _split_w_in(w_in[l])
        gate_vecs = (mlstm_igate_b[l], mlstm_fgate_b[l], gdn_A_log[l], gdn_dt_bias[l])
        lw = dict(
            norm_mix_w=row(norm_mix_w[l]), gdn_conv_w=gdn_conv_w[l].astype(F32),
            mlstm_norm_w=row(mlstm_norm_w[l]), gdn_norm_w=row(gdn_norm_w[l]),
            w_out=w_out[l].astype(BF16), norm_x_w=row(norm_x_w[l]), wq_x=wq_x[l].astype(BF16),
            wo_x=wo_x[l].astype(BF16), norm_ffn_w=row(norm_ffn_w[l]),
            w_ff1=w_ff1[l].astype(BF16), w_ff2=w_ff2[l].astype(BF16))
        common = dict(w_main=w_main, w_gate=w_gate, gate_vecs=gate_vecs, lw=lw, final_w=row(norm_final_w),
                      final_norm=(l == depth - 1))
        wkv = jnp.concatenate([wk_x[l], wv_x[l]], axis=1).astype(BF16)
        mk, mv, mk_b, mv_b = _mem_kv(mem_prompt, row(norm_mem_w[l]), wkv)
        hp, c1, n1, m1, s1, cv1 = _trunk_layer(
            hp, mk_b, mv_b,
            jnp.zeros((bp, N_HEADS, HEAD_DIM, HEAD_DIM), F32), jnp.zeros((bp, N_HEADS, HEAD_DIM), F32),
            jnp.zeros((bp, N_HEADS), F32), jnp.zeros((bp, N_HEADS, HEAD_DIM, HEAD_DIM), F32),
            jnp.zeros((bp, CONV_TAPS - 1, 3 * GROUP_W), F32), valid=CHUNK, **common)
        for acc, val in zip(outs_p, (c1, n1, m1, s1, cv1, mk, mv)):
            acc.append(val)
        hs, c2, n2, m2, s2, cv2 = _trunk_layer(
            hs, cache_mem_k[l], cache_mem_v[l],
            state_mlstm_C[l], state_mlstm_n[l], state_mlstm_m[l], state_gdn_S[l], state_gdn_conv[l],
            valid=ts, **common)
        for acc, val in zip(outs_s, (c2, n2, m2, s2, cv2)):
            acc.append(val)
    return (hp, hs, *[jnp.stack(a) for a in outs_p], *[jnp.stack(a) for a in outs_s])
```

```python
import functools
import math

import jax
import jax.numpy as jnp
from jax import lax
from jax.experimental import pallas as pl
from jax.experimental.pallas import tpu as pltpu

F32 = jnp.float32
BF16 = jnp.bfloat16
EPS = 1e-6
N_HEADS = 4
HEAD_DIM = 128
GROUP_W = N_HEADS * HEAD_DIM
CONV_TAPS = 4
N_XHEADS = 4
CHUNK = 64
GATE_LANES = 128
NEG_BIG = -1e30
HIST_ROW = 8

VMEM_LIMIT_BYTES = 56 * 1024 * 1024
ROW_TILE = 512
MIXER_TILE = 256
MIXER_SEQS = 2
ATTN_MIN_ROWS = 64

_NT = (((1,), (1,)), ((), ()))
_TN = (((0,), (0,)), ((), ()))


def _rms(x, w):
    return x * lax.rsqrt(jnp.mean(x * x, axis=-1, keepdims=True) + EPS) * w


def _dot(a, b):
    return jnp.dot(a, b, preferred_element_type=F32)


def _sigmoid(x):
    return 1.0 / (1.0 + jnp.exp(-x))


def _params(sem):
    return pltpu.CompilerParams(dimension_semantics=sem, vmem_limit_bytes=VMEM_LIMIT_BYTES)


def _in_proj_body(x_ref, nw_ref, w_ref, wg_ref, proj_ref, gate_ref, *, col_tile):
    xn = _rms(x_ref[...], nw_ref[...]).astype(BF16)
    for c0 in range(0, w_ref.shape[1], col_tile):
        proj_ref[:, c0:c0 + col_tile] = _dot(xn, w_ref[:, c0:c0 + col_tile]).astype(BF16)
    gate_ref[...] = lax.dot_general(wg_ref[...], xn, _NT, preferred_element_type=F32).reshape(gate_ref.shape)


def _in_proj(x2d, norm_w, w_main, w_gate, seq_len):
    rows, d = x2d.shape
    n = w_main.shape[1]
    n_g = w_gate.shape[0]
    tm = min(ROW_TILE, rows)
    assert rows % tm == 0 and rows % seq_len == 0
    bsz = rows // seq_len
    direct = seq_len % tm == 0
    if direct:
        gate_spec = pl.BlockSpec((1, n_g, tm), lambda i: (i // (seq_len // tm), 0, i % (seq_len // tm)))
        gate_shape = jax.ShapeDtypeStruct((bsz, n_g, seq_len), F32)
    else:
        gate_spec = pl.BlockSpec((n_g, tm), lambda i: (0, i))
        gate_shape = jax.ShapeDtypeStruct((n_g, rows), F32)
    proj, gates = pl.pallas_call(
        functools.partial(_in_proj_body, col_tile=512),
        grid=(rows // tm,),
        in_specs=[
            pl.BlockSpec((tm, d), lambda i: (i, 0)),
            pl.BlockSpec((1, d), lambda i: (0, 0)),
            pl.BlockSpec((d, n), lambda i: (0, 0)),
            pl.BlockSpec((n_g, d), lambda i: (0, 0)),
        ],
        out_specs=[pl.BlockSpec((tm, n), lambda i: (i, 0)), gate_spec],
        out_shape=[jax.ShapeDtypeStruct((rows, n), BF16), gate_shape],
        compiler_params=_params(("arbitrary",)),
        name="in_proj",
    )(x2d, norm_w, w_main, w_gate)
    if not direct:
        gates = gates.reshape(n_g, bsz, seq_len).transpose(1, 0, 2)
    return proj, gates


def _mixer_body(proj_ref, gate_ref, c0_ref, n0_ref, m0_ref, s0_ref, conv0_ref,
                gpar_ref, convw_ref, anw_ref, bnw_ref,
                h_ref, c1_ref, n1_ref, m1_ref, s1_ref, conv1_ref,
                c_sc, n_sc, m_sc, s_sc, u_sc, qn_sc, kf_sc, v_sc,
                z_sc, st_sc, dn_sc, q_sc, og_sc, zg_sc, dc_sc, sv_sc, nc_sc, lb_sc, qu_sc,
                *, n_seq, n_chunks, n_tiles, valid, pipelined):
    L = CHUNK
    T = n_chunks * L
    n_ch = n_chunks * N_HEADS
    slots = 2 if pipelined else 1
    step = pl.program_id(0)
    scale = HEAD_DIM ** -0.5
    heads = range(N_HEADS)
    chunks = range(n_chunks)
    seqs = range(n_seq)
    if pipelined:
        j_a = lax.rem(jnp.minimum(step, pl.num_programs(0) - 2), n_tiles)
        j_b = lax.rem(step + (n_tiles - 1), n_tiles)
        has_b = step >= 1
        slot_a = step & 1
        slot_b = 1 - slot_a
    else:
        j_a = j_b = slot_a = slot_b = 0
        has_b = True
    handoff = (z_sc, st_sc, dn_sc, q_sc, og_sc, zg_sc, dc_sc, sv_sc, nc_sc, lb_sc, qu_sc)

    def load_state():
        c_sc[...] = c0_ref[...].reshape(c_sc.shape)
        s_sc[...] = s0_ref[...].reshape(s_sc.shape)
        m_sc[...] = jnp.zeros(m_sc.shape, F32)
        for s in seqs:
            for h in heads:
                n_sc[s * N_HEADS + h, 0:1, :] = n0_ref[s, h:h + 1, :]
            m_sc[s, 0:1, N_HEADS:2 * N_HEADS] = m0_ref[s]

    def load_conv_history():
        u_sc[:, HIST_ROW - (CONV_TAPS - 1):HIST_ROW, :] = conv0_ref[...]

    if pipelined:
        @pl.when(step == 0)
        def _zero_scratch():
            for ref in handoff + (c_sc, n_sc, m_sc, s_sc):
                ref[...] = jnp.zeros(ref.shape, ref.dtype)

        pl.when(j_a == 0)(load_conv_history)
        pl.when(has_b & (j_b == 0))(load_state)
    else:
        load_conv_history()
        load_state()

    lane = lax.broadcasted_iota(jnp.int32, (L, 2 * L), 1)
    tok = lax.broadcasted_iota(jnp.int32, (L, 2 * L), 0)
    src = lane & (L - 1)
    left = lane < L
    tri_incl = tok >= src
    tri_strict = tok > src
    eye_left = jnp.where(left & (tok == src), 1.0, 0.0)
    lane_row = lax.broadcasted_iota(jnp.int32, (1, GATE_LANES), 1)
    head_lanes = (lane_row >= N_HEADS) & (lane_row < 2 * N_HEADS)
    base = 4 * GROUP_W
    ch = [(c, h) for c in chunks for h in heads]
    idx = {p: i for i, p in enumerate(ch)}

    def two(x):
        return jnp.concatenate([x, x], axis=0)

    def rows(c):
        return slice(c * L, (c + 1) * L)

    def hcols(h, group=0):
        return slice(group * GROUP_W + h * HEAD_DIM, group * GROUP_W + (h + 1) * HEAD_DIM)

    def highest(a, b, dims=None):
        if dims is None:
            return jnp.dot(a, b, precision=lax.Precision.HIGHEST, preferred_element_type=F32)
        return lax.dot_general(a, b, dims, precision=lax.Precision.HIGHEST, preferred_element_type=F32)

    def phase_a(s):
        sa = s * slots + slot_a
        hd = s * N_HEADS
        n_g = 4 * N_HEADS
        row_g = lax.broadcasted_iota(jnp.int32, (n_g, T), 0)
        pre = gate_ref[s] + gpar_ref[0]
        e = jnp.exp(-jnp.abs(pre))
        t = jnp.log1p(e)
        act = jnp.where(row_g < 4, pre,
                        jnp.where(row_g < 8, -(jnp.maximum(-pre, 0.0) + t),
                                  jnp.where(row_g < 12, -jnp.exp(gpar_ref[1]) * (jnp.maximum(pre, 0.0) + t),
                                            jnp.where(pre >= 0.0, 1.0, e) / (1.0 + e))))
        if valid < L:
            tok_g = lax.broadcasted_iota(jnp.int32, (n_g, T), 1)
            act = jnp.where(tok_g >= valid, jnp.where(row_g < 4, NEG_BIG, 0.0), act)
        r_t = lax.broadcasted_iota(jnp.int32, (T, T), 0)
        c_t = lax.broadcasted_iota(jnp.int32, (T, T), 1)
        shift = int(math.log2(L))
        within = jnp.where((r_t <= c_t) & ((r_t >> shift) == (c_t >> shift)), 1.0, 0.0)
        z_g = jnp.where((row_g >= 4) & (row_g < 12), highest(act, within), act)
        pick = jnp.where(lax.broadcasted_iota(jnp.int32, (n_g, GATE_LANES), 0)
                         == lax.broadcasted_iota(jnp.int32, (n_g, GATE_LANES), 1), 1.0, 0.0)
        z_t = highest(z_g, pick, _TN)
        z_sc[sa] = z_t
        z = [z_t[rows(c), :] for c in chunks]
        r_d = lax.broadcasted_iota(jnp.int32, (T, 2 * L), 0)
        c_d = lax.broadcasted_iota(jnp.int32, (T, 2 * L), 1)
        zt = [highest(z_g, jnp.where(r_d == c * L + (c_d & (L - 1)), 1.0, 0.0)) for c in chunks]
        q_sc[sa] = proj_ref[s, :, 0:GROUP_W]
        og_sc[sa] = proj_ref[s, :, 3 * GROUP_W:4 * GROUP_W]
        zg_sc[sa] = proj_ref[s, :, base + 3 * GROUP_W:base + 4 * GROUP_W]
        yield

        u_sc[s, HIST_ROW:HIST_ROW + T, :] = proj_ref[s, :, base:base + 3 * GROUP_W].astype(F32)
        for blk in range(3 * N_HEADS):
            cols = slice(blk * HEAD_DIM, (blk + 1) * HEAD_DIM)
            y = convw_ref[0:1, cols] * u_sc[s, HIST_ROW - 3:HIST_ROW - 3 + T, cols]
            for tap in range(1, CONV_TAPS):
                y = y + convw_ref[tap:tap + 1, cols] * u_sc[s, HIST_ROW - 3 + tap:HIST_ROW - 3 + tap + T, cols]
            y = y * _sigmoid(y)
            kind, h = divmod(blk, N_HEADS)
            if kind == 0:
                qn_sc[hd + h] = (y * lax.rsqrt(jnp.sum(y * y, axis=-1, keepdims=True) + EPS) * scale).astype(BF16)
            elif kind == 1:
                kf_sc[hd + h] = y * lax.rsqrt(jnp.sum(y * y, axis=-1, keepdims=True) + EPS)
            else:
                v_sc[hd + h] = y
        t_valid = T if valid == L else valid
        u_sc[s, HIST_ROW - 3:HIST_ROW, :] = u_sc[s, HIST_ROW + t_valid - 3:HIST_ROW + t_valid, :]
        yield

        def a_k(c, h):
            return proj_ref[s, rows(c), hcols(h, 1)]

        qk_a = {(c, h): lax.dot_general(proj_ref[s, rows(c), hcols(h)], two(a_k(c, h)), _NT,
                                        preferred_element_type=F32) for c, h in ch}
        qk_b = {}
        for c, h in ch:
            kn = kf_sc[hd + h, rows(c), :].astype(BF16)
            qk_b[c, h] = lax.dot_general(jnp.concatenate([qn_sc[hd + h, rows(c), :], kn], axis=0), two(kn), _NT,
                                         preferred_element_type=F32)
        yield

        d_a, m_a, s_a, kw_a = {}, {}, {}, {}
        for c, h in ch:
            b_col = z[c][:, 4 + h:5 + h]
            ig_row, b_row = zt[c][h:h + 1, :], zt[c][4 + h:5 + h, :]
            d_a[c, h] = jnp.where(tri_incl, b_col - b_row + ig_row, NEG_BIG)
        for c, h in ch:
            m_a[c, h] = jnp.max(d_a[c, h], axis=1, keepdims=True)
            st_sc[sa, rows(c), 4 + h:5 + h] = m_a[c, h]
        yield
        for c, h in ch:
            s_a[c, h] = jnp.where(left, qk_a[c, h] * scale * jnp.exp(d_a[c, h] - m_a[c, h]), 0.0)
        for c, h in ch:
            st_sc[sa, rows(c), h:h + 1] = jnp.sum(s_a[c, h], axis=1, keepdims=True)
        for c, h in ch:
            ig_col, b_col = z[c][:, h:h + 1], z[c][:, 4 + h:5 + h]
            w_s = jnp.exp(b_col[L - 1:L, :] - b_col + ig_col - m_a[c, h][L - 1:L, :])
            kw = a_k(c, h).astype(F32) * scale * w_s
            dn_sc[sa, idx[c, h]:idx[c, h] + 1, :] = jnp.sum(kw, axis=0, keepdims=True)
            kw_a[c, h] = kw.astype(BF16)
        yield
        for c, h in ch:
            v = proj_ref[s, rows(c), hcols(h, 2)]
            sv_sc[sa * n_ch + idx[c, h]] = _dot(s_a[c, h].astype(BF16), two(v))
            dc_sc[sa * n_ch + idx[c, h]] = lax.dot_general(kw_a[c, h], v, _TN, preferred_element_type=F32)
        yield

        decay, r_b = {}, {}
        for c, h in ch:
            g_col, g_row = z[c][:, 8 + h:9 + h], zt[c][8 + h:9 + h, :]
            decay[c, h] = jnp.exp(jnp.where(tri_incl, g_col - g_row, NEG_BIG))
            a2 = jnp.where(tri_strict, z[c][:, 12 + h:13 + h] * decay[c, h] * qk_b[c, h][L:, :], 0.0)
            r_b[c, h] = jnp.where(left, eye_left, -a2)
        yield
        for _ in range(int(math.log2(L))):
            for p in ch:
                r = r_b[p]
                out = _dot(jnp.where(left, 0.0, r).astype(BF16), two(r.astype(BF16)))
                r_b[p] = jnp.where(left, r + out, out)
            yield
        uk = {}
        for c, h in ch:
            g_col, beta = z[c][:, 8 + h:9 + h], z[c][:, 12 + h:13 + h]
            rhs = jnp.concatenate([beta * v_sc[hd + h, rows(c), :],
                                   beta * jnp.exp(g_col) * kf_sc[hd + h, rows(c), :]], axis=1).astype(BF16)
            uk[c, h] = _dot(jnp.where(left, r_b[c, h], 0.0).astype(BF16), two(rhs)).astype(BF16)
        yield
        for c, h in ch:
            g_col = z[c][:, 8 + h:9 + h]
            g_last = g_col[L - 1:L, :]
            qkd = jnp.where(left, qk_b[c, h][:L, :] * decay[c, h], 0.0).astype(BF16)
            qu = _dot(qkd, two(uk[c, h]))
            kd = (kf_sc[hd + h, rows(c), :] * jnp.exp(g_last - g_col)).astype(BF16)
            ku = lax.dot_general(kd, uk[c, h], _TN, preferred_element_type=F32)
            q_eff = jnp.exp(g_col) * qn_sc[hd + h, rows(c), :].astype(F32) - qu[:, HEAD_DIM:]
            i = sa * n_ch + idx[c, h]
            qu_sc[i] = qu[:, :HEAD_DIM]
            nc_sc[i] = ku[:, :HEAD_DIM]
            lb_sc[i] = jnp.concatenate([ku[:, HEAD_DIM:], q_eff], axis=0).astype(BF16)
        yield

    def phase_b(s):
        sb = s * slots + slot_b
        hd = s * N_HEADS
        for c in chunks:
            zc = z_sc[sb, rows(c), :]
            stc = st_sc[sb, rows(c), :]
            base_i = sb * n_ch + c * N_HEADS
            c_prev = [c_sc[hd + h] for h in heads]
            s_prev = [s_sc[hd + h] for h in heads]
            n_prev = [n_sc[hd + h, 0:1, :] for h in heads]
            qc = [_dot(q_sc[sb, rows(c), hcols(h)], c_prev[h].astype(BF16)) for h in heads]
            rm = [_dot(lb_sc[base_i + h], s_prev[h].astype(BF16)) for h in heads]
            m_prev = m_sc[s, 0:1, :]
            inter = zc + m_prev
            m = jnp.maximum(inter, stc)
            m_last = jnp.where(head_lanes, m[L - 1:L, :], 0.0)
            decay0_t = jnp.exp(zc[L - 1:L, :] + m_prev - m_last)
            f_new_t = jnp.exp(stc[L - 1:L, :] - m_last)
            m_sc[s, 0:1, :] = m_last
            for h in heads:
                hl = slice(N_HEADS + h, N_HEADS + h + 1)
                c_sc[hd + h] = decay0_t[:, hl] * c_prev[h] + f_new_t[:, hl] * dc_sc[base_i + h]
                n_sc[hd + h, 0:1, :] = (decay0_t[:, hl] * n_prev[h]
                                        + f_new_t[:, hl] * dn_sc[sb, c * N_HEADS + h:c * N_HEADS + h + 1, :])
            for h in heads:
                s_sc[hd + h] = (jnp.exp(zc[L - 1:L, 8 + h:9 + h]) * s_prev[h] + nc_sc[base_i + h]
                                - rm[h][:HEAD_DIM, :])
            yield
            w_inter_t = jnp.exp(inter - m)
            w_intra_t = jnp.exp(stc - m)
            inv_cap_t = jnp.exp(-m)
            qn = [jnp.sum(q_sc[sb, rows(c), hcols(h)].astype(F32) * n_prev[h], axis=-1, keepdims=True)
                  for h in heads]
            hh, o_b = [], []
            for h in heads:
                hl = slice(N_HEADS + h, N_HEADS + h + 1)
                w_inter, w_intra = w_inter_t[:, hl], w_intra_t[:, hl]
                num = w_inter * qc[h] + w_intra * sv_sc[base_i + h]
                den = w_inter * qn[h] + w_intra * stc[:, h:h + 1]
                hh.append(num / jnp.maximum(jnp.abs(den), inv_cap_t[:, hl]))
                o_b.append(rm[h][HEAD_DIM:, :] + qu_sc[base_i + h])
            ms_a = [jnp.mean(x * x, axis=-1, keepdims=True) for x in hh]
            ms_b = [jnp.mean(x * x, axis=-1, keepdims=True) for x in o_b]
            for h in heads:
                og = og_sc[sb, rows(c), hcols(h)].astype(F32)
                hn = hh[h] * lax.rsqrt(ms_a[h] + EPS) * anw_ref[...]
                h_ref[s, rows(c), hcols(h)] = (_sigmoid(og) * hn).astype(BF16)
            for h in heads:
                zg = zg_sc[sb, rows(c), hcols(h)].astype(F32)
                on = o_b[h] * lax.rsqrt(ms_b[h] + EPS) * bnw_ref[...]
                h_ref[s, rows(c), hcols(h, 1)] = (on * (zg * _sigmoid(zg))).astype(BF16)
            yield

    steps_a, steps_b = 9 + int(math.log2(L)), 2 * n_chunks
    gens_a = [phase_a(s) for s in seqs]
    gens_b = [phase_b(s) for s in seqs]

    def advance(gens):
        for g in gens:
            next(g)

    if pipelined:
        done_b = 0
        for k in range(steps_a):
            while done_b < steps_b and done_b * steps_a <= k * steps_b:
                advance(gens_b)
                done_b += 1
            advance(gens_a)
        for _ in range(steps_b - done_b):
            advance(gens_b)
    else:
        for _ in range(steps_a):
            advance(gens_a)
        for _ in range(steps_b):
            advance(gens_b)
    assert all(next(g, None) is None for g in gens_a + gens_b)

    def store_state():
        c1_ref[...] = c_sc[...].reshape(c1_ref.shape)
        s1_ref[...] = s_sc[...].reshape(s1_ref.shape)
        for s in seqs:
            for h in heads:
                n1_ref[s, h:h + 1, :] = n_sc[s * N_HEADS + h, 0:1, :]
            m1_ref[s] = m_sc[s, 0:1, N_HEADS:2 * N_HEADS]

    def store_conv_history():
        conv1_ref[...] = u_sc[:, HIST_ROW - 3:HIST_ROW, :]

    if pipelined:
        pl.when(has_b & (j_b == n_tiles - 1))(store_state)
        pl.when(j_a == n_tiles - 1)(store_conv_history)
    else:
        store_state()
        store_conv_history()


def _mixers(proj, gates, c0, n0, m0, s0, conv0, gate_par, conv_w, a_norm_w, b_norm_w, *, valid, tile, n_seq):
    bsz, t, n = proj.shape
    tile = min(tile, t)
    assert t % tile == 0 and tile % CHUNK == 0 and bsz % n_seq == 0
    assert valid == CHUNK or (t == CHUNK and CONV_TAPS - 1 <= valid < CHUNK)
    n_chunks = tile // CHUNK
    n_tiles = t // tile
    pipelined = n_tiles > 1
    slots = 2 if pipelined else 1
    n_ch = n_chunks * N_HEADS
    cw = 3 * GROUP_W
    n_groups = bsz // n_seq
    total = n_groups * n_tiles
    if pipelined:
        pair_a = lambda s: jnp.minimum(s, total - 1)
        pair_b = lambda s: jnp.maximum(s - 1, 0)
    else:
        pair_a = pair_b = lambda s: s
    tile_a = lambda s: (pair_a(s) // n_tiles, pair_a(s) % n_tiles)
    tile_b = lambda s: (pair_b(s) // n_tiles, pair_b(s) % n_tiles)
    st4 = lambda s: (tile_b(s)[0], 0, 0, 0)
    st3 = lambda s: (tile_b(s)[0], 0, 0)
    cst2 = lambda s: (0, 0)
    cst3 = lambda s: (0, 0, 0)
    conv_spec = pl.BlockSpec((n_seq, CONV_TAPS - 1, cw), lambda s: (tile_a(s)[0], 0, 0))
    state_specs = [
        pl.BlockSpec((n_seq, N_HEADS, HEAD_DIM, HEAD_DIM), st4),
        pl.BlockSpec((n_seq, N_HEADS, HEAD_DIM), st3),
        pl.BlockSpec((n_seq, 1, N_HEADS), st3),
        pl.BlockSpec((n_seq, N_HEADS, HEAD_DIM, HEAD_DIM), st4),
        conv_spec,
    ]
    state_shapes = [
        jax.ShapeDtypeStruct((bsz, N_HEADS, HEAD_DIM, HEAD_DIM), F32),
        jax.ShapeDtypeStruct((bsz, N_HEADS, HEAD_DIM), F32),
        jax.ShapeDtypeStruct((bsz, 1, N_HEADS), F32),
        jax.ShapeDtypeStruct((bsz, N_HEADS, HEAD_DIM, HEAD_DIM), F32),
        jax.ShapeDtypeStruct((bsz, CONV_TAPS - 1, cw), F32),
    ]
    ns, nh = n_seq * slots, n_seq * N_HEADS
    return pl.pallas_call(
        functools.partial(_mixer_body, n_seq=n_seq, n_chunks=n_chunks, n_tiles=n_tiles, valid=valid,
                          pipelined=pipelined),
        grid=(total + (1 if pipelined else 0),),
        in_specs=[
            pl.BlockSpec((n_seq, tile, n), lambda s: (*tile_a(s), 0)),
            pl.BlockSpec((n_seq, 4 * N_HEADS, tile), lambda s: (tile_a(s)[0], 0, tile_a(s)[1])),
            *state_specs,
            pl.BlockSpec((2, 4 * N_HEADS, tile), cst3),
            pl.BlockSpec((CONV_TAPS, cw), cst2),
            pl.BlockSpec((1, HEAD_DIM), cst2),
            pl.BlockSpec((1, HEAD_DIM), cst2),
        ],
        out_specs=[pl.BlockSpec((n_seq, tile, 2 * GROUP_W), lambda s: (*tile_b(s), 0)), *state_specs],
        out_shape=[jax.ShapeDtypeStruct((bsz, t, 2 * GROUP_W), BF16), *state_shapes],
        scratch_shapes=[
            pltpu.VMEM((nh, HEAD_DIM, HEAD_DIM), F32),
            pltpu.VMEM((nh, 8, HEAD_DIM), F32),
            pltpu.VMEM((n_seq, 8, GATE_LANES), F32),
            pltpu.VMEM((nh, HEAD_DIM, HEAD_DIM), F32),
            pltpu.VMEM((n_seq, HIST_ROW + tile, cw), F32),
            pltpu.VMEM((nh, tile, HEAD_DIM), BF16),
            pltpu.VMEM((nh, tile, HEAD_DIM), F32),
            pltpu.VMEM((nh, tile, HEAD_DIM), F32),
            pltpu.VMEM((ns, tile, GATE_LANES), F32),
            pltpu.VMEM((ns, tile, GATE_LANES), F32),
            pltpu.VMEM((ns, n_ch, HEAD_DIM), F32),
            pltpu.VMEM((ns, tile, GROUP_W), BF16),
            pltpu.VMEM((ns, tile, GROUP_W), BF16),
            pltpu.VMEM((ns, tile, GROUP_W), BF16),
            pltpu.VMEM((ns * n_ch, HEAD_DIM, HEAD_DIM), F32),
            pltpu.VMEM((ns * n_ch, CHUNK, HEAD_DIM), F32),
            pltpu.VMEM((ns * n_ch, HEAD_DIM, HEAD_DIM), F32),
            pltpu.VMEM((ns * n_ch, HEAD_DIM + CHUNK, HEAD_DIM), BF16),
            pltpu.VMEM((ns * n_ch, CHUNK, HEAD_DIM), F32),
        ],
        compiler_params=_params(("arbitrary",)),
        name="mixers",
    )(proj, gates, c0, n0, m0, s0, conv0, gate_par, conv_w, a_norm_w, b_norm_w)


def _mem_kv_body(mem_ref, nw_ref, wkv_ref, k_ref, v_ref, kb_ref, vb_ref):
    nb, n_mem, d = mem_ref.shape
    dh = d // N_XHEADS
    mn = _rms(mem_ref[...].reshape(nb * n_mem, d), nw_ref[...]).astype(BF16)
    for w0, f_ref, b_ref in ((0, k_ref, kb_ref), (d, v_ref, vb_ref)):
        y = _dot(mn, wkv_ref[:, w0:w0 + d])
        b_ref[...] = y.astype(BF16).reshape(nb, n_mem, d)
        for h in range(N_XHEADS):
            f_ref[:, :, h, :] = y[:, h * dh:(h + 1) * dh].reshape(nb, n_mem, dh)


def _mem_kv(mem, norm_w, wkv):
    bsz, n_mem, d = mem.shape
    nb = max(1, min(bsz, ROW_TILE // n_mem))
    assert bsz % nb == 0
    b3 = lambda i: (i, 0, 0)
    b4 = lambda i: (i, 0, 0, 0)
    cst = lambda i: (0, 0)
    dh = d // N_XHEADS
    return pl.pallas_call(
        _mem_kv_body,
        grid=(bsz // nb,),
        in_specs=[pl.BlockSpec((nb, n_mem, d), b3), pl.BlockSpec((1, d), cst), pl.BlockSpec((d, 2 * d), cst)],
        out_specs=[pl.BlockSpec((nb, n_mem, N_XHEADS, dh), b4)] * 2 + [pl.BlockSpec((nb, n_mem, d), b3)] * 2,
        out_shape=([jax.ShapeDtypeStruct((bsz, n_mem, N_XHEADS, dh), F32)] * 2
                   + [jax.ShapeDtypeStruct((bsz, n_mem, d), BF16)] * 2),
        compiler_params=_params(("arbitrary",)),
        name="mem_kv",
    )(mem, norm_w, wkv)


def _attn_body(h_ref, x_ref, wout_ref, nx_ref, wq_ref, mk_ref, mv_ref, wo_ref, o_ref, *, n_sub):
    nb, tm, d = x_ref.shape
    dh = d // N_XHEADS
    rows = nb * tm
    gr = rows // n_sub
    assert gr % tm == 0 or tm % gr == 0
    per = max(1, gr // tm)
    hcols = [slice(h * dh, (h + 1) * dh) for h in range(N_XHEADS)]

    def heads_of(ref, b):
        if len(ref.shape) == 4:
            return [ref[b, :, h, :].astype(BF16) for h in range(N_XHEADS)]
        return [ref[b, :, c] for c in hcols]

    def group_rows(ref, g):
        if per > 1 or gr == tm:
            return ref[g * per:(g + 1) * per].reshape(gr, d)
        return ref[(g * gr) // tm, (g * gr) % tm:(g * gr) % tm + gr, :]

    groups = range(n_sub)
    x1 = [group_rows(x_ref, g) + _dot(group_rows(h_ref, g), wout_ref[...]) for g in groups]
    xn = [_rms(x, nx_ref[...]).astype(BF16) for x in x1]
    q = [_dot(x, wq_ref[...]).astype(BF16) for x in xn]
    unit_rows = min(gr, tm)
    units = [(g, slice(u * unit_rows, (u + 1) * unit_rows), (g * gr + u * unit_rows) // tm)
             for g in groups for u in range(gr // unit_rows)]
    kv = {b: (heads_of(mk_ref, b), heads_of(mv_ref, b)) for b in sorted({b for _, _, b in units})}
    s = [[lax.dot_general(q[g][r, c], kh, _NT, preferred_element_type=F32) * (dh ** -0.5)
          for c, kh in zip(hcols, kv[b][0])] for g, r, b in units]
    mx = [[jnp.max(sh, axis=-1, keepdims=True) for sh in si] for si in s]
    e = [[jnp.exp(sh - mh) for sh, mh in zip(si, mi)] for si, mi in zip(s, mx)]
    den = [[jnp.sum(eh, axis=-1, keepdims=True) for eh in ei] for ei in e]
    p = [[(eh / dh_).astype(BF16) for eh, dh_ in zip(ei, di)] for ei, di in zip(e, den)]
    o = [[_dot(ph, vh).astype(BF16) for ph, vh in zip(pi, kv[b][1])] for pi, (_, _, b) in zip(p, units)]
    for g in groups:
        mine = [oi for oi, (ug, _, _) in zip(o, units) if ug == g]
        acc = x1[g]
        for h, c in enumerate(hcols):
            oh = mine[0][h] if len(mine) == 1 else jnp.concatenate([oi[h] for oi in mine], axis=0)
            acc = acc + _dot(oh, wo_ref[c, :])
        if per > 1 or gr == tm:
            o_ref[g * per:(g + 1) * per] = acc.reshape(per, tm, d)
        else:
            o_ref[(g * gr) // tm, (g * gr) % tm:(g * gr) % tm + gr, :] = acc


def _out_proj_attn(hcat, x, w_out, norm_w, wq, mk, mv, wo):
    bsz, t, d = x.shape
    tm = min(ROW_TILE, t)
    assert t % tm == 0
    nb = max(1, min(bsz, ATTN_MIN_ROWS // tm))
    assert bsz % nb == 0
    bj = lambda b, j: (b, j, 0)
    cst = lambda b, j: (0, 0)
    mem_spec = pl.BlockSpec((nb,) + mk.shape[1:], lambda b, j: (b,) + (0,) * (mk.ndim - 1))
    return pl.pallas_call(
        functools.partial(_attn_body, n_sub=2 if tm >= 256 else 1),
        grid=(bsz // nb, t // tm),
        in_specs=[
            pl.BlockSpec((nb, tm, d), bj),
            pl.BlockSpec((nb, tm, d), bj),
            pl.BlockSpec((d, d), cst),
            pl.BlockSpec((1, d), cst),
            pl.BlockSpec((d, d), cst),
            mem_spec,
            mem_spec,
            pl.BlockSpec((d, d), cst),
        ],
        out_specs=pl.BlockSpec((nb, tm, d), bj),
        out_shape=jax.ShapeDtypeStruct((bsz, t, d), F32),
        compiler_params=_params(("arbitrary", "arbitrary")),
        name="out_proj_attn",
    )(hcat, x, w_out, norm_w, wq, mk, mv, wo)


def _ffn_body(x_ref, nf_ref, w1_ref, w2_ref, nfin_ref, o_ref, *, hid_tile, final_norm):
    x = x_ref[...]
    xn = _rms(x, nf_ref[...]).astype(BF16)
    acc = x
    for c0 in range(0, w1_ref.shape[1], hid_tile):
        hdn = jnp.maximum(_dot(xn, w1_ref[:, c0:c0 + hid_tile]), 0.0)
        acc = acc + _dot((hdn * hdn).astype(BF16), w2_ref[c0:c0 + hid_tile, :])
    o_ref[...] = _rms(acc, nfin_ref[...]) if final_norm else acc


def _ffn(x2d, norm_w, w1, w2, final_w, *, final_norm):
    rows, d = x2d.shape
    dff = w1.shape[1]
    tm = min(ROW_TILE, rows)
    assert rows % tm == 0
    row = lambda i: (i, 0)
    cst = lambda i: (0, 0)
    return pl.pallas_call(
        functools.partial(_ffn_body, hid_tile=1024, final_norm=final_norm),
        grid=(rows // tm,),
        in_specs=[
            pl.BlockSpec((tm, d), row),
            pl.BlockSpec((1, d), cst),
            pl.BlockSpec((d, dff), cst, pipeline_mode=pl.Buffered(1)),
            pl.BlockSpec((dff, d), cst, pipeline_mode=pl.Buffered(1)),
            pl.BlockSpec((1, d), cst),
        ],
        out_specs=pl.BlockSpec((tm, d), row),
        out_shape=jax.ShapeDtypeStruct((rows, d), F32),
        compiler_params=_params(("arbitrary",)),
        name="ffn",
    )(x2d, norm_w, w1, w2, final_w)


def _split_w_in(w_in):
    w = GROUP_W
    g0 = 4 * w
    b0 = g0 + 2 * N_HEADS
    g1 = b0 + 4 * w
    main = jnp.concatenate([w_in[:, :g0], w_in[:, b0:g1]], axis=1).astype(BF16)
    gate = jnp.concatenate([w_in[:, g0:b0], w_in[:, g1:g1 + 2 * N_HEADS]], axis=1)
    return main, gate.T.astype(BF16)


def _gate_params(igate_b, fgate_b, a_log, dt_bias, width):
    zeros = jnp.zeros((N_HEADS,), F32)
    bias = jnp.concatenate([igate_b, fgate_b, dt_bias, zeros]).astype(F32)
    alog = jnp.concatenate([zeros, zeros, a_log.astype(F32), zeros])
    return jnp.broadcast_to(jnp.stack([bias, alog])[:, :, None], (2, 4 * N_HEADS, width))


def _trunk_layer(x, mem_k, mem_v, c0, n0, m0, s0, conv0, *, valid, w_main, w_gate, gate_vecs, lw, final_w,
                 final_norm):
    bsz, t, d = x.shape
    tp = CHUNK if valid < CHUNK else t
    xp = x if tp == t else jnp.pad(x, ((0, 0), (0, tp - t), (0, 0)))
    proj, gates = _in_proj(xp.reshape(bsz * tp, d), lw["norm_mix_w"], w_main, w_gate, tp)
    hcat, c1, n1, m1, s1, conv1 = _mixers(
        proj.reshape(bsz, tp, -1), gates, c0, n0, m0.reshape(bsz, 1, N_HEADS), s0, conv0,
        _gate_params(*gate_vecs, width=min(MIXER_TILE, tp)), lw["gdn_conv_w"], lw["mlstm_norm_w"],
        lw["gdn_norm_w"], valid=valid, tile=MIXER_TILE, n_seq=MIXER_SEQS if tp <= MIXER_TILE else 1)
    x2 = _out_proj_attn(hcat, x, lw["w_out"], lw["norm_x_w"], lw["wq_x"], mem_k, mem_v, lw["wo_x"])
    y = _ffn(x2.reshape(bsz * t, d), lw["norm_ffn_w"], lw["w_ff1"], lw["w_ff2"], final_w,
             final_norm=final_norm)
    return y.reshape(bsz, t, d), c1, n1, m1.reshape(bsz, N_HEADS), s1, conv1


def kernel(x_prompt, x_sample, state_mlstm_C, state_mlstm_n, state_mlstm_m, state_gdn_S, state_gdn_conv, cache_mem_k, cache_mem_v, mem_prompt, norm_mix_w, w_in, mlstm_igate_b, mlstm_fgate_b, mlstm_norm_w, gdn_conv_w, gdn_A_log, gdn_dt_bias, gdn_norm_w, w_out, norm_x_w, norm_mem_w, wq_x, wk_x, wv_x, wo_x, norm_ffn_w, w_ff1, w_ff2, norm_final_w):
    depth = w_in.shape[0]
    bp = x_prompt.shape[0]
    ts = x_sample.shape[1]
    row = lambda a: a.reshape(1, -1).astype(F32)
    hp, hs = x_prompt, x_sample
    outs_p = [[] for _ in range(7)]
    outs_s = [[] for _ in range(5)]
    for l in range(depth):
        w_main, w_gate = _split_w_in(w_in[l])
        gate_vecs = (mlstm_igate_b[l], mlstm_fgate_b[l], gdn_A_log[l], gdn_dt_bias[l])
        lw = dict(
            norm_mix_w=row(norm_mix_w[l]), gdn_conv_w=gdn_conv_w[l].astype(F32),
            mlstm_norm_w=row(mlstm_norm_w[l]), gdn_norm_w=row(gdn_norm_w[l]),
            w_out=w_out[l].astype(BF16), norm_x_w=row(norm_x_w[l]), wq_x=wq_x[l].astype(BF16),
            wo_x=wo_x[l].astype(BF16), norm_ffn_w=row(norm_ffn_w[l]),
            w_ff1=w_ff1[l].astype(BF16), w_ff2=w_ff2[l].astype(BF16))
        common = dict(w_main=w_main, w_gate=w_gate, gate_vecs=gate_vecs, lw=lw, final_w=row(norm_final_w),
                      final_norm=(l == depth - 1))
        wkv = jnp.concatenate([wk_x[l], wv_x[l]], axis=1).astype(BF16)
        mk, mv, mk_b, mv_b = _mem_kv(mem_prompt, row(norm_mem_w[l]), wkv)
        hp, c1, n1, m1, s1, cv1 = _trunk_layer(
            hp, mk_b, mv_b,
            jnp.zeros((bp, N_HEADS, HEAD_DIM, HEAD_DIM), F32), jnp.zeros((bp, N_HEADS, HEAD_DIM), F32),
            jnp.zeros((bp, N_HEADS), F32), jnp.zeros((bp, N_HEADS, HEAD_DIM, HEAD_DIM), F32),
            jnp.zeros((bp, CONV_TAPS - 1, 3 * GROUP_W), F32), valid=CHUNK, **common)
        for acc, val in zip(outs_p, (c1, n1, m1, s1, cv1, mk, mv)):
            acc.append(val)
        hs, c2, n2, m2, s2, cv2 = _trunk_layer(
            hs, cache_mem_k[l], cache_mem_v[l],
            state_mlstm_C[l], state_mlstm_n[l], state_mlstm_m[l], state_gdn_S[l], state_gdn_conv[l],
            valid=ts, **common)
        for acc, val in zip(outs_s, (c2, n2, m2, s2, cv2)):
            acc.append(val)
    return (hp, hs, *[jnp.stack(a) for a in outs_p], *[jnp.stack(a) for a in outs_s])
```

```python
import functools
import math

import jax
import jax.numpy as jnp
from jax import lax
from jax.experimental import pallas as pl
from jax.experimental.pallas import tpu as pltpu

F32 = jnp.float32
BF16 = jnp.bfloat16
EPS = 1e-6
N_HEADS = 4
HEAD_DIM = 128
GROUP_W = N_HEADS * HEAD_DIM
CONV_TAPS = 4
N_XHEADS = 4
CHUNK = 64
GATE_LANES = 128
NEG_BIG = -1e30
HIST_ROW = 8

VMEM_LIMIT_BYTES = 56 * 1024 * 1024
ROW_TILE = 512
MIXER_TILE = 256
MIXER_SEQS = 2
ATTN_MIN_ROWS = 64

_NT = (((1,), (1,)), ((), ()))
_TN = (((0,), (0,)), ((), ()))


def _rms(x, w):
    return x * lax.rsqrt(jnp.mean(x * x, axis=-1, keepdims=True) + EPS) * w


def _dot(a, b):
    return jnp.dot(a, b, preferred_element_type=F32)


def _sigmoid(x):
    return 1.0 / (1.0 + jnp.exp(-x))


def _params(sem):
    return pltpu.CompilerParams(dimension_semantics=sem, vmem_limit_bytes=VMEM_LIMIT_BYTES)


def _in_proj_body(x_ref, nw_ref, w_ref, wg_ref, proj_ref, gate_ref, *, col_tile):
    assert col_tile == GROUP_W
    xn = _rms(x_ref[...], nw_ref[...]).astype(BF16)
    for c0 in range(0, w_ref.shape[1], col_tile):
        y = _dot(xn, w_ref[:, c0:c0 + col_tile])
        if c0 == 3 * GROUP_W:
            y = _sigmoid(y)
        elif c0 == 7 * GROUP_W:
            y = y * _sigmoid(y)
        proj_ref[:, c0:c0 + col_tile] = y.astype(BF16)
    gate_ref[...] = lax.dot_general(wg_ref[...], xn, _NT, preferred_element_type=F32).reshape(gate_ref.shape)


def _in_proj(x2d, norm_w, w_main, w_gate, seq_len):
    rows, d = x2d.shape
    n = w_main.shape[1]
    n_g = w_gate.shape[0]
    tm = min(ROW_TILE, rows)
    assert rows % tm == 0 and rows % seq_len == 0
    bsz = rows // seq_len
    direct = seq_len % tm == 0
    if direct:
        gate_spec = pl.BlockSpec((1, n_g, tm), lambda i: (i // (seq_len // tm), 0, i % (seq_len // tm)))
        gate_shape = jax.ShapeDtypeStruct((bsz, n_g, seq_len), F32)
    else:
        gate_spec = pl.BlockSpec((n_g, tm), lambda i: (0, i))
        gate_shape = jax.ShapeDtypeStruct((n_g, rows), F32)
    proj, gates = pl.pallas_call(
        functools.partial(_in_proj_body, col_tile=512),
        grid=(rows // tm,),
        in_specs=[
            pl.BlockSpec((tm, d), lambda i: (i, 0)),
            pl.BlockSpec((1, d), lambda i: (0, 0)),
            pl.BlockSpec((d, n), lambda i: (0, 0)),
            pl.BlockSpec((n_g, d), lambda i: (0, 0)),
        ],
        out_specs=[pl.BlockSpec((tm, n), lambda i: (i, 0)), gate_spec],
        out_shape=[jax.ShapeDtypeStruct((rows, n), BF16), gate_shape],
        compiler_params=_params(("arbitrary",)),
        name="in_proj",
    )(x2d, norm_w, w_main, w_gate)
    if not direct:
        gates = gates.reshape(n_g, bsz, seq_len).transpose(1, 0, 2)
    return proj, gates


def _mixer_body(proj_ref, gate_ref, c0_ref, n0_ref, m0_ref, s0_ref, conv0_ref,
                gpar_ref, convw_ref, anw_ref, bnw_ref, spread_ref,
                h_ref, c1_ref, n1_ref, m1_ref, s1_ref, conv1_ref,
                c_sc, n_sc, m_sc, s_sc, u_sc, qn_sc, kf_sc, v_sc,
                z_sc, st_sc, dn_sc, q_sc, og_sc, zg_sc, dc_sc, sv_sc, nc_sc, lb_sc, qu_sc,
                *, n_seq, n_chunks, n_tiles, valid, pipelined):
    L = CHUNK
    T = n_chunks * L
    n_ch = n_chunks * N_HEADS
    slots = 2 if pipelined else 1
    step = pl.program_id(0)
    scale = HEAD_DIM ** -0.5
    heads = range(N_HEADS)
    chunks = range(n_chunks)
    seqs = range(n_seq)
    if pipelined:
        j_a = lax.rem(jnp.minimum(step, pl.num_programs(0) - 2), n_tiles)
        j_b = lax.rem(step + (n_tiles - 1), n_tiles)
        has_b = step >= 1
        slot_a = step & 1
        slot_b = 1 - slot_a
    else:
        j_a = j_b = slot_a = slot_b = 0
        has_b = True
    handoff = (z_sc, st_sc, dn_sc, q_sc, og_sc, zg_sc, dc_sc, sv_sc, nc_sc, lb_sc, qu_sc)

    def load_state():
        c_sc[...] = c0_ref[...].reshape(c_sc.shape)
        s_sc[...] = s0_ref[...].reshape(s_sc.shape)
        m_sc[...] = jnp.zeros(m_sc.shape, F32)
        for s in seqs:
            for h in heads:
                n_sc[s * N_HEADS + h, 0:1, :] = n0_ref[s, h:h + 1, :]
            m_sc[s, 0:1, N_HEADS:2 * N_HEADS] = m0_ref[s]

    def load_conv_history():
        u_sc[:, HIST_ROW - (CONV_TAPS - 1):HIST_ROW, :] = conv0_ref[...]

    if pipelined:
        @pl.when(step == 0)
        def _zero_scratch():
            for ref in handoff + (c_sc, n_sc, m_sc, s_sc):
                ref[...] = jnp.zeros(ref.shape, ref.dtype)

        pl.when(j_a == 0)(load_conv_history)
        pl.when(has_b & (j_b == 0))(load_state)
    else:
        load_conv_history()
        load_state()

    lane = lax.broadcasted_iota(jnp.int32, (L, 2 * L), 1)
    tok = lax.broadcasted_iota(jnp.int32, (L, 2 * L), 0)
    src = lane & (L - 1)
    left = lane < L
    tri_incl = tok >= src
    tri_strict = tok > src
    eye_left = jnp.where(left & (tok == src), 1.0, 0.0)
    lane_row = lax.broadcasted_iota(jnp.int32, (1, GATE_LANES), 1)
    head_lanes = (lane_row >= N_HEADS) & (lane_row < 2 * N_HEADS)
    base = 4 * GROUP_W
    ch = [(c, h) for c in chunks for h in heads]
    idx = {p: i for i, p in enumerate(ch)}

    def two(x):
        return jnp.concatenate([x, x], axis=0)

    def rows(c):
        return slice(c * L, (c + 1) * L)

    def hcols(h, group=0):
        return slice(group * GROUP_W + h * HEAD_DIM, group * GROUP_W + (h + 1) * HEAD_DIM)

    def highest(a, b, dims=None):
        if dims is None:
            return jnp.dot(a, b, precision=lax.Precision.HIGHEST, preferred_element_type=F32)
        return lax.dot_general(a, b, dims, precision=lax.Precision.HIGHEST, preferred_element_type=F32)

    def phase_a(s):
        sa = s * slots + slot_a
        hd = s * N_HEADS
        n_g = 4 * N_HEADS
        row_g = lax.broadcasted_iota(jnp.int32, (n_g, T), 0)
        pre = gate_ref[s] + gpar_ref[0]
        e = jnp.exp(-jnp.abs(pre))
        t = jnp.log1p(e)
        act = jnp.where(row_g < 4, pre,
                        jnp.where(row_g < 8, -(jnp.maximum(-pre, 0.0) + t),
                                  jnp.where(row_g < 12, -jnp.exp(gpar_ref[1]) * (jnp.maximum(pre, 0.0) + t),
                                            jnp.where(pre >= 0.0, 1.0, e) / (1.0 + e))))
        if valid < L:
            tok_g = lax.broadcasted_iota(jnp.int32, (n_g, T), 1)
            act = jnp.where(tok_g >= valid, jnp.where(row_g < 4, NEG_BIG, 0.0), act)
        r_t = lax.broadcasted_iota(jnp.int32, (T, T), 0)
        c_t = lax.broadcasted_iota(jnp.int32, (T, T), 1)
        shift = int(math.log2(L))
        within = jnp.where((r_t <= c_t) & ((r_t >> shift) == (c_t >> shift)), 1.0, 0.0)
        z_g = jnp.where((row_g >= 4) & (row_g < 12), highest(act, within), act)
        pick = jnp.where(lax.broadcasted_iota(jnp.int32, (n_g, GATE_LANES), 0)
                         == lax.broadcasted_iota(jnp.int32, (n_g, GATE_LANES), 1), 1.0, 0.0)
        z_t = highest(z_g, pick, _TN)
        z_sc[sa] = z_t
        z = [z_t[rows(c), :] for c in chunks]

        def spread(x):
            hi = x.astype(BF16)
            lo = (x - hi.astype(F32)).astype(BF16)
            return lax.dot_general(jnp.concatenate([hi, lo], axis=1), spread_ref[...], _NT,
                                   preferred_element_type=F32)

        zb = [spread(z[c]) for c in chunks]

        def zcol(c, k):
            return zb[c][:, k * GATE_LANES:(k + 1) * GATE_LANES]
        r_d = lax.broadcasted_iota(jnp.int32, (T, 2 * L), 0)
        c_d = lax.broadcasted_iota(jnp.int32, (T, 2 * L), 1)
        zt = [highest(z_g, jnp.where(r_d == c * L + (c_d & (L - 1)), 1.0, 0.0)) for c in chunks]
        q_sc[sa] = proj_ref[s, :, 0:GROUP_W]
        og_sc[sa] = proj_ref[s, :, 3 * GROUP_W:4 * GROUP_W]
        zg_sc[sa] = proj_ref[s, :, base + 3 * GROUP_W:base + 4 * GROUP_W]
        yield

        u_sc[s, HIST_ROW:HIST_ROW + T, :] = proj_ref[s, :, base:base + 3 * GROUP_W].astype(F32)
        for blk in range(3 * N_HEADS):
            cols = slice(blk * HEAD_DIM, (blk + 1) * HEAD_DIM)
            y = convw_ref[0:1, cols] * u_sc[s, HIST_ROW - 3:HIST_ROW - 3 + T, cols]
            for tap in range(1, CONV_TAPS):
                y = y + convw_ref[tap:tap + 1, cols] * u_sc[s, HIST_ROW - 3 + tap:HIST_ROW - 3 + tap + T, cols]
            y = y * _sigmoid(y)
            kind, h = divmod(blk, N_HEADS)
            if kind == 0:
                qn_sc[hd + h] = (y * lax.rsqrt(jnp.sum(y * y, axis=-1, keepdims=True) + EPS) * scale).astype(BF16)
            elif kind == 1:
                kf_sc[hd + h] = y * lax.rsqrt(jnp.sum(y * y, axis=-1, keepdims=True) + EPS)
            else:
                v_sc[hd + h] = y
        t_valid = T if valid == L else valid
        u_sc[s, HIST_ROW - 3:HIST_ROW, :] = u_sc[s, HIST_ROW + t_valid - 3:HIST_ROW + t_valid, :]
        yield

        def a_k(c, h):
            return proj_ref[s, rows(c), hcols(h, 1)]

        qk_a = {(c, h): lax.dot_general(proj_ref[s, rows(c), hcols(h)], two(a_k(c, h)), _NT,
                                        preferred_element_type=F32) for c, h in ch}
        qk_b = {}
        for c, h in ch:
            kn = kf_sc[hd + h, rows(c), :].astype(BF16)
            qk_b[c, h] = lax.dot_general(jnp.concatenate([qn_sc[hd + h, rows(c), :], kn], axis=0), two(kn), _NT,
                                         preferred_element_type=F32)
        yield

        d_a, m_a, s_a, kw_a = {}, {}, {}, {}
        for c, h in ch:
            ig_row, b_row = zt[c][h:h + 1, :], zt[c][4 + h:5 + h, :]
            d_a[c, h] = jnp.where(tri_incl, zcol(c, 4 + h) - b_row + ig_row, NEG_BIG)
        for c, h in ch:
            m_a[c, h] = jnp.max(d_a[c, h], axis=1, keepdims=True)
            st_sc[sa, rows(c), 4 + h:5 + h] = m_a[c, h]
        yield
        for c, h in ch:
            s_a[c, h] = jnp.where(left, qk_a[c, h] * scale * jnp.exp(d_a[c, h] - m_a[c, h]), 0.0)
        for c, h in ch:
            st_sc[sa, rows(c), h:h + 1] = jnp.sum(s_a[c, h], axis=1, keepdims=True)
        for c, h in ch:
            ig_b, b_b = zcol(c, h), zcol(c, 4 + h)
            w_s = jnp.exp(b_b[L - 1:L, :] - b_b + ig_b - m_a[c, h][L - 1:L, :])
            kw = a_k(c, h).astype(F32) * scale * w_s
            dn_sc[sa, idx[c, h]:idx[c, h] + 1, :] = jnp.sum(kw, axis=0, keepdims=True)
            kw_a[c, h] = kw.astype(BF16)
        yield
        for c, h in ch:
            v = proj_ref[s, rows(c), hcols(h, 2)]
            sv_sc[sa * n_ch + idx[c, h]] = _dot(s_a[c, h].astype(BF16), two(v))
            dc_sc[sa * n_ch + idx[c, h]] = lax.dot_general(kw_a[c, h], v, _TN, preferred_element_type=F32)
        yield

        decay, r_b = {}, {}
        for c, h in ch:
            g_row = zt[c][8 + h:9 + h, :]
            decay[c, h] = jnp.exp(jnp.where(tri_incl, zcol(c, 8 + h) - g_row, NEG_BIG))
            a2 = jnp.where(tri_strict, zcol(c, 12 + h) * decay[c, h] * qk_b[c, h][L:, :], 0.0)
            r_b[c, h] = jnp.where(left, eye_left, -a2)
        yield
        for _ in range(int(math.log2(L))):
            for p in ch:
                r = r_b[p]
                out = _dot(jnp.where(left, 0.0, r).astype(BF16), two(r.astype(BF16)))
                r_b[p] = jnp.where(left, r + out, out)
            yield
        uk = {}
        for c, h in ch:
            beta = zcol(c, 12 + h)
            rhs = jnp.concatenate([beta * v_sc[hd + h, rows(c), :],
                                   beta * jnp.exp(zcol(c, 8 + h)) * kf_sc[hd + h, rows(c), :]], axis=1).astype(BF16)
            uk[c, h] = _dot(jnp.where(left, r_b[c, h], 0.0).astype(BF16), two(rhs)).astype(BF16)
        yield
        for c, h in ch:
            g_col = zcol(c, 8 + h)
            g_last = g_col[L - 1:L, :]
            qkd = jnp.where(left, qk_b[c, h][:L, :] * decay[c, h], 0.0).astype(BF16)
            qu = _dot(qkd, two(uk[c, h]))
            kd = (kf_sc[hd + h, rows(c), :] * jnp.exp(g_last - g_col)).astype(BF16)
            ku = lax.dot_general(kd, uk[c, h], _TN, preferred_element_type=F32)
            q_eff = jnp.exp(g_col) * qn_sc[hd + h, rows(c), :].astype(F32) - qu[:, HEAD_DIM:]
            i = sa * n_ch + idx[c, h]
            qu_sc[i] = qu[:, :HEAD_DIM]
            nc_sc[i] = ku[:, :HEAD_DIM]
            lb_sc[i] = jnp.concatenate([ku[:, HEAD_DIM:], q_eff], axis=0).astype(BF16)
        yield

    def phase_b(s):
        sb = s * slots + slot_b
        hd = s * N_HEADS
        for c in chunks:
            zc = z_sc[sb, rows(c), :]
            stc = st_sc[sb, rows(c), :]
            base_i = sb * n_ch + c * N_HEADS
            c_prev = [c_sc[hd + h] for h in heads]
            s_prev = [s_sc[hd + h] for h in heads]
            n_prev = [n_sc[hd + h, 0:1, :] for h in heads]
            qc = [_dot(q_sc[sb, rows(c), hcols(h)], c_prev[h].astype(BF16)) for h in heads]
            rm = [_dot(lb_sc[base_i + h], s_prev[h].astype(BF16)) for h in heads]
            m_prev = m_sc[s, 0:1, :]
            inter = zc + m_prev
            m = jnp.maximum(inter, stc)
            m_last = jnp.where(head_lanes, m[L - 1:L, :], 0.0)
            decay0_t = jnp.exp(zc[L - 1:L, :] + m_prev - m_last)
            f_new_t = jnp.exp(stc[L - 1:L, :] - m_last)
            m_sc[s, 0:1, :] = m_last
            for h in heads:
                hl = slice(N_HEADS + h, N_HEADS + h + 1)
                c_sc[hd + h] = decay0_t[:, hl] * c_prev[h] + f_new_t[:, hl] * dc_sc[base_i + h]
                n_sc[hd + h, 0:1, :] = (decay0_t[:, hl] * n_prev[h]
                                        + f_new_t[:, hl] * dn_sc[sb, c * N_HEADS + h:c * N_HEADS + h + 1, :])
            for h in heads:
                s_sc[hd + h] = (jnp.exp(zc[L - 1:L, 8 + h:9 + h]) * s_prev[h] + nc_sc[base_i + h]
                                - rm[h][:HEAD_DIM, :])
            yield
            w_inter_t = jnp.exp(inter - m)
            w_intra_t = jnp.exp(stc - m)
            inv_cap_t = jnp.exp(-m)
            qn = [jnp.sum(q_sc[sb, rows(c), hcols(h)].astype(F32) * n_prev[h], axis=-1, keepdims=True)
                  for h in heads]
            num, cap, o_b = [], [], []
            for h in heads:
                hl = slice(N_HEADS + h, N_HEADS + h + 1)
                w_inter, w_intra = w_inter_t[:, hl], w_intra_t[:, hl]
                num.append(w_inter * qc[h] + w_intra * sv_sc[base_i + h])
                den = w_inter * qn[h] + w_intra * stc[:, h:h + 1]
                cap.append(jnp.maximum(jnp.abs(den), inv_cap_t[:, hl]))
                o_b.append(rm[h][HEAD_DIM:, :] + qu_sc[base_i + h])
            ms_a = [jnp.mean(x * x, axis=-1, keepdims=True) for x in num]
            ms_b = [jnp.mean(x * x, axis=-1, keepdims=True) for x in o_b]
            for h in heads:
                hn = num[h] * lax.rsqrt(ms_a[h] + EPS * cap[h] * cap[h]) * anw_ref[...]
                h_ref[s, rows(c), hcols(h)] = (og_sc[sb, rows(c), hcols(h)].astype(F32) * hn).astype(BF16)
            for h in heads:
                on = o_b[h] * lax.rsqrt(ms_b[h] + EPS) * bnw_ref[...]
                h_ref[s, rows(c), hcols(h, 1)] = (on * zg_sc[sb, rows(c), hcols(h)].astype(F32)).astype(BF16)
            yield

    steps_a, steps_b = 9 + int(math.log2(L)), 2 * n_chunks
    gens_a = [phase_a(s) for s in seqs]
    gens_b = [phase_b(s) for s in seqs]

    def advance(gens):
        for g in gens:
            next(g)

    if pipelined:
        done_b = 0
        for k in range(steps_a):
            while done_b < steps_b and done_b * steps_a <= k * steps_b:
                advance(gens_b)
                done_b += 1
            advance(gens_a)
        for _ in range(steps_b - done_b):
            advance(gens_b)
    else:
        for _ in range(steps_a):
            advance(gens_a)
        for _ in range(steps_b):
            advance(gens_b)
    assert all(next(g, None) is None for g in gens_a + gens_b)

    def store_state():
        c1_ref[...] = c_sc[...].reshape(c1_ref.shape)
        s1_ref[...] = s_sc[...].reshape(s1_ref.shape)
        for s in seqs:
            for h in heads:
                n1_ref[s, h:h + 1, :] = n_sc[s * N_HEADS + h, 0:1, :]
            m1_ref[s] = m_sc[s, 0:1, N_HEADS:2 * N_HEADS]

    def store_conv_history():
        conv1_ref[...] = u_sc[:, HIST_ROW - 3:HIST_ROW, :]

    if pipelined:
        pl.when(has_b & (j_b == n_tiles - 1))(store_state)
        pl.when(j_a == n_tiles - 1)(store_conv_history)
    else:
        store_state()
        store_conv_history()


def _spread_matrix():
    n = lax.broadcasted_iota(jnp.int32, (4 * N_HEADS * GATE_LANES, 2 * GATE_LANES), 0)
    k = lax.broadcasted_iota(jnp.int32, (4 * N_HEADS * GATE_LANES, 2 * GATE_LANES), 1)
    return ((k % GATE_LANES) == (n // GATE_LANES)).astype(BF16)


def _mixers(proj, gates, c0, n0, m0, s0, conv0, gate_par, conv_w, a_norm_w, b_norm_w, *, valid, tile, n_seq):
    bsz, t, n = proj.shape
    tile = min(tile, t)
    assert t % tile == 0 and tile % CHUNK == 0 and bsz % n_seq == 0
    assert valid == CHUNK or (t == CHUNK and CONV_TAPS - 1 <= valid < CHUNK)
    n_chunks = tile // CHUNK
    n_tiles = t // tile
    pipelined = n_tiles > 1
    slots = 2 if pipelined else 1
    n_ch = n_chunks * N_HEADS
    cw = 3 * GROUP_W
    n_groups = bsz // n_seq
    total = n_groups * n_tiles
    if pipelined:
        pair_a = lambda s: jnp.minimum(s, total - 1)
        pair_b = lambda s: jnp.maximum(s - 1, 0)
    else:
        pair_a = pair_b = lambda s: s
    tile_a = lambda s: (pair_a(s) // n_tiles, pair_a(s) % n_tiles)
    tile_b = lambda s: (pair_b(s) // n_tiles, pair_b(s) % n_tiles)
    st4 = lambda s: (tile_b(s)[0], 0, 0, 0)
    st3 = lambda s: (tile_b(s)[0], 0, 0)
    cst2 = lambda s: (0, 0)
    cst3 = lambda s: (0, 0, 0)
    conv_spec = pl.BlockSpec((n_seq, CONV_TAPS - 1, cw), lambda s: (tile_a(s)[0], 0, 0))
    state_specs = [
        pl.BlockSpec((n_seq, N_HEADS, HEAD_DIM, HEAD_DIM), st4),
        pl.BlockSpec((n_seq, N_HEADS, HEAD_DIM), st3),
        pl.BlockSpec((n_seq, 1, N_HEADS), st3),
        pl.BlockSpec((n_seq, N_HEADS, HEAD_DIM, HEAD_DIM), st4),
        conv_spec,
    ]
    state_shapes = [
        jax.ShapeDtypeStruct((bsz, N_HEADS, HEAD_DIM, HEAD_DIM), F32),
        jax.ShapeDtypeStruct((bsz, N_HEADS, HEAD_DIM), F32),
        jax.ShapeDtypeStruct((bsz, 1, N_HEADS), F32),
        jax.ShapeDtypeStruct((bsz, N_HEADS, HEAD_DIM, HEAD_DIM), F32),
        jax.ShapeDtypeStruct((bsz, CONV_TAPS - 1, cw), F32),
    ]
    ns, nh = n_seq * slots, n_seq * N_HEADS
    return pl.pallas_call(
        functools.partial(_mixer_body, n_seq=n_seq, n_chunks=n_chunks, n_tiles=n_tiles, valid=valid,
                          pipelined=pipelined),
        grid=(total + (1 if pipelined else 0),),
        in_specs=[
            pl.BlockSpec((n_seq, tile, n), lambda s: (*tile_a(s), 0)),
            pl.BlockSpec((n_seq, 4 * N_HEADS, tile), lambda s: (tile_a(s)[0], 0, tile_a(s)[1])),
            *state_specs,
            pl.BlockSpec((2, 4 * N_HEADS, tile), cst3),
            pl.BlockSpec((CONV_TAPS, cw), cst2),
            pl.BlockSpec((1, HEAD_DIM), cst2),
            pl.BlockSpec((1, HEAD_DIM), cst2),
            pl.BlockSpec((4 * N_HEADS * GATE_LANES, 2 * GATE_LANES), cst2),
        ],
        out_specs=[pl.BlockSpec((n_seq, tile, 2 * GROUP_W), lambda s: (*tile_b(s), 0)), *state_specs],
        out_shape=[jax.ShapeDtypeStruct((bsz, t, 2 * GROUP_W), BF16), *state_shapes],
        scratch_shapes=[
            pltpu.VMEM((nh, HEAD_DIM, HEAD_DIM), F32),
            pltpu.VMEM((nh, 8, HEAD_DIM), F32),
            pltpu.VMEM((n_seq, 8, GATE_LANES), F32),
            pltpu.VMEM((nh, HEAD_DIM, HEAD_DIM), F32),
            pltpu.VMEM((n_seq, HIST_ROW + tile, cw), F32),
            pltpu.VMEM((nh, tile, HEAD_DIM), BF16),
            pltpu.VMEM((nh, tile, HEAD_DIM), F32),
            pltpu.VMEM((nh, tile, HEAD_DIM), F32),
            pltpu.VMEM((ns, tile, GATE_LANES), F32),
            pltpu.VMEM((ns, tile, GATE_LANES), F32),
            pltpu.VMEM((ns, n_ch, HEAD_DIM), F32),
            pltpu.VMEM((ns, tile, GROUP_W), BF16),
            pltpu.VMEM((ns, tile, GROUP_W), BF16),
            pltpu.VMEM((ns, tile, GROUP_W), BF16),
            pltpu.VMEM((ns * n_ch, HEAD_DIM, HEAD_DIM), F32),
            pltpu.VMEM((ns * n_ch, CHUNK, HEAD_DIM), F32),
            pltpu.VMEM((ns * n_ch, HEAD_DIM, HEAD_DIM), F32),
            pltpu.VMEM((ns * n_ch, HEAD_DIM + CHUNK, HEAD_DIM), BF16),
            pltpu.VMEM((ns * n_ch, CHUNK, HEAD_DIM), F32),
        ],
        compiler_params=_params(("arbitrary",)),
        name="mixers",
    )(proj, gates, c0, n0, m0, s0, conv0, gate_par, conv_w, a_norm_w, b_norm_w, _spread_matrix())


def _mem_kv_body(mem_ref, nw_ref, wkv_ref, k_ref, v_ref, kb_ref, vb_ref):
    nb, n_mem, d = mem_ref.shape
    dh = d // N_XHEADS
    mn = _rms(mem_ref[...].reshape(nb * n_mem, d), nw_ref[...]).astype(BF16)
    for w0, f_ref, b_ref in ((0, k_ref, kb_ref), (d, v_ref, vb_ref)):
        y = _dot(mn, wkv_ref[:, w0:w0 + d])
        b_ref[...] = y.astype(BF16).reshape(nb, n_mem, d)
        for h in range(N_XHEADS):
            f_ref[:, :, h, :] = y[:, h * dh:(h + 1) * dh].reshape(nb, n_mem, dh)


def _mem_kv(mem, norm_w, wkv):
    bsz, n_mem, d = mem.shape
    nb = max(1, min(bsz, ROW_TILE // n_mem))
    assert bsz % nb == 0
    b3 = lambda i: (i, 0, 0)
    b4 = lambda i: (i, 0, 0, 0)
    cst = lambda i: (0, 0)
    dh = d // N_XHEADS
    return pl.pallas_call(
        _mem_kv_body,
        grid=(bsz // nb,),
        in_specs=[pl.BlockSpec((nb, n_mem, d), b3), pl.BlockSpec((1, d), cst), pl.BlockSpec((d, 2 * d), cst)],
        out_specs=[pl.BlockSpec((nb, n_mem, N_XHEADS, dh), b4)] * 2 + [pl.BlockSpec((nb, n_mem, d), b3)] * 2,
        out_shape=([jax.ShapeDtypeStruct((bsz, n_mem, N_XHEADS, dh), F32)] * 2
                   + [jax.ShapeDtypeStruct((bsz, n_mem, d), BF16)] * 2),
        compiler_params=_params(("arbitrary",)),
        name="mem_kv",
    )(mem, norm_w, wkv)


def _attn_body(h_ref, x_ref, wout_ref, nx_ref, wq_ref, mk_ref, mv_ref, wo_ref, o_ref, *, n_sub):
    nb, tm, d = x_ref.shape
    dh = d // N_XHEADS
    rows = nb * tm
    gr = rows // n_sub
    assert gr % tm == 0 or tm % gr == 0
    per = max(1, gr // tm)
    hcols = [slice(h * dh, (h + 1) * dh) for h in range(N_XHEADS)]

    def heads_of(ref, b):
        if len(ref.shape) == 4:
            return [ref[b, :, h, :].astype(BF16) for h in range(N_XHEADS)]
        return [ref[b, :, c] for c in hcols]

    def group_rows(ref, g):
        if per > 1 or gr == tm:
            return ref[g * per:(g + 1) * per].reshape(gr, d)
        return ref[(g * gr) // tm, (g * gr) % tm:(g * gr) % tm + gr, :]

    groups = range(n_sub)
    x1 = [group_rows(x_ref, g) + _dot(group_rows(h_ref, g), wout_ref[...]) for g in groups]
    xn = [_rms(x, nx_ref[...]).astype(BF16) for x in x1]
    q = [_dot(x, wq_ref[...]).astype(BF16) for x in xn]
    unit_rows = min(gr, tm)
    units = [(g, slice(u * unit_rows, (u + 1) * unit_rows), (g * gr + u * unit_rows) // tm)
             for g in groups for u in range(gr // unit_rows)]
    kv = {b: (heads_of(mk_ref, b), heads_of(mv_ref, b)) for b in sorted({b for _, _, b in units})}
    s = [[lax.dot_general(q[g][r, c], kh, _NT, preferred_element_type=F32) * (dh ** -0.5)
          for c, kh in zip(hcols, kv[b][0])] for g, r, b in units]
    mx = [[jnp.max(sh, axis=-1, keepdims=True) for sh in si] for si in s]
    e = [[jnp.exp(sh - mh) for sh, mh in zip(si, mi)] for si, mi in zip(s, mx)]
    den = [[jnp.sum(eh, axis=-1, keepdims=True) for eh in ei] for ei in e]
    p = [[(eh / dh_).astype(BF16) for eh, dh_ in zip(ei, di)] for ei, di in zip(e, den)]
    o = [[_dot(ph, vh).astype(BF16) for ph, vh in zip(pi, kv[b][1])] for pi, (_, _, b) in zip(p, units)]
    for g in groups:
        mine = [oi for oi, (ug, _, _) in zip(o, units) if ug == g]
        acc = x1[g]
        for h, c in enumerate(hcols):
            oh = mine[0][h] if len(mine) == 1 else jnp.concatenate([oi[h] for oi in mine], axis=0)
            acc = acc + _dot(oh, wo_ref[c, :])
        if per > 1 or gr == tm:
            o_ref[g * per:(g + 1) * per] = acc.reshape(per, tm, d)
        else:
            o_ref[(g * gr) // tm, (g * gr) % tm:(g * gr) % tm + gr, :] = acc


def _out_proj_attn(hcat, x, w_out, norm_w, wq, mk, mv, wo):
    bsz, t, d = x.shape
    tm = min(ROW_TILE, t)
    assert t % tm == 0
    nb = max(1, min(bsz, ATTN_MIN_ROWS // tm))
    assert bsz % nb == 0
    bj = lambda b, j: (b, j, 0)
    cst = lambda b, j: (0, 0)
    mem_spec = pl.BlockSpec((nb,) + mk.shape[1:], lambda b, j: (b,) + (0,) * (mk.ndim - 1))
    return pl.pallas_call(
        functools.partial(_attn_body, n_sub=2 if tm >= 256 else 1),
        grid=(bsz // nb, t // tm),
        in_specs=[
            pl.BlockSpec((nb, tm, d), bj),
            pl.BlockSpec((nb, tm, d), bj),
            pl.BlockSpec((d, d), cst),
            pl.BlockSpec((1, d), cst),
            pl.BlockSpec((d, d), cst),
            mem_spec,
            mem_spec,
            pl.BlockSpec((d, d), cst),
        ],
        out_specs=pl.BlockSpec((nb, tm, d), bj),
        out_shape=jax.ShapeDtypeStruct((bsz, t, d), F32),
        compiler_params=_params(("arbitrary", "arbitrary")),
        name="out_proj_attn",
    )(hcat, x, w_out, norm_w, wq, mk, mv, wo)


def _ffn_body(x_ref, nf_ref, w1_ref, w2_ref, nfin_ref, o_ref, *, hid_tile, final_norm):
    x = x_ref[...]
    xn = _rms(x, nf_ref[...]).astype(BF16)
    acc = x
    for c0 in range(0, w1_ref.shape[1], hid_tile):
        hdn = jnp.maximum(_dot(xn, w1_ref[:, c0:c0 + hid_tile]), 0.0)
        acc = acc + _dot((hdn * hdn).astype(BF16), w2_ref[c0:c0 + hid_tile, :])
    o_ref[...] = _rms(acc, nfin_ref[...]) if final_norm else acc


def _ffn(x2d, norm_w, w1, w2, final_w, *, final_norm):
    rows, d = x2d.shape
    dff = w1.shape[1]
    tm = min(ROW_TILE, rows)
    assert rows % tm == 0
    row = lambda i: (i, 0)
    cst = lambda i: (0, 0)
    return pl.pallas_call(
        functools.partial(_ffn_body, hid_tile=1024, final_norm=final_norm),
        grid=(rows // tm,),
        in_specs=[
            pl.BlockSpec((tm, d), row),
            pl.BlockSpec((1, d), cst),
            pl.BlockSpec((d, dff), cst, pipeline_mode=pl.Buffered(1)),
            pl.BlockSpec((dff, d), cst, pipeline_mode=pl.Buffered(1)),
            pl.BlockSpec((1, d), cst),
        ],
        out_specs=pl.BlockSpec((tm, d), row),
        out_shape=jax.ShapeDtypeStruct((rows, d), F32),
        compiler_params=_params(("arbitrary",)),
        name="ffn",
    )(x2d, norm_w, w1, w2, final_w)


def _split_w_in(w_in):
    w = GROUP_W
    g0 = 4 * w
    b0 = g0 + 2 * N_HEADS
    g1 = b0 + 4 * w
    main = jnp.concatenate([w_in[:, :g0], w_in[:, b0:g1]], axis=1).astype(BF16)
    gate = jnp.concatenate([w_in[:, g0:b0], w_in[:, g1:g1 + 2 * N_HEADS]], axis=1)
    return main, gate.T.astype(BF16)


def _gate_params(igate_b, fgate_b, a_log, dt_bias, width):
    zeros = jnp.zeros((N_HEADS,), F32)
    bias = jnp.concatenate([igate_b, fgate_b, dt_bias, zeros]).astype(F32)
    alog = jnp.concatenate([zeros, zeros, a_log.astype(F32), zeros])
    return jnp.broadcast_to(jnp.stack([bias, alog])[:, :, None], (2, 4 * N_HEADS, width))


def _trunk_layer(x, mem_k, mem_v, c0, n0, m0, s0, conv0, *, valid, w_main, w_gate, gate_vecs, lw, final_w,
                 final_norm):
    bsz, t, d = x.shape
    tp = CHUNK if valid < CHUNK else t
    xp = x if tp == t else jnp.pad(x, ((0, 0), (0, tp - t), (0, 0)))
    proj, gates = _in_proj(xp.reshape(bsz * tp, d), lw["norm_mix_w"], w_main, w_gate, tp)
    hcat, c1, n1, m1, s1, conv1 = _mixers(
        proj.reshape(bsz, tp, -1), gates, c0, n0, m0.reshape(bsz, 1, N_HEADS), s0, conv0,
        _gate_params(*gate_vecs, width=min(MIXER_TILE, tp)), lw["gdn_conv_w"], lw["mlstm_norm_w"],
        lw["gdn_norm_w"], valid=valid, tile=MIXER_TILE, n_seq=MIXER_SEQS if tp <= MIXER_TILE else 1)
    x2 = _out_proj_attn(hcat, x, lw["w_out"], lw["norm_x_w"], lw["wq_x"], mem_k, mem_v, lw["wo_x"])
    y = _ffn(x2.reshape(bsz * t, d), lw["norm_ffn_w"], lw["w_ff1"], lw["w_ff2"], final_w,
             final_norm=final_norm)
    return y.reshape(bsz, t, d), c1, n1, m1.reshape(bsz, N_HEADS), s1, conv1


def kernel(x_prompt, x_sample, state_mlstm_C, state_mlstm_n, state_mlstm_m, state_gdn_S, state_gdn_conv, cache_mem_k, cache_mem_v, mem_prompt, norm_mix_w, w_in, mlstm_igate_b, mlstm_fgate_b, mlstm_norm_w, gdn_conv_w, gdn_A_log, gdn_dt_bias, gdn_norm_w, w_out, norm_x_w, norm_mem_w, wq_x, wk_x, wv_x, wo_x, norm_ffn_w, w_ff1, w_ff2, norm_final_w):
    depth = w_in.shape[0]
    bp = x_prompt.shape[0]
    ts = x_sample.shape[1]
    row = lambda a: a.reshape(1, -1).astype(F32)
    hp, hs = x_prompt, x_sample
    outs_p = [[] for _ in range(7)]
    outs_s = [[] for _ in range(5)]
    for l in range(depth):
        w_main, w_gate = _split_w_in(w_in[l])
        gate_vecs = (mlstm_igate_b[l], mlstm_fgate_b[l], gdn_A_log[l], gdn_dt_bias[l])
        lw = dict(
            norm_mix_w=row(norm_mix_w[l]), gdn_conv_w=gdn_conv_w[l].astype(F32),
            mlstm_norm_w=row(mlstm_norm_w[l]), gdn_norm_w=row(gdn_norm_w[l]),
            w_out=w_out[l].astype(BF16), norm_x_w=row(norm_x_w[l]), wq_x=wq_x[l].astype(BF16),
            wo_x=wo_x[l].astype(BF16), norm_ffn_w=row(norm_ffn_w[l]),
            w_ff1=w_ff1[l].astype(BF16), w_ff2=w_ff2[l].astype(BF16))
        common = dict(w_main=w_main, w_gate=w_gate, gate_vecs=gate_vecs, lw=lw, final_w=row(norm_final_w),
                      final_norm=(l == depth - 1))
        wkv = jnp.concatenate([wk_x[l], wv_x[l]], axis=1).astype(BF16)
        mk, mv, mk_b, mv_b = _mem_kv(mem_prompt, row(norm_mem_w[l]), wkv)
        hp, c1, n1, m1, s1, cv1 = _trunk_layer(
            hp, mk_b, mv_b,
            jnp.zeros((bp, N_HEADS, HEAD_DIM, HEAD_DIM), F32), jnp.zeros((bp, N_HEADS, HEAD_DIM), F32),
            jnp.zeros((bp, N_HEADS), F32), jnp.zeros((bp, N_HEADS, HEAD_DIM, HEAD_DIM), F32),
            jnp.zeros((bp, CONV_TAPS - 1, 3 * GROUP_W), F32), valid=CHUNK, **common)
        for acc, val in zip(outs_p, (c1, n1, m1, s1, cv1, mk, mv)):
            acc.append(val)
        hs, c2, n2, m2, s2, cv2 = _trunk_layer(
            hs, cache_mem_k[l], cache_mem_v[l],
            state_mlstm_C[l], state_mlstm_n[l], state_mlstm_m[l], state_gdn_S[l], state_gdn_conv[l],
            valid=ts, **common)
        for acc, val in zip(outs_s, (c2, n2, m2, s2, cv2)):
            acc.append(val)
    return (hp, hs, *[jnp.stack(a) for a in outs_p], *[jnp.stack(a) for a in outs_s])
```

```python
import functools
import math

import jax
import jax.numpy as jnp
from jax import lax
from jax.experimental import pallas as pl
from jax.experimental.pallas import tpu as pltpu

F32 = jnp.float32
BF16 = jnp.bfloat16
EPS = 1e-6
N_HEADS = 4
HEAD_DIM = 128
GROUP_W = N_HEADS * HEAD_DIM
CONV_TAPS = 4
N_XHEADS = 4
CHUNK = 64
GATE_LANES = 128
NEG_BIG = -1e30
HIST_ROW = 8

VMEM_LIMIT_BYTES = 56 * 1024 * 1024
ROW_TILE = 512
MIXER_TILE = 256
MIXER_SEQS = 2
ATTN_MIN_ROWS = 64

_NT = (((1,), (1,)), ((), ()))
_TN = (((0,), (0,)), ((), ()))


def _rms(x, w):
    return x * lax.rsqrt(jnp.mean(x * x, axis=-1, keepdims=True) + EPS) * w


def _dot(a, b):
    return jnp.dot(a, b, preferred_element_type=F32)


def _sigmoid(x):
    return 1.0 / (1.0 + jnp.exp(-x))


def _params(sem):
    return pltpu.CompilerParams(dimension_semantics=sem, vmem_limit_bytes=VMEM_LIMIT_BYTES)


def _in_proj_body(x_ref, nw_ref, w_ref, wg_ref, proj_ref, gate_ref, *, col_tile):
    assert col_tile == GROUP_W
    xn = _rms(x_ref[...], nw_ref[...]).astype(BF16)
    for c0 in range(0, w_ref.shape[1], col_tile):
        y = _dot(xn, w_ref[:, c0:c0 + col_tile])
        if c0 == 3 * GROUP_W:
            y = _sigmoid(y)
        elif c0 == 7 * GROUP_W:
            y = y * _sigmoid(y)
        proj_ref[:, c0:c0 + col_tile] = y.astype(BF16)
    gate_ref[...] = lax.dot_general(wg_ref[...], xn, _NT, preferred_element_type=F32).reshape(gate_ref.shape)


def _in_proj(x2d, norm_w, w_main, w_gate, seq_len):
    rows, d = x2d.shape
    n = w_main.shape[1]
    n_g = w_gate.shape[0]
    tm = min(ROW_TILE, rows)
    assert rows % tm == 0 and rows % seq_len == 0
    bsz = rows // seq_len
    direct = seq_len % tm == 0
    if direct:
        gate_spec = pl.BlockSpec((1, n_g, tm), lambda i: (i // (seq_len // tm), 0, i % (seq_len // tm)))
        gate_shape = jax.ShapeDtypeStruct((bsz, n_g, seq_len), F32)
    else:
        gate_spec = pl.BlockSpec((n_g, tm), lambda i: (0, i))
        gate_shape = jax.ShapeDtypeStruct((n_g, rows), F32)
    proj, gates = pl.pallas_call(
        functools.partial(_in_proj_body, col_tile=512),
        grid=(rows // tm,),
        in_specs=[
            pl.BlockSpec((tm, d), lambda i: (i, 0)),
            pl.BlockSpec((1, d), lambda i: (0, 0)),
            pl.BlockSpec((d, n), lambda i: (0, 0)),
            pl.BlockSpec((n_g, d), lambda i: (0, 0)),
        ],
        out_specs=[pl.BlockSpec((tm, n), lambda i: (i, 0)), gate_spec],
        out_shape=[jax.ShapeDtypeStruct((rows, n), BF16), gate_shape],
        compiler_params=_params(("arbitrary",)),
        name="in_proj",
    )(x2d, norm_w, w_main, w_gate)
    if not direct:
        gates = gates.reshape(n_g, bsz, seq_len).transpose(1, 0, 2)
    return proj, gates


def _mixer_body(proj_ref, gate_ref, c0_ref, n0_ref, m0_ref, s0_ref, conv0_ref,
                gpar_ref, convw_ref, anw_ref, bnw_ref, spread_ref,
                h_ref, c1_ref, n1_ref, m1_ref, s1_ref, conv1_ref,
                c_sc, n_sc, m_sc, s_sc, u_sc, qn_sc, kf_sc, v_sc,
                z_sc, st_sc, dn_sc, q_sc, og_sc, zg_sc, dc_sc, sv_sc, nc_sc, lb_sc, qu_sc,
                *, n_seq, n_chunks, n_tiles, valid, pipelined):
    L = CHUNK
    T = n_chunks * L
    n_ch = n_chunks * N_HEADS
    slots = 2 if pipelined else 1
    step = pl.program_id(0)
    scale = HEAD_DIM ** -0.5
    heads = range(N_HEADS)
    chunks = range(n_chunks)
    seqs = range(n_seq)
    if pipelined:
        j_a = lax.rem(jnp.minimum(step, pl.num_programs(0) - 2), n_tiles)
        j_b = lax.rem(step + (n_tiles - 1), n_tiles)
        has_b = step >= 1
        slot_a = step & 1
        slot_b = 1 - slot_a
    else:
        j_a = j_b = slot_a = slot_b = 0
        has_b = True
    handoff = (z_sc, st_sc, dn_sc, q_sc, og_sc, zg_sc, dc_sc, sv_sc, nc_sc, lb_sc, qu_sc)

    def load_state():
        c_sc[...] = c0_ref[...].reshape(c_sc.shape)
        s_sc[...] = s0_ref[...].reshape(s_sc.shape)
        m_sc[...] = jnp.zeros(m_sc.shape, F32)
        for s in seqs:
            for h in heads:
                n_sc[s * N_HEADS + h, 0:1, :] = n0_ref[s, h:h + 1, :]
            m_sc[s, 0:1, N_HEADS:2 * N_HEADS] = m0_ref[s]

    def load_conv_history():
        u_sc[:, 0:HIST_ROW - (CONV_TAPS - 1), :] = jnp.zeros((n_seq, HIST_ROW - (CONV_TAPS - 1), u_sc.shape[2]), F32)
        u_sc[:, HIST_ROW - (CONV_TAPS - 1):HIST_ROW, :] = conv0_ref[...]

    if pipelined:
        @pl.when(step == 0)
        def _zero_scratch():
            for ref in handoff + (c_sc, n_sc, m_sc, s_sc):
                ref[...] = jnp.zeros(ref.shape, ref.dtype)

        pl.when(j_a == 0)(load_conv_history)
        pl.when(has_b & (j_b == 0))(load_state)
    else:
        load_conv_history()
        load_state()

    lane = lax.broadcasted_iota(jnp.int32, (L, 2 * L), 1)
    tok = lax.broadcasted_iota(jnp.int32, (L, 2 * L), 0)
    src = lane & (L - 1)
    left = lane < L
    tri_incl = tok >= src
    tri_strict = tok > src
    eye_left = jnp.where(left & (tok == src), 1.0, 0.0)
    lane_row = lax.broadcasted_iota(jnp.int32, (1, GATE_LANES), 1)
    head_lanes = (lane_row >= N_HEADS) & (lane_row < 2 * N_HEADS)
    base = 4 * GROUP_W
    ch = [(c, h) for c in chunks for h in heads]
    idx = {p: i for i, p in enumerate(ch)}

    def two(x):
        return jnp.concatenate([x, x], axis=0)

    def rows(c):
        return slice(c * L, (c + 1) * L)

    def hcols(h, group=0):
        return slice(group * GROUP_W + h * HEAD_DIM, group * GROUP_W + (h + 1) * HEAD_DIM)

    ones_sq = jnp.full((HEAD_DIM, HEAD_DIM), 1.0, BF16)

    def highest(a, b, dims=None):
        if dims is None:
            return jnp.dot(a, b, precision=lax.Precision.HIGHEST, preferred_element_type=F32)
        return lax.dot_general(a, b, dims, precision=lax.Precision.HIGHEST, preferred_element_type=F32)

    def phase_a(s):
        sa = s * slots + slot_a
        hd = s * N_HEADS
        n_g = 4 * N_HEADS
        row_g = lax.broadcasted_iota(jnp.int32, (n_g, T), 0)
        pre = gate_ref[s] + gpar_ref[0]
        e = jnp.exp(-jnp.abs(pre))
        t = jnp.log1p(e)
        act = jnp.where(row_g < 4, pre,
                        jnp.where(row_g < 8, -(jnp.maximum(-pre, 0.0) + t),
                                  jnp.where(row_g < 12, -jnp.exp(gpar_ref[1]) * (jnp.maximum(pre, 0.0) + t),
                                            jnp.where(pre >= 0.0, 1.0, e) / (1.0 + e))))
        if valid < L:
            tok_g = lax.broadcasted_iota(jnp.int32, (n_g, T), 1)
            act = jnp.where(tok_g >= valid, jnp.where(row_g < 4, NEG_BIG, 0.0), act)
        r_t = lax.broadcasted_iota(jnp.int32, (T, T), 0)
        c_t = lax.broadcasted_iota(jnp.int32, (T, T), 1)
        shift = int(math.log2(L))
        within = jnp.where((r_t <= c_t) & ((r_t >> shift) == (c_t >> shift)), 1.0, 0.0)
        z_g = jnp.where((row_g >= 4) & (row_g < 12), highest(act, within), act)
        pick = jnp.where(lax.broadcasted_iota(jnp.int32, (n_g, GATE_LANES), 0)
                         == lax.broadcasted_iota(jnp.int32, (n_g, GATE_LANES), 1), 1.0, 0.0)
        z_t = highest(z_g, pick, _TN)
        z_sc[sa] = z_t
        z = [z_t[rows(c), :] for c in chunks]

        def spread(x):
            hi = x.astype(BF16)
            lo = (x - hi.astype(F32)).astype(BF16)
            return lax.dot_general(jnp.concatenate([hi, lo], axis=1), spread_ref[...], _NT,
                                   preferred_element_type=F32)

        zb = [spread(z[c]) for c in chunks]

        def zcol(c, k):
            return zb[c][:, k * GATE_LANES:(k + 1) * GATE_LANES]
        r_d = lax.broadcasted_iota(jnp.int32, (T, 2 * L), 0)
        c_d = lax.broadcasted_iota(jnp.int32, (T, 2 * L), 1)
        zt = [highest(z_g, jnp.where(r_d == c * L + (c_d & (L - 1)), 1.0, 0.0)) for c in chunks]
        q_sc[sa] = proj_ref[s, :, 0:GROUP_W]
        og_sc[sa] = proj_ref[s, :, 3 * GROUP_W:4 * GROUP_W]
        zg_sc[sa] = proj_ref[s, :, base + 3 * GROUP_W:base + 4 * GROUP_W]
        yield

        u_sc[s, HIST_ROW:HIST_ROW + T, :] = proj_ref[s, :, base:base + 3 * GROUP_W].astype(F32)
        for blk in range(3 * N_HEADS):
            cols = slice(blk * HEAD_DIM, (blk + 1) * HEAD_DIM)
            u_all = u_sc[s, :, cols]
            y = convw_ref[0:1, cols] * u_all
            for tap in range(1, CONV_TAPS):
                y = pltpu.roll(y, 1, axis=0) + convw_ref[tap:tap + 1, cols] * u_all
            y = y[HIST_ROW:, :]
            y = y * _sigmoid(y)
            kind, h = divmod(blk, N_HEADS)
            if kind == 0:
                qn_sc[hd + h] = (y * lax.rsqrt(jnp.sum(y * y, axis=-1, keepdims=True) + EPS) * scale).astype(BF16)
            elif kind == 1:
                kf_sc[hd + h] = y * lax.rsqrt(jnp.sum(y * y, axis=-1, keepdims=True) + EPS)
            else:
                v_sc[hd + h] = y
        t_valid = T if valid == L else valid
        u_sc[s, HIST_ROW - 3:HIST_ROW, :] = u_sc[s, HIST_ROW + t_valid - 3:HIST_ROW + t_valid, :]
        yield

        def a_k(c, h):
            return proj_ref[s, rows(c), hcols(h, 1)]

        qk_a = {(c, h): lax.dot_general(proj_ref[s, rows(c), hcols(h)], two(a_k(c, h)), _NT,
                                        preferred_element_type=F32) for c, h in ch}
        qk_b = {}
        for c, h in ch:
            kn = kf_sc[hd + h, rows(c), :].astype(BF16)
            qk_b[c, h] = lax.dot_general(jnp.concatenate([qn_sc[hd + h, rows(c), :], kn], axis=0), two(kn), _NT,
                                         preferred_element_type=F32)
        yield

        d_a, m_a, s_a, kw_a = {}, {}, {}, {}
        for c, h in ch:
            ig_row, b_row = zt[c][h:h + 1, :], zt[c][4 + h:5 + h, :]
            d_a[c, h] = jnp.where(tri_incl, zcol(c, 4 + h) - b_row + ig_row, NEG_BIG)
        for c, h in ch:
            m_a[c, h] = jnp.max(d_a[c, h], axis=1, keepdims=True)
            st_sc[sa, rows(c), 4 + h:5 + h] = m_a[c, h]
        yield
        for c, h in ch:
            s_a[c, h] = jnp.where(left, qk_a[c, h] * scale * jnp.exp(d_a[c, h] - m_a[c, h]), 0.0)
        for c, h in ch:
            ig_b, b_b = zcol(c, h), zcol(c, 4 + h)
            w_s = jnp.exp(b_b[L - 1:L, :] - b_b + ig_b - m_a[c, h][L - 1:L, :])
            kw = a_k(c, h).astype(F32) * scale * w_s
            dn_sc[sa, idx[c, h]:idx[c, h] + 1, :] = jnp.sum(kw, axis=0, keepdims=True)
            kw_a[c, h] = kw.astype(BF16)
        yield
        for c, h in ch:
            v = proj_ref[s, rows(c), hcols(h, 2)]
            sv1 = _dot(s_a[c, h].astype(BF16), jnp.concatenate([two(v), ones_sq], axis=1))
            sv_sc[sa * n_ch + idx[c, h]] = sv1[:, :HEAD_DIM]
            st_sc[sa, rows(c), h:h + 1] = sv1[:, HEAD_DIM:HEAD_DIM + 1]
            dc_sc[sa * n_ch + idx[c, h]] = lax.dot_general(kw_a[c, h], v, _TN, preferred_element_type=F32)
        yield

        decay, r_b = {}, {}
        for c, h in ch:
            g_row = zt[c][8 + h:9 + h, :]
            decay[c, h] = jnp.exp(jnp.where(tri_incl, zcol(c, 8 + h) - g_row, NEG_BIG))
            a2 = jnp.where(tri_strict, zcol(c, 12 + h) * decay[c, h] * qk_b[c, h][L:, :], 0.0)
            r_b[c, h] = jnp.where(left, eye_left, -a2)
        yield
        for _ in range(int(math.log2(L))):
            for p in ch:
                r = r_b[p]
                out = _dot(jnp.where(left, 0.0, r).astype(BF16), two(r.astype(BF16)))
                r_b[p] = jnp.where(left, r + out, out)
            yield
        uk = {}
        for c, h in ch:
            beta = zcol(c, 12 + h)
            rhs = jnp.concatenate([beta * v_sc[hd + h, rows(c), :],
                                   beta * jnp.exp(zcol(c, 8 + h)) * kf_sc[hd + h, rows(c), :]], axis=1).astype(BF16)
            uk[c, h] = _dot(jnp.where(left, r_b[c, h], 0.0).astype(BF16), two(rhs)).astype(BF16)
        yield
        for c, h in ch:
            g_col = zcol(c, 8 + h)
            g_last = g_col[L - 1:L, :]
            qkd = jnp.where(left, qk_b[c, h][:L, :] * decay[c, h], 0.0).astype(BF16)
            qu = _dot(qkd, two(uk[c, h]))
            kd = (kf_sc[hd + h, rows(c), :] * jnp.exp(g_last - g_col)).astype(BF16)
            ku = lax.dot_general(kd, uk[c, h], _TN, preferred_element_type=F32)
            q_eff = jnp.exp(g_col) * qn_sc[hd + h, rows(c), :].astype(F32) - qu[:, HEAD_DIM:]
            i = sa * n_ch + idx[c, h]
            qu_sc[i] = qu[:, :HEAD_DIM]
            nc_sc[i] = ku[:, :HEAD_DIM]
            lb_sc[i] = jnp.concatenate([ku[:, HEAD_DIM:], q_eff], axis=0).astype(BF16)
        yield

    def phase_b(s):
        sb = s * slots + slot_b
        hd = s * N_HEADS
        for c in chunks:
            zc = z_sc[sb, rows(c), :]
            stc = st_sc[sb, rows(c), :]
            base_i = sb * n_ch + c * N_HEADS
            c_prev = [c_sc[hd + h] for h in heads]
            s_prev = [s_sc[hd + h] for h in heads]
            n_prev = [n_sc[hd + h, 0:1, :] for h in heads]
            qc = [_dot(q_sc[sb, rows(c), hcols(h)], c_prev[h].astype(BF16)) for h in heads]
            rm = [_dot(lb_sc[base_i + h], s_prev[h].astype(BF16)) for h in heads]
            m_prev = m_sc[s, 0:1, :]
            inter = zc + m_prev
            m = jnp.maximum(inter, stc)
            m_last = jnp.where(head_lanes, m[L - 1:L, :], 0.0)
            decay0_t = jnp.exp(zc[L - 1:L, :] + m_prev - m_last)
            f_new_t = jnp.exp(stc[L - 1:L, :] - m_last)
            m_sc[s, 0:1, :] = m_last
            for h in heads:
                hl = slice(N_HEADS + h, N_HEADS + h + 1)
                c_sc[hd + h] = decay0_t[:, hl] * c_prev[h] + f_new_t[:, hl] * dc_sc[base_i + h]
                n_sc[hd + h, 0:1, :] = (decay0_t[:, hl] * n_prev[h]
                                        + f_new_t[:, hl] * dn_sc[sb, c * N_HEADS + h:c * N_HEADS + h + 1, :])
            for h in heads:
                s_sc[hd + h] = (jnp.exp(zc[L - 1:L, 8 + h:9 + h]) * s_prev[h] + nc_sc[base_i + h]
                                - rm[h][:HEAD_DIM, :])
            yield
            w_inter_t = jnp.exp(inter - m)
            w_intra_t = jnp.exp(stc - m)
            inv_cap_t = jnp.exp(-m)
            qn = [jnp.sum(q_sc[sb, rows(c), hcols(h)].astype(F32) * n_prev[h], axis=-1, keepdims=True)
                  for h in heads]
            num, cap, o_b = [], [], []
            for h in heads:
                hl = slice(N_HEADS + h, N_HEADS + h + 1)
                w_inter, w_intra = w_inter_t[:, hl], w_intra_t[:, hl]
                num.append(w_inter * qc[h] + w_intra * sv_sc[base_i + h])
                den = w_inter * qn[h] + w_intra * stc[:, h:h + 1]
                cap.append(jnp.maximum(jnp.abs(den), inv_cap_t[:, hl]))
                o_b.append(rm[h][HEAD_DIM:, :] + qu_sc[base_i + h])
            ms_a = [jnp.mean(x * x, axis=-1, keepdims=True) for x in num]
            ms_b = [jnp.mean(x * x, axis=-1, keepdims=True) for x in o_b]
            for h in heads:
                hn = num[h] * lax.rsqrt(ms_a[h] + EPS * cap[h] * cap[h]) * anw_ref[...]
                h_ref[s, rows(c), hcols(h)] = (og_sc[sb, rows(c), hcols(h)].astype(F32) * hn).astype(BF16)
            for h in heads:
                on = o_b[h] * lax.rsqrt(ms_b[h] + EPS) * bnw_ref[...]
                h_ref[s, rows(c), hcols(h, 1)] = (on * zg_sc[sb, rows(c), hcols(h)].astype(F32)).astype(BF16)
            yield

    steps_a, steps_b = 9 + int(math.log2(L)), 2 * n_chunks
    gens_a = [phase_a(s) for s in seqs]
    gens_b = [phase_b(s) for s in seqs]

    def advance(gens):
        for g in gens:
            next(g)

    if pipelined:
        done_b = 0
        for k in range(steps_a):
            while done_b < steps_b and done_b * steps_a <= k * steps_b:
                advance(gens_b)
                done_b += 1
            advance(gens_a)
        for _ in range(steps_b - done_b):
            advance(gens_b)
    else:
        for _ in range(steps_a):
            advance(gens_a)
        for _ in range(steps_b):
            advance(gens_b)
    assert all(next(g, None) is None for g in gens_a + gens_b)

    def store_state():
        c1_ref[...] = c_sc[...].reshape(c1_ref.shape)
        s1_ref[...] = s_sc[...].reshape(s1_ref.shape)
        for s in seqs:
            for h in heads:
                n1_ref[s, h:h + 1, :] = n_sc[s * N_HEADS + h, 0:1, :]
            m1_ref[s] = m_sc[s, 0:1, N_HEADS:2 * N_HEADS]

    def store_conv_history():
        conv1_ref[...] = u_sc[:, HIST_ROW - 3:HIST_ROW, :]

    if pipelined:
        pl.when(has_b & (j_b == n_tiles - 1))(store_state)
        pl.when(j_a == n_tiles - 1)(store_conv_history)
    else:
        store_state()
        store_conv_history()


def _spread_matrix():
    n = lax.broadcasted_iota(jnp.int32, (4 * N_HEADS * GATE_LANES, 2 * GATE_LANES), 0)
    k = lax.broadcasted_iota(jnp.int32, (4 * N_HEADS * GATE_LANES, 2 * GATE_LANES), 1)
    return ((k % GATE_LANES) == (n // GATE_LANES)).astype(BF16)


def _mixers(proj, gates, c0, n0, m0, s0, conv0, gate_par, conv_w, a_norm_w, b_norm_w, *, valid, tile, n_seq):
    bsz, t, n = proj.shape
    tile = min(tile, t)
    assert t % tile == 0 and tile % CHUNK == 0 and bsz % n_seq == 0
    assert valid == CHUNK or (t == CHUNK and CONV_TAPS - 1 <= valid < CHUNK)
    n_chunks = tile // CHUNK
    n_tiles = t // tile
    pipelined = n_tiles > 1
    slots = 2 if pipelined else 1
    n_ch = n_chunks * N_HEADS
    cw = 3 * GROUP_W
    n_groups = bsz // n_seq
    total = n_groups * n_tiles
    if pipelined:
        pair_a = lambda s: jnp.minimum(s, total - 1)
        pair_b = lambda s: jnp.maximum(s - 1, 0)
    else:
        pair_a = pair_b = lambda s: s
    tile_a = lambda s: (pair_a(s) // n_tiles, pair_a(s) % n_tiles)
    tile_b = lambda s: (pair_b(s) // n_tiles, pair_b(s) % n_tiles)
    st4 = lambda s: (tile_b(s)[0], 0, 0, 0)
    st3 = lambda s: (tile_b(s)[0], 0, 0)
    cst2 = lambda s: (0, 0)
    cst3 = lambda s: (0, 0, 0)
    conv_spec = pl.BlockSpec((n_seq, CONV_TAPS - 1, cw), lambda s: (tile_a(s)[0], 0, 0))
    state_specs = [
        pl.BlockSpec((n_seq, N_HEADS, HEAD_DIM, HEAD_DIM), st4),
        pl.BlockSpec((n_seq, N_HEADS, HEAD_DIM), st3),
        pl.BlockSpec((n_seq, 1, N_HEADS), st3),
        pl.BlockSpec((n_seq, N_HEADS, HEAD_DIM, HEAD_DIM), st4),
        conv_spec,
    ]
    state_shapes = [
        jax.ShapeDtypeStruct((bsz, N_HEADS, HEAD_DIM, HEAD_DIM), F32),
        jax.ShapeDtypeStruct((bsz, N_HEADS, HEAD_DIM), F32),
        jax.ShapeDtypeStruct((bsz, 1, N_HEADS), F32),
        jax.ShapeDtypeStruct((bsz, N_HEADS, HEAD_DIM, HEAD_DIM), F32),
        jax.ShapeDtypeStruct((bsz, CONV_TAPS - 1, cw), F32),
    ]
    ns, nh = n_seq * slots, n_seq * N_HEADS
    return pl.pallas_call(
        functools.partial(_mixer_body, n_seq=n_seq, n_chunks=n_chunks, n_tiles=n_tiles, valid=valid,
                          pipelined=pipelined),
        grid=(total + (1 if pipelined else 0),),
        in_specs=[
            pl.BlockSpec((n_seq, tile, n), lambda s: (*tile_a(s), 0)),
            pl.BlockSpec((n_seq, 4 * N_HEADS, tile), lambda s: (tile_a(s)[0], 0, tile_a(s)[1])),
            *state_specs,
            pl.BlockSpec((2, 4 * N_HEADS, tile), cst3),
            pl.BlockSpec((CONV_TAPS, cw), cst2),
            pl.BlockSpec((1, HEAD_DIM), cst2),
            pl.BlockSpec((1, HEAD_DIM), cst2),
            pl.BlockSpec((4 * N_HEADS * GATE_LANES, 2 * GATE_LANES), cst2),
        ],
        out_specs=[pl.BlockSpec((n_seq, tile, 2 * GROUP_W), lambda s: (*tile_b(s), 0)), *state_specs],
        out_shape=[jax.ShapeDtypeStruct((bsz, t, 2 * GROUP_W), BF16), *state_shapes],
        scratch_shapes=[
            pltpu.VMEM((nh, HEAD_DIM, HEAD_DIM), F32),
            pltpu.VMEM((nh, 8, HEAD_DIM), F32),
            pltpu.VMEM((n_seq, 8, GATE_LANES), F32),
            pltpu.VMEM((nh, HEAD_DIM, HEAD_DIM), F32),
            pltpu.VMEM((n_seq, HIST_ROW + tile, cw), F32),
            pltpu.VMEM((nh, tile, HEAD_DIM), BF16),
            pltpu.VMEM((nh, tile, HEAD_DIM), F32),
            pltpu.VMEM((nh, tile, HEAD_DIM), F32),
            pltpu.VMEM((ns, tile, GATE_LANES), F32),
            pltpu.VMEM((ns, tile, GATE_LANES), F32),
            pltpu.VMEM((ns, n_ch, HEAD_DIM), F32),
            pltpu.VMEM((ns, tile, GROUP_W), BF16),
            pltpu.VMEM((ns, tile, GROUP_W), BF16),
            pltpu.VMEM((ns, tile, GROUP_W), BF16),
            pltpu.VMEM((ns * n_ch, HEAD_DIM, HEAD_DIM), F32),
            pltpu.VMEM((ns * n_ch, CHUNK, HEAD_DIM), F32),
            pltpu.VMEM((ns * n_ch, HEAD_DIM, HEAD_DIM), F32),
            pltpu.VMEM((ns * n_ch, HEAD_DIM + CHUNK, HEAD_DIM), BF16),
            pltpu.VMEM((ns * n_ch, CHUNK, HEAD_DIM), F32),
        ],
        compiler_params=_params(("arbitrary",)),
        name="mixers",
    )(proj, gates, c0, n0, m0, s0, conv0, gate_par, conv_w, a_norm_w, b_norm_w, _spread_matrix())


def _mem_kv_body(mem_ref, nw_ref, wkv_ref, k_ref, v_ref, kb_ref, vb_ref):
    nb, n_mem, d = mem_ref.shape
    dh = d // N_XHEADS
    mn = _rms(mem_ref[...].reshape(nb * n_mem, d), nw_ref[...]).astype(BF16)
    for w0, f_ref, b_ref in ((0, k_ref, kb_ref), (d, v_ref, vb_ref)):
        y = _dot(mn, wkv_ref[:, w0:w0 + d])
        b_ref[...] = y.astype(BF16).reshape(nb, n_mem, d)
        for h in range(N_XHEADS):
            f_ref[:, :, h, :] = y[:, h * dh:(h + 1) * dh].reshape(nb, n_mem, dh)


def _mem_kv(mem, norm_w, wkv):
    bsz, n_mem, d = mem.shape
    nb = max(1, min(bsz, ROW_TILE // n_mem))
    assert bsz % nb == 0
    b3 = lambda i: (i, 0, 0)
    b4 = lambda i: (i, 0, 0, 0)
    cst = lambda i: (0, 0)
    dh = d // N_XHEADS
    return pl.pallas_call(
        _mem_kv_body,
        grid=(bsz // nb,),
        in_specs=[pl.BlockSpec((nb, n_mem, d), b3), pl.BlockSpec((1, d), cst), pl.BlockSpec((d, 2 * d), cst)],
        out_specs=[pl.BlockSpec((nb, n_mem, N_XHEADS, dh), b4)] * 2 + [pl.BlockSpec((nb, n_mem, d), b3)] * 2,
        out_shape=([jax.ShapeDtypeStruct((bsz, n_mem, N_XHEADS, dh), F32)] * 2
                   + [jax.ShapeDtypeStruct((bsz, n_mem, d), BF16)] * 2),
        compiler_params=_params(("arbitrary",)),
        name="mem_kv",
    )(mem, norm_w, wkv)


def _attn_body(h_ref, x_ref, wout_ref, nx_ref, wq_ref, mk_ref, mv_ref, wo_ref, o_ref, *, n_sub):
    nb, tm, d = x_ref.shape
    dh = d // N_XHEADS
    rows = nb * tm
    gr = rows // n_sub
    assert gr % tm == 0 or tm % gr == 0
    per = max(1, gr // tm)
    hcols = [slice(h * dh, (h + 1) * dh) for h in range(N_XHEADS)]

    def heads_of(ref, b):
        if len(ref.shape) == 4:
            return [ref[b, :, h, :].astype(BF16) for h in range(N_XHEADS)]
        return [ref[b, :, c] for c in hcols]

    def group_rows(ref, g):
        if per > 1 or gr == tm:
            return ref[g * per:(g + 1) * per].reshape(gr, d)
        return ref[(g * gr) // tm, (g * gr) % tm:(g * gr) % tm + gr, :]

    groups = range(n_sub)
    x1 = [group_rows(x_ref, g) + _dot(group_rows(h_ref, g), wout_ref[...]) for g in groups]
    xn = [_rms(x, nx_ref[...]).astype(BF16) for x in x1]
    q = [_dot(x, wq_ref[...]).astype(BF16) for x in xn]
    unit_rows = min(gr, tm)
    units = [(g, slice(u * unit_rows, (u + 1) * unit_rows), (g * gr + u * unit_rows) // tm)
             for g in groups for u in range(gr // unit_rows)]
    kv = {b: (heads_of(mk_ref, b), heads_of(mv_ref, b)) for b in sorted({b for _, _, b in units})}
    s = [[lax.dot_general(q[g][r, c], kh, _NT, preferred_element_type=F32) * (dh ** -0.5)
          for c, kh in zip(hcols, kv[b][0])] for g, r, b in units]
    mx = [[jnp.max(sh, axis=-1, keepdims=True) for sh in si] for si in s]
    e = [[jnp.exp(sh - mh) for sh, mh in zip(si, mi)] for si, mi in zip(s, mx)]
    den = [[jnp.sum(eh, axis=-1, keepdims=True) for eh in ei] for ei in e]
    p = [[(eh / dh_).astype(BF16) for eh, dh_ in zip(ei, di)] for ei, di in zip(e, den)]
    o = [[_dot(ph, vh).astype(BF16) for ph, vh in zip(pi, kv[b][1])] for pi, (_, _, b) in zip(p, units)]
    for g in groups:
        mine = [oi for oi, (ug, _, _) in zip(o, units) if ug == g]
        acc = x1[g]
        for h, c in enumerate(hcols):
            oh = mine[0][h] if len(mine) == 1 else jnp.concatenate([oi[h] for oi in mine], axis=0)
            acc = acc + _dot(oh, wo_ref[c, :])
        if per > 1 or gr == tm:
            o_ref[g * per:(g + 1) * per] = acc.reshape(per, tm, d)
        else:
            o_ref[(g * gr) // tm, (g * gr) % tm:(g * gr) % tm + gr, :] = acc


def _out_proj_attn(hcat, x, w_out, norm_w, wq, mk, mv, wo):
    bsz, t, d = x.shape
    tm = min(ROW_TILE, t)
    assert t % tm == 0
    nb = max(1, min(bsz, ATTN_MIN_ROWS // tm))
    assert bsz % nb == 0
    bj = lambda b, j: (b, j, 0)
    cst = lambda b, j: (0, 0)
    mem_spec = pl.BlockSpec((nb,) + mk.shape[1:], lambda b, j: (b,) + (0,) * (mk.ndim - 1))
    return pl.pallas_call(
        functools.partial(_attn_body, n_sub=2 if tm >= 256 else 1),
        grid=(bsz // nb, t // tm),
        in_specs=[
            pl.BlockSpec((nb, tm, d), bj),
            pl.BlockSpec((nb, tm, d), bj),
            pl.BlockSpec((d, d), cst),
            pl.BlockSpec((1, d), cst),
            pl.BlockSpec((d, d), cst),
            mem_spec,
            mem_spec,
            pl.BlockSpec((d, d), cst),
        ],
        out_specs=pl.BlockSpec((nb, tm, d), bj),
        out_shape=jax.ShapeDtypeStruct((bsz, t, d), F32),
        compiler_params=_params(("arbitrary", "arbitrary")),
        name="out_proj_attn",
    )(hcat, x, w_out, norm_w, wq, mk, mv, wo)


def _ffn_body(x_ref, nf_ref, w1_ref, w2_ref, nfin_ref, o_ref, *, hid_tile, final_norm):
    x = x_ref[...]
    xn = _rms(x, nf_ref[...]).astype(BF16)
    acc = x
    for c0 in range(0, w1_ref.shape[1], hid_tile):
        hdn = jnp.maximum(_dot(xn, w1_ref[:, c0:c0 + hid_tile]), 0.0)
        acc = acc + _dot((hdn * hdn).astype(BF16), w2_ref[c0:c0 + hid_tile, :])
    o_ref[...] = _rms(acc, nfin_ref[...]) if final_norm else acc


def _ffn(x2d, norm_w, w1, w2, final_w, *, final_norm):
    rows, d = x2d.shape
    dff = w1.shape[1]
    tm = min(ROW_TILE, rows)
    assert rows % tm == 0
    row = lambda i: (i, 0)
    cst = lambda i: (0, 0)
    return pl.pallas_call(
        functools.partial(_ffn_body, hid_tile=1024, final_norm=final_norm),
        grid=(rows // tm,),
        in_specs=[
            pl.BlockSpec((tm, d), row),
            pl.BlockSpec((1, d), cst),
            pl.BlockSpec((d, dff), cst, pipeline_mode=pl.Buffered(1)),
            pl.BlockSpec((dff, d), cst, pipeline_mode=pl.Buffered(1)),
            pl.BlockSpec((1, d), cst),
        ],
        out_specs=pl.BlockSpec((tm, d), row),
        out_shape=jax.ShapeDtypeStruct((rows, d), F32),
        compiler_params=_params(("arbitrary",)),
        name="ffn",
    )(x2d, norm_w, w1, w2, final_w)


def _split_w_in(w_in):
    w = GROUP_W
    g0 = 4 * w
    b0 = g0 + 2 * N_HEADS
    g1 = b0 + 4 * w
    main = jnp.concatenate([w_in[:, :g0], w_in[:, b0:g1]], axis=1).astype(BF16)
    gate = jnp.concatenate([w_in[:, g0:b0], w_in[:, g1:g1 + 2 * N_HEADS]], axis=1)
    return main, gate.T.astype(BF16)


def _gate_params(igate_b, fgate_b, a_log, dt_bias, width):
    zeros = jnp.zeros((N_HEADS,), F32)
    bias = jnp.concatenate([igate_b, fgate_b, dt_bias, zeros]).astype(F32)
    alog = jnp.concatenate([zeros, zeros, a_log.astype(F32), zeros])
    return jnp.broadcast_to(jnp.stack([bias, alog])[:, :, None], (2, 4 * N_HEADS, width))


def _trunk_layer(x, mem_k, mem_v, c0, n0, m0, s0, conv0, *, valid, w_main, w_gate, gate_vecs, lw, final_w,
                 final_norm):
    bsz, t, d = x.shape
    tp = CHUNK if valid < CHUNK else t
    xp = x if tp == t else jnp.pad(x, ((0, 0), (0, tp - t), (0, 0)))
    proj, gates = _in_proj(xp.reshape(bsz * tp, d), lw["norm_mix_w"], w_main, w_gate, tp)
    hcat, c1, n1, m1, s1, conv1 = _mixers(
        proj.reshape(bsz, tp, -1), gates, c0, n0, m0.reshape(bsz, 1, N_HEADS), s0, conv0,
        _gate_params(*gate_vecs, width=min(MIXER_TILE, tp)), lw["gdn_conv_w"], lw["mlstm_norm_w"],
        lw["gdn_norm_w"], valid=valid, tile=MIXER_TILE, n_seq=MIXER_SEQS if tp <= MIXER_TILE else 1)
    x2 = _out_proj_attn(hcat, x, lw["w_out"], lw["norm_x_w"], lw["wq_x"], mem_k, mem_v, lw["wo_x"])
    y = _ffn(x2.reshape(bsz * t, d), lw["norm_ffn_w"], lw["w_ff1"], lw["w_ff2"], final_w,
             final_norm=final_norm)
    return y.reshape(bsz, t, d), c1, n1, m1.reshape(bsz, N_HEADS), s1, conv1


def kernel(x_prompt, x_sample, state_mlstm_C, state_mlstm_n, state_mlstm_m, state_gdn_S, state_gdn_conv, cache_mem_k, cache_mem_v, mem_prompt, norm_mix_w, w_in, mlstm_igate_b, mlstm_fgate_b, mlstm_norm_w, gdn_conv_w, gdn_A_log, gdn_dt_bias, gdn_norm_w, w_out, norm_x_w, norm_mem_w, wq_x, wk_x, wv_x, wo_x, norm_ffn_w, w_ff1, w_ff2, norm_final_w):
    depth = w_in.shape[0]
    bp = x_prompt.shape[0]
    ts = x_sample.shape[1]
    row = lambda a: a.reshape(1, -1).astype(F32)
    hp, hs = x_prompt, x_sample
    outs_p = [[] for _ in range(7)]
    outs_s = [[] for _ in range(5)]
    for l in range(depth):
        w_main, w_gate = _split_w_in(w_in[l])
        gate_vecs = (mlstm_igate_b[l], mlstm_fgate_b[l], gdn_A_log[l], gdn_dt_bias[l])
        lw = dict(
            norm_mix_w=row(norm_mix_w[l]), gdn_conv_w=gdn_conv_w[l].astype(F32),
            mlstm_norm_w=row(mlstm_norm_w[l]), gdn_norm_w=row(gdn_norm_w[l]),
            w_out=w_out[l].astype(BF16), norm_x_w=row(norm_x_w[l]), wq_x=wq_x[l].astype(BF16),
            wo_x=wo_x[l].astype(BF16), norm_ffn_w=row(norm_ffn_w[l]),
            w_ff1=w_ff1[l].astype(BF16), w_ff2=w_ff2[l].astype(BF16))
        common = dict(w_main=w_main, w_gate=w_gate, gate_vecs=gate_vecs, lw=lw, final_w=row(norm_final_w),
                      final_norm=(l == depth - 1))
        wkv = jnp.concatenate([wk_x[l], wv_x[l]], axis=1).astype(BF16)
        mk, mv, mk_b, mv_b = _mem_kv(mem_prompt, row(norm_mem_w[l]), wkv)
        hp, c1, n1, m1, s1, cv1 = _trunk_layer(
            hp, mk_b, mv_b,
            jnp.zeros((bp, N_HEADS, HEAD_DIM, HEAD_DIM), F32), jnp.zeros((bp, N_HEADS, HEAD_DIM), F32),
            jnp.zeros((bp, N_HEADS), F32), jnp.zeros((bp, N_HEADS, HEAD_DIM, HEAD_DIM), F32),
            jnp.zeros((bp, CONV_TAPS - 1, 3 * GROUP_W), F32), valid=CHUNK, **common)
        for acc, val in zip(outs_p, (c1, n1, m1, s1, cv1, mk, mv)):
            acc.append(val)
        hs, c2, n2, m2, s2, cv2 = _trunk_layer(
            hs, cache_mem_k[l], cache_mem_v[l],
            state_mlstm_C[l], state_mlstm_n[l], state_mlstm_m[l], state_gdn_S[l], state_gdn_conv[l],
            valid=ts, **common)
        for acc, val in zip(outs_s, (c2, n2, m2, s2, cv2)):
            acc.append(val)
    return (hp, hs, *[jnp.stack(a) for a in outs_p], *[jnp.stack(a) for a in outs_s])
```

```python
import functools
import math

import jax
import jax.numpy as jnp
from jax import lax
from jax.experimental import pallas as pl
from jax.experimental.pallas import tpu as pltpu

F32 = jnp.float32
BF16 = jnp.bfloat16
EPS = 1e-6
N_HEADS = 4
HEAD_DIM = 128
GROUP_W = N_HEADS * HEAD_DIM
CONV_TAPS = 4
N_XHEADS = 4
CHUNK = 64
GATE_LANES = 128
NEG_BIG = -1e30
HIST_ROW = 8

VMEM_LIMIT_BYTES = 56 * 1024 * 1024
ROW_TILE = 512
MIXER_TILE = 512
MIXER_SEQS = 2
ATTN_MIN_ROWS = 64

_NT = (((1,), (1,)), ((), ()))
_TN = (((0,), (0,)), ((), ()))


def _rms(x, w):
    return x * lax.rsqrt(jnp.mean(x * x, axis=-1, keepdims=True) + EPS) * w


def _dot(a, b):
    return jnp.dot(a, b, preferred_element_type=F32)


def _sigmoid(x):
    return 1.0 / (1.0 + jnp.exp(-x))


def _params(sem):
    return pltpu.CompilerParams(dimension_semantics=sem, vmem_limit_bytes=VMEM_LIMIT_BYTES)


def _in_proj_body(x_ref, nw_ref, w_ref, wg_ref, proj_ref, gate_ref, *, col_tile):
    assert col_tile == GROUP_W
    xn = _rms(x_ref[...], nw_ref[...]).astype(BF16)
    for c0 in range(0, w_ref.shape[1], col_tile):
        y = _dot(xn, w_ref[:, c0:c0 + col_tile])
        if c0 == 3 * GROUP_W:
            y = _sigmoid(y)
        elif c0 == 7 * GROUP_W:
            y = y * _sigmoid(y)
        proj_ref[:, c0:c0 + col_tile] = y.astype(BF16)
    gate_ref[...] = lax.dot_general(wg_ref[...], xn, _NT, preferred_element_type=F32).reshape(gate_ref.shape)


def _in_proj(x2d, norm_w, w_main, w_gate, seq_len):
    rows, d = x2d.shape
    n = w_main.shape[1]
    n_g = w_gate.shape[0]
    tm = min(ROW_TILE, rows)
    assert rows % tm == 0 and rows % seq_len == 0
    bsz = rows // seq_len
    direct = seq_len % tm == 0
    if direct:
        gate_spec = pl.BlockSpec((1, n_g, tm), lambda i: (i // (seq_len // tm), 0, i % (seq_len // tm)))
        gate_shape = jax.ShapeDtypeStruct((bsz, n_g, seq_len), F32)
    else:
        gate_spec = pl.BlockSpec((n_g, tm), lambda i: (0, i))
        gate_shape = jax.ShapeDtypeStruct((n_g, rows), F32)
    proj, gates = pl.pallas_call(
        functools.partial(_in_proj_body, col_tile=512),
        grid=(rows // tm,),
        in_specs=[
            pl.BlockSpec((tm, d), lambda i: (i, 0)),
            pl.BlockSpec((1, d), lambda i: (0, 0)),
            pl.BlockSpec((d, n), lambda i: (0, 0)),
            pl.BlockSpec((n_g, d), lambda i: (0, 0)),
        ],
        out_specs=[pl.BlockSpec((tm, n), lambda i: (i, 0)), gate_spec],
        out_shape=[jax.ShapeDtypeStruct((rows, n), BF16), gate_shape],
        compiler_params=_params(("arbitrary",)),
        name="in_proj",
    )(x2d, norm_w, w_main, w_gate)
    if not direct:
        gates = gates.reshape(n_g, bsz, seq_len).transpose(1, 0, 2)
    return proj, gates


def _mixer_body(proj_ref, gate_ref, c0_ref, n0_ref, m0_ref, s0_ref, conv0_ref,
                gpar_ref, convw_ref, anw_ref, bnw_ref, spread_ref,
                h_ref, c1_ref, n1_ref, m1_ref, s1_ref, conv1_ref,
                c_sc, n_sc, m_sc, s_sc, u_sc, qn_sc, kf_sc, v_sc,
                z_sc, st_sc, dn_sc, q_sc, og_sc, zg_sc, dc_sc, sv_sc, nc_sc, lb_sc, qu_sc,
                *, n_seq, n_chunks, n_tiles, valid, pipelined):
    L = CHUNK
    T = n_chunks * L
    n_ch = n_chunks * N_HEADS
    slots = 2 if pipelined else 1
    step = pl.program_id(0)
    scale = HEAD_DIM ** -0.5
    heads = range(N_HEADS)
    chunks = range(n_chunks)
    seqs = range(n_seq)
    if pipelined:
        j_a = lax.rem(jnp.minimum(step, pl.num_programs(0) - 2), n_tiles)
        j_b = lax.rem(step + (n_tiles - 1), n_tiles)
        has_b = step >= 1
        slot_a = step & 1
        slot_b = 1 - slot_a
    else:
        j_a = j_b = slot_a = slot_b = 0
        has_b = True
    handoff = (z_sc, st_sc, dn_sc, q_sc, og_sc, zg_sc, dc_sc, sv_sc, nc_sc, lb_sc, qu_sc)

    def load_state():
        c_sc[...] = c0_ref[...].reshape(c_sc.shape)
        s_sc[...] = s0_ref[...].reshape(s_sc.shape)
        m_sc[...] = jnp.zeros(m_sc.shape, F32)
        for s in seqs:
            for h in heads:
                n_sc[s * N_HEADS + h, 0:1, :] = n0_ref[s, h:h + 1, :]
            m_sc[s, 0:1, N_HEADS:2 * N_HEADS] = m0_ref[s]

    def load_conv_history():
        u_sc[:, 0:HIST_ROW - (CONV_TAPS - 1), :] = jnp.zeros((n_seq, HIST_ROW - (CONV_TAPS - 1), u_sc.shape[2]), F32)
        u_sc[:, HIST_ROW - (CONV_TAPS - 1):HIST_ROW, :] = conv0_ref[...]

    if pipelined:
        @pl.when(step == 0)
        def _zero_scratch():
            for ref in handoff + (c_sc, n_sc, m_sc, s_sc):
                ref[...] = jnp.zeros(ref.shape, ref.dtype)

        pl.when(j_a == 0)(load_conv_history)
        pl.when(has_b & (j_b == 0))(load_state)
    else:
        load_conv_history()
        load_state()

    lane = lax.broadcasted_iota(jnp.int32, (L, 2 * L), 1)
    tok = lax.broadcasted_iota(jnp.int32, (L, 2 * L), 0)
    src = lane & (L - 1)
    left = lane < L
    tri_incl = tok >= src
    tri_strict = tok > src
    eye_left = jnp.where(left & (tok == src), 1.0, 0.0)
    lane_row = lax.broadcasted_iota(jnp.int32, (1, GATE_LANES), 1)
    head_lanes = (lane_row >= N_HEADS) & (lane_row < 2 * N_HEADS)
    base = 4 * GROUP_W
    ch = [(c, h) for c in chunks for h in heads]
    idx = {p: i for i, p in enumerate(ch)}

    def two(x):
        return jnp.concatenate([x, x], axis=0)

    def rows(c):
        return slice(c * L, (c + 1) * L)

    def hcols(h, group=0):
        return slice(group * GROUP_W + h * HEAD_DIM, group * GROUP_W + (h + 1) * HEAD_DIM)

    ones_sq = jnp.full((HEAD_DIM, HEAD_DIM), 1.0, BF16)

    def highest(a, b, dims=None):
        if dims is None:
            return jnp.dot(a, b, precision=lax.Precision.HIGHEST, preferred_element_type=F32)
        return lax.dot_general(a, b, dims, precision=lax.Precision.HIGHEST, preferred_element_type=F32)

    def phase_a(s):
        sa = s * slots + slot_a
        hd = s * N_HEADS
        n_g = 4 * N_HEADS
        row_g = lax.broadcasted_iota(jnp.int32, (n_g, T), 0)
        pre = gate_ref[s] + gpar_ref[0]
        e = jnp.exp(-jnp.abs(pre))
        t = jnp.log1p(e)
        act = jnp.where(row_g < 4, pre,
                        jnp.where(row_g < 8, -(jnp.maximum(-pre, 0.0) + t),
                                  jnp.where(row_g < 12, -jnp.exp(gpar_ref[1]) * (jnp.maximum(pre, 0.0) + t),
                                            jnp.where(pre >= 0.0, 1.0, e) / (1.0 + e))))
        if valid < L:
            tok_g = lax.broadcasted_iota(jnp.int32, (n_g, T), 1)
            act = jnp.where(tok_g >= valid, jnp.where(row_g < 4, NEG_BIG, 0.0), act)
        r_t = lax.broadcasted_iota(jnp.int32, (T, T), 0)
        c_t = lax.broadcasted_iota(jnp.int32, (T, T), 1)
        shift = int(math.log2(L))
        within = jnp.where((r_t <= c_t) & ((r_t >> shift) == (c_t >> shift)), 1.0, 0.0)
        z_g = jnp.where((row_g >= 4) & (row_g < 12), highest(act, within), act)
        pick = jnp.where(lax.broadcasted_iota(jnp.int32, (n_g, GATE_LANES), 0)
                         == lax.broadcasted_iota(jnp.int32, (n_g, GATE_LANES), 1), 1.0, 0.0)
        z_t = highest(z_g, pick, _TN)
        z_sc[sa] = z_t
        z = [z_t[rows(c), :] for c in chunks]

        def spread(x):
            hi = x.astype(BF16)
            lo = (x - hi.astype(F32)).astype(BF16)
            return lax.dot_general(jnp.concatenate([hi, lo], axis=1), spread_ref[...], _NT,
                                   preferred_element_type=F32)

        zb = [spread(z[c]) for c in chunks]

        def zcol(c, k):
            return zb[c][:, k * GATE_LANES:(k + 1) * GATE_LANES]
        r_d = lax.broadcasted_iota(jnp.int32, (T, 2 * L), 0)
        c_d = lax.broadcasted_iota(jnp.int32, (T, 2 * L), 1)
        zt = [highest(z_g, jnp.where(r_d == c * L + (c_d & (L - 1)), 1.0, 0.0)) for c in chunks]
        q_sc[sa] = proj_ref[s, :, 0:GROUP_W]
        og_sc[sa] = proj_ref[s, :, 3 * GROUP_W:4 * GROUP_W]
        zg_sc[sa] = proj_ref[s, :, base + 3 * GROUP_W:base + 4 * GROUP_W]
        yield

        u_sc[s, HIST_ROW:HIST_ROW + T, :] = proj_ref[s, :, base:base + 3 * GROUP_W].astype(F32)
        for blk in range(3 * N_HEADS):
            cols = slice(blk * HEAD_DIM, (blk + 1) * HEAD_DIM)
            u_all = u_sc[s, :, cols]
            y = convw_ref[0:1, cols] * u_all
            for tap in range(1, CONV_TAPS):
                y = pltpu.roll(y, 1, axis=0) + convw_ref[tap:tap + 1, cols] * u_all
            y = y[HIST_ROW:, :]
            y = y * _sigmoid(y)
            kind, h = divmod(blk, N_HEADS)
            if kind == 0:
                qn_sc[hd + h] = (y * lax.rsqrt(jnp.sum(y * y, axis=-1, keepdims=True) + EPS) * scale).astype(BF16)
            elif kind == 1:
                kf_sc[hd + h] = y * lax.rsqrt(jnp.sum(y * y, axis=-1, keepdims=True) + EPS)
            else:
                v_sc[hd + h] = y
        t_valid = T if valid == L else valid
        u_sc[s, HIST_ROW - 3:HIST_ROW, :] = u_sc[s, HIST_ROW + t_valid - 3:HIST_ROW + t_valid, :]
        yield

        def a_k(c, h):
            return proj_ref[s, rows(c), hcols(h, 1)]

        qk_a = {(c, h): lax.dot_general(proj_ref[s, rows(c), hcols(h)], two(a_k(c, h)), _NT,
                                        preferred_element_type=F32) for c, h in ch}
        qk_b = {}
        for c, h in ch:
            kn = kf_sc[hd + h, rows(c), :].astype(BF16)
            qk_b[c, h] = lax.dot_general(jnp.concatenate([qn_sc[hd + h, rows(c), :], kn], axis=0), two(kn), _NT,
                                         preferred_element_type=F32)
        yield

        d_a, m_a, s_a, kw_a = {}, {}, {}, {}
        for c, h in ch:
            ig_row, b_row = zt[c][h:h + 1, :], zt[c][4 + h:5 + h, :]
            d_a[c, h] = jnp.where(tri_incl, zcol(c, 4 + h) - b_row + ig_row, NEG_BIG)
        for c, h in ch:
            m_a[c, h] = jnp.max(d_a[c, h], axis=1, keepdims=True)
            st_sc[sa, rows(c), 4 + h:5 + h] = m_a[c, h]
        yield
        for c, h in ch:
            s_a[c, h] = jnp.where(left, qk_a[c, h] * scale * jnp.exp(d_a[c, h] - m_a[c, h]), 0.0)
        for c, h in ch:
            ig_b, b_b = zcol(c, h), zcol(c, 4 + h)
            w_s = jnp.exp(b_b[L - 1:L, :] - b_b + ig_b - m_a[c, h][L - 1:L, :])
            kw = a_k(c, h).astype(F32) * scale * w_s
            dn_sc[sa, idx[c, h]:idx[c, h] + 1, :] = jnp.sum(kw, axis=0, keepdims=True)
            kw_a[c, h] = kw.astype(BF16)
        yield
        for c, h in ch:
            v = proj_ref[s, rows(c), hcols(h, 2)]
            sv1 = _dot(s_a[c, h].astype(BF16), jnp.concatenate([two(v), ones_sq], axis=1))
            sv_sc[sa * n_ch + idx[c, h]] = sv1[:, :HEAD_DIM]
            st_sc[sa, rows(c), h:h + 1] = sv1[:, HEAD_DIM:HEAD_DIM + 1]
            dc_sc[sa * n_ch + idx[c, h]] = lax.dot_general(kw_a[c, h], v, _TN, preferred_element_type=F32)
        yield

        decay, r_b = {}, {}
        for c, h in ch:
            g_row = zt[c][8 + h:9 + h, :]
            decay[c, h] = jnp.exp(jnp.where(tri_incl, zcol(c, 8 + h) - g_row, NEG_BIG))
            a2 = jnp.where(tri_strict, zcol(c, 12 + h) * decay[c, h] * qk_b[c, h][L:, :], 0.0)
            r_b[c, h] = jnp.where(left, eye_left, -a2)
        yield
        for _ in range(int(math.log2(L))):
            for p in ch:
                r = r_b[p]
                out = _dot(jnp.where(left, 0.0, r).astype(BF16), two(r.astype(BF16)))
                r_b[p] = jnp.where(left, r + out, out)
            yield
        uk = {}
        for c, h in ch:
            beta = zcol(c, 12 + h)
            rhs = jnp.concatenate([beta * v_sc[hd + h, rows(c), :],
                                   beta * jnp.exp(zcol(c, 8 + h)) * kf_sc[hd + h, rows(c), :]], axis=1).astype(BF16)
            uk[c, h] = _dot(jnp.where(left, r_b[c, h], 0.0).astype(BF16), two(rhs)).astype(BF16)
        yield
        for c, h in ch:
            g_col = zcol(c, 8 + h)
            g_last = g_col[L - 1:L, :]
            qkd = jnp.where(left, qk_b[c, h][:L, :] * decay[c, h], 0.0).astype(BF16)
            qu = _dot(qkd, two(uk[c, h]))
            kd = (kf_sc[hd + h, rows(c), :] * jnp.exp(g_last - g_col)).astype(BF16)
            ku = lax.dot_general(kd, uk[c, h], _TN, preferred_element_type=F32)
            q_eff = jnp.exp(g_col) * qn_sc[hd + h, rows(c), :].astype(F32) - qu[:, HEAD_DIM:]
            i = sa * n_ch + idx[c, h]
            qu_sc[i] = qu[:, :HEAD_DIM]
            nc_sc[i] = ku[:, :HEAD_DIM]
            lb_sc[i] = jnp.concatenate([ku[:, HEAD_DIM:], q_eff], axis=0).astype(BF16)
        yield

    def phase_b(s):
        sb = s * slots + slot_b
        hd = s * N_HEADS
        for c in chunks:
            zc = z_sc[sb, rows(c), :]
            stc = st_sc[sb, rows(c), :]
            base_i = sb * n_ch + c * N_HEADS
            c_prev = [c_sc[hd + h] for h in heads]
            s_prev = [s_sc[hd + h] for h in heads]
            n_prev = [n_sc[hd + h, 0:1, :] for h in heads]
            qc = [_dot(q_sc[sb, rows(c), hcols(h)], c_prev[h].astype(BF16)) for h in heads]
            rm = [_dot(lb_sc[base_i + h], s_prev[h].astype(BF16)) for h in heads]
            m_prev = m_sc[s, 0:1, :]
            inter = zc + m_prev
            m = jnp.maximum(inter, stc)
            m_last = jnp.where(head_lanes, m[L - 1:L, :], 0.0)
            decay0_t = jnp.exp(zc[L - 1:L, :] + m_prev - m_last)
            f_new_t = jnp.exp(stc[L - 1:L, :] - m_last)
            m_sc[s, 0:1, :] = m_last
            for h in heads:
                hl = slice(N_HEADS + h, N_HEADS + h + 1)
                c_sc[hd + h] = decay0_t[:, hl] * c_prev[h] + f_new_t[:, hl] * dc_sc[base_i + h]
                n_sc[hd + h, 0:1, :] = (decay0_t[:, hl] * n_prev[h]
                                        + f_new_t[:, hl] * dn_sc[sb, c * N_HEADS + h:c * N_HEADS + h + 1, :])
            for h in heads:
                s_sc[hd + h] = (jnp.exp(zc[L - 1:L, 8 + h:9 + h]) * s_prev[h] + nc_sc[base_i + h]
                                - rm[h][:HEAD_DIM, :])
            yield
            w_inter_t = jnp.exp(inter - m)
            w_intra_t = jnp.exp(stc - m)
            inv_cap_t = jnp.exp(-m)
            qn = [jnp.sum(q_sc[sb, rows(c), hcols(h)].astype(F32) * n_prev[h], axis=-1, keepdims=True)
                  for h in heads]
            num, cap, o_b = [], [], []
            for h in heads:
                hl = slice(N_HEADS + h, N_HEADS + h + 1)
                w_inter, w_intra = w_inter_t[:, hl], w_intra_t[:, hl]
                num.append(w_inter * qc[h] + w_intra * sv_sc[base_i + h])
                den = w_inter * qn[h] + w_intra * stc[:, h:h + 1]
                cap.append(jnp.maximum(jnp.abs(den), inv_cap_t[:, hl]))
                o_b.append(rm[h][HEAD_DIM:, :] + qu_sc[base_i + h])
            ms_a = [jnp.mean(x * x, axis=-1, keepdims=True) for x in num]
            ms_b = [jnp.mean(x * x, axis=-1, keepdims=True) for x in o_b]
            for h in heads:
                hn = num[h] * lax.rsqrt(ms_a[h] + EPS * cap[h] * cap[h]) * anw_ref[...]
                h_ref[s, rows(c), hcols(h)] = (og_sc[sb, rows(c), hcols(h)].astype(F32) * hn).astype(BF16)
            for h in heads:
                on = o_b[h] * lax.rsqrt(ms_b[h] + EPS) * bnw_ref[...]
                h_ref[s, rows(c), hcols(h, 1)] = (on * zg_sc[sb, rows(c), hcols(h)].astype(F32)).astype(BF16)
            yield

    steps_a, steps_b = 9 + int(math.log2(L)), 2 * n_chunks
    gens_a = [phase_a(s) for s in seqs]
    gens_b = [phase_b(s) for s in seqs]

    def advance(gens):
        for g in gens:
            next(g)

    if pipelined:
        done_b = 0
        for k in range(steps_a):
            while done_b < steps_b and done_b * steps_a <= k * steps_b:
                advance(gens_b)
                done_b += 1
            advance(gens_a)
        for _ in range(steps_b - done_b):
            advance(gens_b)
    else:
        for _ in range(steps_a):
            advance(gens_a)
        for _ in range(steps_b):
            advance(gens_b)
    assert all(next(g, None) is None for g in gens_a + gens_b)

    def store_state():
        c1_ref[...] = c_sc[...].reshape(c1_ref.shape)
        s1_ref[...] = s_sc[...].reshape(s1_ref.shape)
        for s in seqs:
            for h in heads:
                n1_ref[s, h:h + 1, :] = n_sc[s * N_HEADS + h, 0:1, :]
            m1_ref[s] = m_sc[s, 0:1, N_HEADS:2 * N_HEADS]

    def store_conv_history():
        conv1_ref[...] = u_sc[:, HIST_ROW - 3:HIST_ROW, :]

    if pipelined:
        pl.when(has_b & (j_b == n_tiles - 1))(store_state)
        pl.when(j_a == n_tiles - 1)(store_conv_history)
    else:
        store_state()
        store_conv_history()


def _spread_matrix():
    n = lax.broadcasted_iota(jnp.int32, (4 * N_HEADS * GATE_LANES, 2 * GATE_LANES), 0)
    k = lax.broadcasted_iota(jnp.int32, (4 * N_HEADS * GATE_LANES, 2 * GATE_LANES), 1)
    return ((k % GATE_LANES) == (n // GATE_LANES)).astype(BF16)


def _mixers(proj, gates, c0, n0, m0, s0, conv0, gate_par, conv_w, a_norm_w, b_norm_w, *, valid, tile, n_seq):
    bsz, t, n = proj.shape
    tile = min(tile, t)
    assert t % tile == 0 and tile % CHUNK == 0 and bsz % n_seq == 0
    assert valid == CHUNK or (t == CHUNK and CONV_TAPS - 1 <= valid < CHUNK)
    n_chunks = tile // CHUNK
    n_tiles = t // tile
    pipelined = n_tiles > 1
    slots = 2 if pipelined else 1
    n_ch = n_chunks * N_HEADS
    cw = 3 * GROUP_W
    n_groups = bsz // n_seq
    total = n_groups * n_tiles
    if pipelined:
        pair_a = lambda s: jnp.minimum(s, total - 1)
        pair_b = lambda s: jnp.maximum(s - 1, 0)
    else:
        pair_a = pair_b = lambda s: s
    tile_a = lambda s: (pair_a(s) // n_tiles, pair_a(s) % n_tiles)
    tile_b = lambda s: (pair_b(s) // n_tiles, pair_b(s) % n_tiles)
    st4 = lambda s: (tile_b(s)[0], 0, 0, 0)
    st3 = lambda s: (tile_b(s)[0], 0, 0)
    cst2 = lambda s: (0, 0)
    cst3 = lambda s: (0, 0, 0)
    conv_spec = pl.BlockSpec((n_seq, CONV_TAPS - 1, cw), lambda s: (tile_a(s)[0], 0, 0))
    state_specs = [
        pl.BlockSpec((n_seq, N_HEADS, HEAD_DIM, HEAD_DIM), st4),
        pl.BlockSpec((n_seq, N_HEADS, HEAD_DIM), st3),
        pl.BlockSpec((n_seq, 1, N_HEADS), st3),
        pl.BlockSpec((n_seq, N_HEADS, HEAD_DIM, HEAD_DIM), st4),
        conv_spec,
    ]
    state_shapes = [
        jax.ShapeDtypeStruct((bsz, N_HEADS, HEAD_DIM, HEAD_DIM), F32),
        jax.ShapeDtypeStruct((bsz, N_HEADS, HEAD_DIM), F32),
        jax.ShapeDtypeStruct((bsz, 1, N_HEADS), F32),
        jax.ShapeDtypeStruct((bsz, N_HEADS, HEAD_DIM, HEAD_DIM), F32),
        jax.ShapeDtypeStruct((bsz, CONV_TAPS - 1, cw), F32),
    ]
    ns, nh = n_seq * slots, n_seq * N_HEADS
    return pl.pallas_call(
        functools.partial(_mixer_body, n_seq=n_seq, n_chunks=n_chunks, n_tiles=n_tiles, valid=valid,
                          pipelined=pipelined),
        grid=(total + (1 if pipelined else 0),),
        in_specs=[
            pl.BlockSpec((n_seq, tile, n), lambda s: (*tile_a(s), 0)),
            pl.BlockSpec((n_seq, 4 * N_HEADS, tile), lambda s: (tile_a(s)[0], 0, tile_a(s)[1])),
            *state_specs,
            pl.BlockSpec((2, 4 * N_HEADS, tile), cst3),
            pl.BlockSpec((CONV_TAPS, cw), cst2),
            pl.BlockSpec((1, HEAD_DIM), cst2),
            pl.BlockSpec((1, HEAD_DIM), cst2),
            pl.BlockSpec((4 * N_HEADS * GATE_LANES, 2 * GATE_LANES), cst2),
        ],
        out_specs=[pl.BlockSpec((n_seq, tile, 2 * GROUP_W), lambda s: (*tile_b(s), 0)), *state_specs],
        out_shape=[jax.ShapeDtypeStruct((bsz, t, 2 * GROUP_W), BF16), *state_shapes],
        scratch_shapes=[
            pltpu.VMEM((nh, HEAD_DIM, HEAD_DIM), F32),
            pltpu.VMEM((nh, 8, HEAD_DIM), F32),
            pltpu.VMEM((n_seq, 8, GATE_LANES), F32),
            pltpu.VMEM((nh, HEAD_DIM, HEAD_DIM), F32),
            pltpu.VMEM((n_seq, HIST_ROW + tile, cw), F32),
            pltpu.VMEM((nh, tile, HEAD_DIM), BF16),
            pltpu.VMEM((nh, tile, HEAD_DIM), F32),
            pltpu.VMEM((nh, tile, HEAD_DIM), F32),
            pltpu.VMEM((ns, tile, GATE_LANES), F32),
            pltpu.VMEM((ns, tile, GATE_LANES), F32),
            pltpu.VMEM((ns, n_ch, HEAD_DIM), F32),
            pltpu.VMEM((ns, tile, GROUP_W), BF16),
            pltpu.VMEM((ns, tile, GROUP_W), BF16),
            pltpu.VMEM((ns, tile, GROUP_W), BF16),
            pltpu.VMEM((ns * n_ch, HEAD_DIM, HEAD_DIM), F32),
            pltpu.VMEM((ns * n_ch, CHUNK, HEAD_DIM), F32),
            pltpu.VMEM((ns * n_ch, HEAD_DIM, HEAD_DIM), F32),
            pltpu.VMEM((ns * n_ch, HEAD_DIM + CHUNK, HEAD_DIM), BF16),
            pltpu.VMEM((ns * n_ch, CHUNK, HEAD_DIM), F32),
        ],
        compiler_params=_params(("arbitrary",)),
        name="mixers",
    )(proj, gates, c0, n0, m0, s0, conv0, gate_par, conv_w, a_norm_w, b_norm_w, _spread_matrix())


def _mem_kv_body(mem_ref, nw_ref, wkv_ref, k_ref, v_ref, kb_ref, vb_ref):
    nb, n_mem, d = mem_ref.shape
    dh = d // N_XHEADS
    mn = _rms(mem_ref[...].reshape(nb * n_mem, d), nw_ref[...]).astype(BF16)
    for w0, f_ref, b_ref in ((0, k_ref, kb_ref), (d, v_ref, vb_ref)):
        y = _dot(mn, wkv_ref[:, w0:w0 + d])
        b_ref[...] = y.astype(BF16).reshape(nb, n_mem, d)
        for h in range(N_XHEADS):
            f_ref[:, :, h, :] = y[:, h * dh:(h + 1) * dh].reshape(nb, n_mem, dh)


def _mem_kv(mem, norm_w, wkv):
    bsz, n_mem, d = mem.shape
    nb = max(1, min(bsz, ROW_TILE // n_mem))
    assert bsz % nb == 0
    b3 = lambda i: (i, 0, 0)
    b4 = lambda i: (i, 0, 0, 0)
    cst = lambda i: (0, 0)
    dh = d // N_XHEADS
    return pl.pallas_call(
        _mem_kv_body,
        grid=(bsz // nb,),
        in_specs=[pl.BlockSpec((nb, n_mem, d), b3), pl.BlockSpec((1, d), cst), pl.BlockSpec((d, 2 * d), cst)],
        out_specs=[pl.BlockSpec((nb, n_mem, N_XHEADS, dh), b4)] * 2 + [pl.BlockSpec((nb, n_mem, d), b3)] * 2,
        out_shape=([jax.ShapeDtypeStruct((bsz, n_mem, N_XHEADS, dh), F32)] * 2
                   + [jax.ShapeDtypeStruct((bsz, n_mem, d), BF16)] * 2),
        compiler_params=_params(("arbitrary",)),
        name="mem_kv",
    )(mem, norm_w, wkv)


def _attn_body(h_ref, x_ref, wout_ref, nx_ref, wq_ref, mk_ref, mv_ref, wo_ref, o_ref, *, n_sub):
    nb, tm, d = x_ref.shape
    dh = d // N_XHEADS
    rows = nb * tm
    gr = rows // n_sub
    assert gr % tm == 0 or tm % gr == 0
    per = max(1, gr // tm)
    hcols = [slice(h * dh, (h + 1) * dh) for h in range(N_XHEADS)]

    def heads_of(ref, b):
        if len(ref.shape) == 4:
            return [ref[b, :, h, :].astype(BF16) for h in range(N_XHEADS)]
        return [ref[b, :, c] for c in hcols]

    def group_rows(ref, g):
        if per > 1 or gr == tm:
            return ref[g * per:(g + 1) * per].reshape(gr, d)
        return ref[(g * gr) // tm, (g * gr) % tm:(g * gr) % tm + gr, :]

    groups = range(n_sub)
    x1 = [group_rows(x_ref, g) + _dot(group_rows(h_ref, g), wout_ref[...]) for g in groups]
    xn = [_rms(x, nx_ref[...]).astype(BF16) for x in x1]
    q = [_dot(x, wq_ref[...]).astype(BF16) for x in xn]
    unit_rows = min(gr, tm)
    units = [(g, slice(u * unit_rows, (u + 1) * unit_rows), (g * gr + u * unit_rows) // tm)
             for g in groups for u in range(gr // unit_rows)]
    kv = {b: (heads_of(mk_ref, b), heads_of(mv_ref, b)) for b in sorted({b for _, _, b in units})}
    s = [[lax.dot_general(q[g][r, c], kh, _NT, preferred_element_type=F32) * (dh ** -0.5)
          for c, kh in zip(hcols, kv[b][0])] for g, r, b in units]
    mx = [[jnp.max(sh, axis=-1, keepdims=True) for sh in si] for si in s]
    e = [[jnp.exp(sh - mh) for sh, mh in zip(si, mi)] for si, mi in zip(s, mx)]
    den = [[jnp.sum(eh, axis=-1, keepdims=True) for eh in ei] for ei in e]
    p = [[(eh / dh_).astype(BF16) for eh, dh_ in zip(ei, di)] for ei, di in zip(e, den)]
    o = [[_dot(ph, vh).astype(BF16) for ph, vh in zip(pi, kv[b][1])] for pi, (_, _, b) in zip(p, units)]
    for g in groups:
        mine = [oi for oi, (ug, _, _) in zip(o, units) if ug == g]
        acc = x1[g]
        for h, c in enumerate(hcols):
            oh = mine[0][h] if len(mine) == 1 else jnp.concatenate([oi[h] for oi in mine], axis=0)
            acc = acc + _dot(oh, wo_ref[c, :])
        if per > 1 or gr == tm:
            o_ref[g * per:(g + 1) * per] = acc.reshape(per, tm, d)
        else:
            o_ref[(g * gr) // tm, (g * gr) % tm:(g * gr) % tm + gr, :] = acc


def _out_proj_attn(hcat, x, w_out, norm_w, wq, mk, mv, wo):
    bsz, t, d = x.shape
    tm = min(ROW_TILE, t)
    assert t % tm == 0
    nb = max(1, min(bsz, ATTN_MIN_ROWS // tm))
    assert bsz % nb == 0
    bj = lambda b, j: (b, j, 0)
    cst = lambda b, j: (0, 0)
    mem_spec = pl.BlockSpec((nb,) + mk.shape[1:], lambda b, j: (b,) + (0,) * (mk.ndim - 1))
    return pl.pallas_call(
        functools.partial(_attn_body, n_sub=2 if tm >= 256 else 1),
        grid=(bsz // nb, t // tm),
        in_specs=[
            pl.BlockSpec((nb, tm, d), bj),
            pl.BlockSpec((nb, tm, d), bj),
            pl.BlockSpec((d, d), cst),
            pl.BlockSpec((1, d), cst),
            pl.BlockSpec((d, d), cst),
            mem_spec,
            mem_spec,
            pl.BlockSpec((d, d), cst),
        ],
        out_specs=pl.BlockSpec((nb, tm, d), bj),
        out_shape=jax.ShapeDtypeStruct((bsz, t, d), F32),
        compiler_params=_params(("arbitrary", "arbitrary")),
        name="out_proj_attn",
    )(hcat, x, w_out, norm_w, wq, mk, mv, wo)


def _ffn_body(x_ref, nf_ref, w1_ref, w2_ref, nfin_ref, o_ref, *, hid_tile, final_norm):
    x = x_ref[...]
    xn = _rms(x, nf_ref[...]).astype(BF16)
    acc = x
    for c0 in range(0, w1_ref.shape[1], hid_tile):
        hdn = jnp.maximum(_dot(xn, w1_ref[:, c0:c0 + hid_tile]), 0.0)
        acc = acc + _dot((hdn * hdn).astype(BF16), w2_ref[c0:c0 + hid_tile, :])
    o_ref[...] = _rms(acc, nfin_ref[...]) if final_norm else acc


def _ffn(x2d, norm_w, w1, w2, final_w, *, final_norm):
    rows, d = x2d.shape
    dff = w1.shape[1]
    tm = min(ROW_TILE, rows)
    assert rows % tm == 0
    row = lambda i: (i, 0)
    cst = lambda i: (0, 0)
    return pl.pallas_call(
        functools.partial(_ffn_body, hid_tile=1024, final_norm=final_norm),
        grid=(rows // tm,),
        in_specs=[
            pl.BlockSpec((tm, d), row),
            pl.BlockSpec((1, d), cst),
            pl.BlockSpec((d, dff), cst, pipeline_mode=pl.Buffered(1)),
            pl.BlockSpec((dff, d), cst, pipeline_mode=pl.Buffered(1)),
            pl.BlockSpec((1, d), cst),
        ],
        out_specs=pl.BlockSpec((tm, d), row),
        out_shape=jax.ShapeDtypeStruct((rows, d), F32),
        compiler_params=_params(("arbitrary",)),
        name="ffn",
    )(x2d, norm_w, w1, w2, final_w)


def _split_w_in(w_in):
    w = GROUP_W
    g0 = 4 * w
    b0 = g0 + 2 * N_HEADS
    g1 = b0 + 4 * w
    main = jnp.concatenate([w_in[:, :g0], w_in[:, b0:g1]], axis=1).astype(BF16)
    gate = jnp.concatenate([w_in[:, g0:b0], w_in[:, g1:g1 + 2 * N_HEADS]], axis=1)
    return main, gate.T.astype(BF16)


def _gate_params(igate_b, fgate_b, a_log, dt_bias, width):
    zeros = jnp.zeros((N_HEADS,), F32)
    bias = jnp.concatenate([igate_b, fgate_b, dt_bias, zeros]).astype(F32)
    alog = jnp.concatenate([zeros, zeros, a_log.astype(F32), zeros])
    return jnp.broadcast_to(jnp.stack([bias, alog])[:, :, None], (2, 4 * N_HEADS, width))


def _trunk_layer(x, mem_k, mem_v, c0, n0, m0, s0, conv0, *, valid, w_main, w_gate, gate_vecs, lw, final_w,
                 final_norm):
    bsz, t, d = x.shape
    tp = CHUNK if valid < CHUNK else t
    xp = x if tp == t else jnp.pad(x, ((0, 0), (0, tp - t), (0, 0)))
    proj, gates = _in_proj(xp.reshape(bsz * tp, d), lw["norm_mix_w"], w_main, w_gate, tp)
    hcat, c1, n1, m1, s1, conv1 = _mixers(
        proj.reshape(bsz, tp, -1), gates, c0, n0, m0.reshape(bsz, 1, N_HEADS), s0, conv0,
        _gate_params(*gate_vecs, width=min(MIXER_TILE, tp)), lw["gdn_conv_w"], lw["mlstm_norm_w"],
        lw["gdn_norm_w"], valid=valid, tile=MIXER_TILE, n_seq=MIXER_SEQS if tp <= MIXER_TILE else 1)
    x2 = _out_proj_attn(hcat, x, lw["w_out"], lw["norm_x_w"], lw["wq_x"], mem_k, mem_v, lw["wo_x"])
    y = _ffn(x2.reshape(bsz * t, d), lw["norm_ffn_w"], lw["w_ff1"], lw["w_ff2"], final_w,
             final_norm=final_norm)
    return y.reshape(bsz, t, d), c1, n1, m1.reshape(bsz, N_HEADS), s1, conv1


def kernel(x_prompt, x_sample, state_mlstm_C, state_mlstm_n, state_mlstm_m, state_gdn_S, state_gdn_conv, cache_mem_k, cache_mem_v, mem_prompt, norm_mix_w, w_in, mlstm_igate_b, mlstm_fgate_b, mlstm_norm_w, gdn_conv_w, gdn_A_log, gdn_dt_bias, gdn_norm_w, w_out, norm_x_w, norm_mem_w, wq_x, wk_x, wv_x, wo_x, norm_ffn_w, w_ff1, w_ff2, norm_final_w):
    depth = w_in.shape[0]
    bp = x_prompt.shape[0]
    ts = x_sample.shape[1]
    row = lambda a: a.reshape(1, -1).astype(F32)
    hp, hs = x_prompt, x_sample
    outs_p = [[] for _ in range(7)]
    outs_s = [[] for _ in range(5)]
    for l in range(depth):
        w_main, w_gate = _split_w_in(w_in[l])
        gate_vecs = (mlstm_igate_b[l], mlstm_fgate_b[l], gdn_A_log[l], gdn_dt_bias[l])
        lw = dict(
            norm_mix_w=row(norm_mix_w[l]), gdn_conv_w=gdn_conv_w[l].astype(F32),
            mlstm_norm_w=row(mlstm_norm_w[l]), gdn_norm_w=row(gdn_norm_w[l]),
            w_out=w_out[l].astype(BF16), norm_x_w=row(norm_x_w[l]), wq_x=wq_x[l].astype(BF16),
            wo_x=wo_x[l].astype(BF16), norm_ffn_w=row(norm_ffn_w[l]),
            w_ff1=w_ff1[l].astype(BF16), w_ff2=w_ff2[l].astype(BF16))
        common = dict(w_main=w_main, w_gate=w_gate, gate_vecs=gate_vecs, lw=lw, final_w=row(norm_final_w),
                      final_norm=(l == depth - 1))
        wkv = jnp.concatenate([wk_x[l], wv_x[l]], axis=1).astype(BF16)
        mk, mv, mk_b, mv_b = _mem_kv(mem_prompt, row(norm_mem_w[l]), wkv)
        hp, c1, n1, m1, s1, cv1 = _trunk_layer(
            hp, mk_b, mv_b,
            jnp.zeros((bp, N_HEADS, HEAD_DIM, HEAD_DIM), F32), jnp.zeros((bp, N_HEADS, HEAD_DIM), F32),
            jnp.zeros((bp, N_HEADS), F32), jnp.zeros((bp, N_HEADS, HEAD_DIM, HEAD_DIM), F32),
            jnp.zeros((bp, CONV_TAPS - 1, 3 * GROUP_W), F32), valid=CHUNK, **common)
        for acc, val in zip(outs_p, (c1, n1, m1, s1, cv1, mk, mv)):
            acc.append(val)
        hs, c2, n2, m2, s2, cv2 = _trunk_layer(
            hs, cache_mem_k[l], cache_mem_v[l],
            state_mlstm_C[l], state_mlstm_n[l], state_mlstm_m[l], state_gdn_S[l], state_gdn_conv[l],
            valid=ts, **common)
        for acc, val in zip(outs_s, (c2, n2, m2, s2, cv2)):
            acc.append(val)
    return (hp, hs, *[jnp.stack(a) for a in outs_p], *[jnp.stack(a) for a in outs_s])
```

```python
import functools
import math

import jax
import jax.numpy as jnp
from jax import lax
from jax.experimental import pallas as pl
from jax.experimental.pallas import tpu as pltpu

F32 = jnp.float32
BF16 = jnp.bfloat16
EPS = 1e-6
N_HEADS = 4
HEAD_DIM = 128
GROUP_W = N_HEADS * HEAD_DIM
CONV_TAPS = 4
N_XHEADS = 4
CHUNK = 64
GATE_LANES = 128
NEG_BIG = -1e30
HIST_ROW = 8

VMEM_LIMIT_BYTES = 56 * 1024 * 1024
ROW_TILE = 1024
MIXER_TILE = 512
MIXER_SEQS = 2
ATTN_MIN_ROWS = 64

_NT = (((1,), (1,)), ((), ()))
_TN = (((0,), (0,)), ((), ()))


def _rms(x, w):
    return x * lax.rsqrt(jnp.mean(x * x, axis=-1, keepdims=True) + EPS) * w


def _dot(a, b):
    return jnp.dot(a, b, preferred_element_type=F32)


def _sigmoid(x):
    return 1.0 / (1.0 + jnp.exp(-x))


def _params(sem):
    return pltpu.CompilerParams(dimension_semantics=sem, vmem_limit_bytes=VMEM_LIMIT_BYTES)


def _in_proj_body(x_ref, nw_ref, w_ref, wg_ref, proj_ref, gate_ref, *, col_tile):
    assert col_tile == GROUP_W
    xn = _rms(x_ref[...], nw_ref[...]).astype(BF16)
    for c0 in range(0, w_ref.shape[1], col_tile):
        y = _dot(xn, w_ref[:, c0:c0 + col_tile])
        if c0 == 3 * GROUP_W:
            y = _sigmoid(y)
        elif c0 == 7 * GROUP_W:
            y = y * _sigmoid(y)
        proj_ref[:, c0:c0 + col_tile] = y.astype(BF16)
    gate_ref[...] = lax.dot_general(wg_ref[...], xn, _NT, preferred_element_type=F32).reshape(gate_ref.shape)


def _in_proj(x2d, norm_w, w_main, w_gate, seq_len):
    rows, d = x2d.shape
    n = w_main.shape[1]
    n_g = w_gate.shape[0]
    tm = min(ROW_TILE, rows)
    assert rows % tm == 0 and rows % seq_len == 0
    bsz = rows // seq_len
    direct = seq_len % tm == 0
    if direct:
        gate_spec = pl.BlockSpec((1, n_g, tm), lambda i: (i // (seq_len // tm), 0, i % (seq_len // tm)))
        gate_shape = jax.ShapeDtypeStruct((bsz, n_g, seq_len), F32)
    else:
        gate_spec = pl.BlockSpec((n_g, tm), lambda i: (0, i))
        gate_shape = jax.ShapeDtypeStruct((n_g, rows), F32)
    proj, gates = pl.pallas_call(
        functools.partial(_in_proj_body, col_tile=512),
        grid=(rows // tm,),
        in_specs=[
            pl.BlockSpec((tm, d), lambda i: (i, 0)),
            pl.BlockSpec((1, d), lambda i: (0, 0)),
            pl.BlockSpec((d, n), lambda i: (0, 0)),
            pl.BlockSpec((n_g, d), lambda i: (0, 0)),
        ],
        out_specs=[pl.BlockSpec((tm, n), lambda i: (i, 0)), gate_spec],
        out_shape=[jax.ShapeDtypeStruct((rows, n), BF16), gate_shape],
        compiler_params=_params(("arbitrary",)),
        name="in_proj",
    )(x2d, norm_w, w_main, w_gate)
    if not direct:
        gates = gates.reshape(n_g, bsz, seq_len).transpose(1, 0, 2)
    return proj, gates


def _mixer_body(proj_ref, gate_ref, c0_ref, n0_ref, m0_ref, s0_ref, conv0_ref,
                gpar_ref, convw_ref, anw_ref, bnw_ref, spread_ref,
                h_ref, c1_ref, n1_ref, m1_ref, s1_ref, conv1_ref,
                c_sc, n_sc, m_sc, s_sc, u_sc, qn_sc, kf_sc, v_sc,
                z_sc, st_sc, dn_sc, q_sc, og_sc, zg_sc, dc_sc, sv_sc, nc_sc, lb_sc, qu_sc,
                *, n_seq, n_chunks, n_tiles, valid, pipelined):
    L = CHUNK
    T = n_chunks * L
    n_ch = n_chunks * N_HEADS
    slots = 2 if pipelined else 1
    step = pl.program_id(0)
    scale = HEAD_DIM ** -0.5
    heads = range(N_HEADS)
    chunks = range(n_chunks)
    seqs = range(n_seq)
    if pipelined:
        j_a = lax.rem(jnp.minimum(step, pl.num_programs(0) - 2), n_tiles)
        j_b = lax.rem(step + (n_tiles - 1), n_tiles)
        has_b = step >= 1
        slot_a = step & 1
        slot_b = 1 - slot_a
    else:
        j_a = j_b = slot_a = slot_b = 0
        has_b = True
    handoff = (z_sc, st_sc, dn_sc, q_sc, og_sc, zg_sc, dc_sc, sv_sc, nc_sc, lb_sc, qu_sc)

    def load_state():
        c_sc[...] = c0_ref[...].reshape(c_sc.shape)
        s_sc[...] = s0_ref[...].reshape(s_sc.shape)
        m_sc[...] = jnp.zeros(m_sc.shape, F32)
        for s in seqs:
            for h in heads:
                n_sc[s * N_HEADS + h, 0:1, :] = n0_ref[s, h:h + 1, :]
            m_sc[s, 0:1, N_HEADS:2 * N_HEADS] = m0_ref[s]

    def load_conv_history():
        u_sc[:, 0:HIST_ROW - (CONV_TAPS - 1), :] = jnp.zeros((n_seq, HIST_ROW - (CONV_TAPS - 1), u_sc.shape[2]), F32)
        u_sc[:, HIST_ROW - (CONV_TAPS - 1):HIST_ROW, :] = conv0_ref[...]

    if pipelined:
        @pl.when(step == 0)
        def _zero_scratch():
            for ref in handoff + (c_sc, n_sc, m_sc, s_sc):
                ref[...] = jnp.zeros(ref.shape, ref.dtype)

        pl.when(j_a == 0)(load_conv_history)
        pl.when(has_b & (j_b == 0))(load_state)
    else:
        load_conv_history()
        load_state()

    lane = lax.broadcasted_iota(jnp.int32, (L, 2 * L), 1)
    tok = lax.broadcasted_iota(jnp.int32, (L, 2 * L), 0)
    src = lane & (L - 1)
    left = lane < L
    tri_incl = tok >= src
    tri_strict = tok > src
    eye_left = jnp.where(left & (tok == src), 1.0, 0.0)
    lane_row = lax.broadcasted_iota(jnp.int32, (1, GATE_LANES), 1)
    head_lanes = (lane_row >= N_HEADS) & (lane_row < 2 * N_HEADS)
    base = 4 * GROUP_W
    ch = [(c, h) for c in chunks for h in heads]
    idx = {p: i for i, p in enumerate(ch)}

    def two(x):
        return jnp.concatenate([x, x], axis=0)

    def rows(c):
        return slice(c * L, (c + 1) * L)

    def hcols(h, group=0):
        return slice(group * GROUP_W + h * HEAD_DIM, group * GROUP_W + (h + 1) * HEAD_DIM)

    ones_sq = jnp.full((HEAD_DIM, HEAD_DIM), 1.0, BF16)

    def highest(a, b, dims=None):
        if dims is None:
            return jnp.dot(a, b, precision=lax.Precision.HIGHEST, preferred_element_type=F32)
        return lax.dot_general(a, b, dims, precision=lax.Precision.HIGHEST, preferred_element_type=F32)

    def phase_a(s):
        sa = s * slots + slot_a
        hd = s * N_HEADS
        n_g = 4 * N_HEADS
        row_g = lax.broadcasted_iota(jnp.int32, (n_g, T), 0)
        pre = gate_ref[s] + gpar_ref[0]
        e = jnp.exp(-jnp.abs(pre))
        t = jnp.log1p(e)
        act = jnp.where(row_g < 4, pre,
                        jnp.where(row_g < 8, -(jnp.maximum(-pre, 0.0) + t),
                                  jnp.where(row_g < 12, -jnp.exp(gpar_ref[1]) * (jnp.maximum(pre, 0.0) + t),
                                            jnp.where(pre >= 0.0, 1.0, e) / (1.0 + e))))
        if valid < L:
            tok_g = lax.broadcasted_iota(jnp.int32, (n_g, T), 1)
            act = jnp.where(tok_g >= valid, jnp.where(row_g < 4, NEG_BIG, 0.0), act)
        r_t = lax.broadcasted_iota(jnp.int32, (T, T), 0)
        c_t = lax.broadcasted_iota(jnp.int32, (T, T), 1)
        shift = int(math.log2(L))
        within = jnp.where((r_t <= c_t) & ((r_t >> shift) == (c_t >> shift)), 1.0, 0.0)
        z_g = jnp.where((row_g >= 4) & (row_g < 12), highest(act, within), act)
        pick = jnp.where(lax.broadcasted_iota(jnp.int32, (n_g, GATE_LANES), 0)
                         == lax.broadcasted_iota(jnp.int32, (n_g, GATE_LANES), 1), 1.0, 0.0)
        z_t = highest(z_g, pick, _TN)
        z_sc[sa] = z_t
        z = [z_t[rows(c), :] for c in chunks]

        def spread(x):
            hi = x.astype(BF16)
            lo = (x - hi.astype(F32)).astype(BF16)
            return lax.dot_general(jnp.concatenate([hi, lo], axis=1), spread_ref[...], _NT,
                                   preferred_element_type=F32)

        zb = [spread(z[c]) for c in chunks]

        def zcol(c, k):
            return zb[c][:, k * GATE_LANES:(k + 1) * GATE_LANES]
        r_d = lax.broadcasted_iota(jnp.int32, (T, 2 * L), 0)
        c_d = lax.broadcasted_iota(jnp.int32, (T, 2 * L), 1)
        zt = [highest(z_g, jnp.where(r_d == c * L + (c_d & (L - 1)), 1.0, 0.0)) for c in chunks]
        q_sc[sa] = proj_ref[s, :, 0:GROUP_W]
        og_sc[sa] = proj_ref[s, :, 3 * GROUP_W:4 * GROUP_W]
        zg_sc[sa] = proj_ref[s, :, base + 3 * GROUP_W:base + 4 * GROUP_W]
        yield

        u_sc[s, HIST_ROW:HIST_ROW + T, :] = proj_ref[s, :, base:base + 3 * GROUP_W].astype(F32)
        for blk in range(3 * N_HEADS):
            cols = slice(blk * HEAD_DIM, (blk + 1) * HEAD_DIM)
            u_all = u_sc[s, :, cols]
            y = convw_ref[0:1, cols] * u_all
            for tap in range(1, CONV_TAPS):
                y = pltpu.roll(y, 1, axis=0) + convw_ref[tap:tap + 1, cols] * u_all
            y = y[HIST_ROW:, :]
            y = y * _sigmoid(y)
            kind, h = divmod(blk, N_HEADS)
            if kind == 0:
                qn_sc[hd + h] = (y * lax.rsqrt(jnp.sum(y * y, axis=-1, keepdims=True) + EPS) * scale).astype(BF16)
            elif kind == 1:
                kf_sc[hd + h] = y * lax.rsqrt(jnp.sum(y * y, axis=-1, keepdims=True) + EPS)
            else:
                v_sc[hd + h] = y
        t_valid = T if valid == L else valid
        u_sc[s, HIST_ROW - 3:HIST_ROW, :] = u_sc[s, HIST_ROW + t_valid - 3:HIST_ROW + t_valid, :]
        yield

        def a_k(c, h):
            return proj_ref[s, rows(c), hcols(h, 1)]

        qk_a = {(c, h): lax.dot_general(proj_ref[s, rows(c), hcols(h)], two(a_k(c, h)), _NT,
                                        preferred_element_type=F32) for c, h in ch}
        qk_b = {}
        for c, h in ch:
            kn = kf_sc[hd + h, rows(c), :].astype(BF16)
            qk_b[c, h] = lax.dot_general(jnp.concatenate([qn_sc[hd + h, rows(c), :], kn], axis=0), two(kn), _NT,
                                         preferred_element_type=F32)
        yield

        d_a, m_a, s_a, kw_a = {}, {}, {}, {}
        for c, h in ch:
            ig_row, b_row = zt[c][h:h + 1, :], zt[c][4 + h:5 + h, :]
            d_a[c, h] = jnp.where(tri_incl, zcol(c, 4 + h) - b_row + ig_row, NEG_BIG)
        for c, h in ch:
            m_a[c, h] = jnp.max(d_a[c, h], axis=1, keepdims=True)
            st_sc[sa, rows(c), 4 + h:5 + h] = m_a[c, h]
        yield
        for c, h in ch:
            s_a[c, h] = jnp.where(left, qk_a[c, h] * scale * jnp.exp(d_a[c, h] - m_a[c, h]), 0.0)
        for c, h in ch:
            ig_b, b_b = zcol(c, h), zcol(c, 4 + h)
            w_s = jnp.exp(b_b[L - 1:L, :] - b_b + ig_b - m_a[c, h][L - 1:L, :])
            kw = a_k(c, h).astype(F32) * scale * w_s
            dn_sc[sa, idx[c, h]:idx[c, h] + 1, :] = jnp.sum(kw, axis=0, keepdims=True)
            kw_a[c, h] = kw.astype(BF16)
        yield
        for c, h in ch:
            v = proj_ref[s, rows(c), hcols(h, 2)]
            sv1 = _dot(s_a[c, h].astype(BF16), jnp.concatenate([two(v), ones_sq], axis=1))
            sv_sc[sa * n_ch + idx[c, h]] = sv1[:, :HEAD_DIM]
            st_sc[sa, rows(c), h:h + 1] = sv1[:, HEAD_DIM:HEAD_DIM + 1]
            dc_sc[sa * n_ch + idx[c, h]] = lax.dot_general(kw_a[c, h], v, _TN, preferred_element_type=F32)
        yield

        decay, r_b = {}, {}
        for c, h in ch:
            g_row = zt[c][8 + h:9 + h, :]
            decay[c, h] = jnp.exp(jnp.where(tri_incl, zcol(c, 8 + h) - g_row, NEG_BIG))
            a2 = jnp.where(tri_strict, zcol(c, 12 + h) * decay[c, h] * qk_b[c, h][L:, :], 0.0)
            r_b[c, h] = jnp.where(left, eye_left, -a2)
        yield
        for _ in range(int(math.log2(L))):
            for p in ch:
                r = r_b[p]
                out = _dot(jnp.where(left, 0.0, r).astype(BF16), two(r.astype(BF16)))
                r_b[p] = jnp.where(left, r + out, out)
            yield
        uk = {}
        for c, h in ch:
            beta = zcol(c, 12 + h)
            rhs = jnp.concatenate([beta * v_sc[hd + h, rows(c), :],
                                   beta * jnp.exp(zcol(c, 8 + h)) * kf_sc[hd + h, rows(c), :]], axis=1).astype(BF16)
            uk[c, h] = _dot(jnp.where(left, r_b[c, h], 0.0).astype(BF16), two(rhs)).astype(BF16)
        yield
        for c, h in ch:
            g_col = zcol(c, 8 + h)
            g_last = g_col[L - 1:L, :]
            qkd = jnp.where(left, qk_b[c, h][:L, :] * decay[c, h], 0.0).astype(BF16)
            qu = _dot(qkd, two(uk[c, h]))
            kd = (kf_sc[hd + h, rows(c), :] * jnp.exp(g_last - g_col)).astype(BF16)
            ku = lax.dot_general(kd, uk[c, h], _TN, preferred_element_type=F32)
            q_eff = jnp.exp(g_col) * qn_sc[hd + h, rows(c), :].astype(F32) - qu[:, HEAD_DIM:]
            i = sa * n_ch + idx[c, h]
            qu_sc[i] = qu[:, :HEAD_DIM]
            nc_sc[i] = ku[:, :HEAD_DIM]
            lb_sc[i] = jnp.concatenate([ku[:, HEAD_DIM:], q_eff], axis=0).astype(BF16)
        yield

    def phase_b(s):
        sb = s * slots + slot_b
        hd = s * N_HEADS
        for c in chunks:
            zc = z_sc[sb, rows(c), :]
            stc = st_sc[sb, rows(c), :]
            base_i = sb * n_ch + c * N_HEADS
            c_prev = [c_sc[hd + h] for h in heads]
            s_prev = [s_sc[hd + h] for h in heads]
            n_prev = [n_sc[hd + h, 0:1, :] for h in heads]
            qc = [_dot(q_sc[sb, rows(c), hcols(h)], c_prev[h].astype(BF16)) for h in heads]
            rm = [_dot(lb_sc[base_i + h], s_prev[h].astype(BF16)) for h in heads]
            m_prev = m_sc[s, 0:1, :]
            inter = zc + m_prev
            m = jnp.maximum(inter, stc)
            m_last = jnp.where(head_lanes, m[L - 1:L, :], 0.0)
            decay0_t = jnp.exp(zc[L - 1:L, :] + m_prev - m_last)
            f_new_t = jnp.exp(stc[L - 1:L, :] - m_last)
            m_sc[s, 0:1, :] = m_last
            for h in heads:
                hl = slice(N_HEADS + h, N_HEADS + h + 1)
                c_sc[hd + h] = decay0_t[:, hl] * c_prev[h] + f_new_t[:, hl] * dc_sc[base_i + h]
                n_sc[hd + h, 0:1, :] = (decay0_t[:, hl] * n_prev[h]
                                        + f_new_t[:, hl] * dn_sc[sb, c * N_HEADS + h:c * N_HEADS + h + 1, :])
            for h in heads:
                s_sc[hd + h] = (jnp.exp(zc[L - 1:L, 8 + h:9 + h]) * s_prev[h] + nc_sc[base_i + h]
                                - rm[h][:HEAD_DIM, :])
            yield
            w_inter_t = jnp.exp(inter - m)
            w_intra_t = jnp.exp(stc - m)
            inv_cap_t = jnp.exp(-m)
            qn = [jnp.sum(q_sc[sb, rows(c), hcols(h)].astype(F32) * n_prev[h], axis=-1, keepdims=True)
                  for h in heads]
            num, cap, o_b = [], [], []
            for h in heads:
                hl = slice(N_HEADS + h, N_HEADS + h + 1)
                w_inter, w_intra = w_inter_t[:, hl], w_intra_t[:, hl]
                num.append(w_inter * qc[h] + w_intra * sv_sc[base_i + h])
                den = w_inter * qn[h] + w_intra * stc[:, h:h + 1]
                cap.append(jnp.maximum(jnp.abs(den), inv_cap_t[:, hl]))
                o_b.append(rm[h][HEAD_DIM:, :] + qu_sc[base_i + h])
            ms_a = [jnp.mean(x * x, axis=-1, keepdims=True) for x in num]
            ms_b = [jnp.mean(x * x, axis=-1, keepdims=True) for x in o_b]
            for h in heads:
                hn = num[h] * lax.rsqrt(ms_a[h] + EPS * cap[h] * cap[h]) * anw_ref[...]
                h_ref[s, rows(c), hcols(h)] = (og_sc[sb, rows(c), hcols(h)].astype(F32) * hn).astype(BF16)
            for h in heads:
                on = o_b[h] * lax.rsqrt(ms_b[h] + EPS) * bnw_ref[...]
                h_ref[s, rows(c), hcols(h, 1)] = (on * zg_sc[sb, rows(c), hcols(h)].astype(F32)).astype(BF16)
            yield

    steps_a, steps_b = 9 + int(math.log2(L)), 2 * n_chunks
    gens_a = [phase_a(s) for s in seqs]
    gens_b = [phase_b(s) for s in seqs]

    def advance(gens):
        for g in gens:
            next(g)

    if pipelined:
        done_b = 0
        for k in range(steps_a):
            while done_b < steps_b and done_b * steps_a <= k * steps_b:
                advance(gens_b)
                done_b += 1
            advance(gens_a)
        for _ in range(steps_b - done_b):
            advance(gens_b)
    else:
        for _ in range(steps_a):
            advance(gens_a)
        for _ in range(steps_b):
            advance(gens_b)
    assert all(next(g, None) is None for g in gens_a + gens_b)

    def store_state():
        c1_ref[...] = c_sc[...].reshape(c1_ref.shape)
        s1_ref[...] = s_sc[...].reshape(s1_ref.shape)
        for s in seqs:
            for h in heads:
                n1_ref[s, h:h + 1, :] = n_sc[s * N_HEADS + h, 0:1, :]
            m1_ref[s] = m_sc[s, 0:1, N_HEADS:2 * N_HEADS]

    def store_conv_history():
        conv1_ref[...] = u_sc[:, HIST_ROW - 3:HIST_ROW, :]

    if pipelined:
        pl.when(has_b & (j_b == n_tiles - 1))(store_state)
        pl.when(j_a == n_tiles - 1)(store_conv_history)
    else:
        store_state()
        store_conv_history()


def _spread_matrix():
    n = lax.broadcasted_iota(jnp.int32, (4 * N_HEADS * GATE_LANES, 2 * GATE_LANES), 0)
    k = lax.broadcasted_iota(jnp.int32, (4 * N_HEADS * GATE_LANES, 2 * GATE_LANES), 1)
    return ((k % GATE_LANES) == (n // GATE_LANES)).astype(BF16)


def _mixers(proj, gates, c0, n0, m0, s0, conv0, gate_par, conv_w, a_norm_w, b_norm_w, *, valid, tile, n_seq):
    bsz, t, n = proj.shape
    tile = min(tile, t)
    assert t % tile == 0 and tile % CHUNK == 0 and bsz % n_seq == 0
    assert valid == CHUNK or (t == CHUNK and CONV_TAPS - 1 <= valid < CHUNK)
    n_chunks = tile // CHUNK
    n_tiles = t // tile
    pipelined = n_tiles > 1
    slots = 2 if pipelined else 1
    n_ch = n_chunks * N_HEADS
    cw = 3 * GROUP_W
    n_groups = bsz // n_seq
    total = n_groups * n_tiles
    if pipelined:
        pair_a = lambda s: jnp.minimum(s, total - 1)
        pair_b = lambda s: jnp.maximum(s - 1, 0)
    else:
        pair_a = pair_b = lambda s: s
    tile_a = lambda s: (pair_a(s) // n_tiles, pair_a(s) % n_tiles)
    tile_b = lambda s: (pair_b(s) // n_tiles, pair_b(s) % n_tiles)
    st4 = lambda s: (tile_b(s)[0], 0, 0, 0)
    st3 = lambda s: (tile_b(s)[0], 0, 0)
    cst2 = lambda s: (0, 0)
    cst3 = lambda s: (0, 0, 0)
    conv_spec = pl.BlockSpec((n_seq, CONV_TAPS - 1, cw), lambda s: (tile_a(s)[0], 0, 0))
    state_specs = [
        pl.BlockSpec((n_seq, N_HEADS, HEAD_DIM, HEAD_DIM), st4),
        pl.BlockSpec((n_seq, N_HEADS, HEAD_DIM), st3),
        pl.BlockSpec((n_seq, 1, N_HEADS), st3),
        pl.BlockSpec((n_seq, N_HEADS, HEAD_DIM, HEAD_DIM), st4),
        conv_spec,
    ]
    state_shapes = [
        jax.ShapeDtypeStruct((bsz, N_HEADS, HEAD_DIM, HEAD_DIM), F32),
        jax.ShapeDtypeStruct((bsz, N_HEADS, HEAD_DIM), F32),
        jax.ShapeDtypeStruct((bsz, 1, N_HEADS), F32),
        jax.ShapeDtypeStruct((bsz, N_HEADS, HEAD_DIM, HEAD_DIM), F32),
        jax.ShapeDtypeStruct((bsz, CONV_TAPS - 1, cw), F32),
    ]
    ns, nh = n_seq * slots, n_seq * N_HEADS
    return pl.pallas_call(
        functools.partial(_mixer_body, n_seq=n_seq, n_chunks=n_chunks, n_tiles=n_tiles, valid=valid,
                          pipelined=pipelined),
        grid=(total + (1 if pipelined else 0),),
        in_specs=[
            pl.BlockSpec((n_seq, tile, n), lambda s: (*tile_a(s), 0)),
            pl.BlockSpec((n_seq, 4 * N_HEADS, tile), lambda s: (tile_a(s)[0], 0, tile_a(s)[1])),
            *state_specs,
            pl.BlockSpec((2, 4 * N_HEADS, tile), cst3),
            pl.BlockSpec((CONV_TAPS, cw), cst2),
            pl.BlockSpec((1, HEAD_DIM), cst2),
            pl.BlockSpec((1, HEAD_DIM), cst2),
            pl.BlockSpec((4 * N_HEADS * GATE_LANES, 2 * GATE_LANES), cst2),
        ],
        out_specs=[pl.BlockSpec((n_seq, tile, 2 * GROUP_W), lambda s: (*tile_b(s), 0)), *state_specs],
        out_shape=[jax.ShapeDtypeStruct((bsz, t, 2 * GROUP_W), BF16), *state_shapes],
        scratch_shapes=[
            pltpu.VMEM((nh, HEAD_DIM, HEAD_DIM), F32),
            pltpu.VMEM((nh, 8, HEAD_DIM), F32),
            pltpu.VMEM((n_seq, 8, GATE_LANES), F32),
            pltpu.VMEM((nh, HEAD_DIM, HEAD_DIM), F32),
            pltpu.VMEM((n_seq, HIST_ROW + tile, cw), F32),
            pltpu.VMEM((nh, tile, HEAD_DIM), BF16),
            pltpu.VMEM((nh, tile, HEAD_DIM), F32),
            pltpu.VMEM((nh, tile, HEAD_DIM), F32),
            pltpu.VMEM((ns, tile, GATE_LANES), F32),
            pltpu.VMEM((ns, tile, GATE_LANES), F32),
            pltpu.VMEM((ns, n_ch, HEAD_DIM), F32),
            pltpu.VMEM((ns, tile, GROUP_W), BF16),
            pltpu.VMEM((ns, tile, GROUP_W), BF16),
            pltpu.VMEM((ns, tile, GROUP_W), BF16),
            pltpu.VMEM((ns * n_ch, HEAD_DIM, HEAD_DIM), F32),
            pltpu.VMEM((ns * n_ch, CHUNK, HEAD_DIM), F32),
            pltpu.VMEM((ns * n_ch, HEAD_DIM, HEAD_DIM), F32),
            pltpu.VMEM((ns * n_ch, HEAD_DIM + CHUNK, HEAD_DIM), BF16),
            pltpu.VMEM((ns * n_ch, CHUNK, HEAD_DIM), F32),
        ],
        compiler_params=_params(("arbitrary",)),
        name="mixers",
    )(proj, gates, c0, n0, m0, s0, conv0, gate_par, conv_w, a_norm_w, b_norm_w, _spread_matrix())


def _mem_kv_body(mem_ref, nw_ref, wkv_ref, k_ref, v_ref, kb_ref, vb_ref):
    nb, n_mem, d = mem_ref.shape
    dh = d // N_XHEADS
    mn = _rms(mem_ref[...].reshape(nb * n_mem, d), nw_ref[...]).astype(BF16)
    for w0, f_ref, b_ref in ((0, k_ref, kb_ref), (d, v_ref, vb_ref)):
        y = _dot(mn, wkv_ref[:, w0:w0 + d])
        b_ref[...] = y.astype(BF16).reshape(nb, n_mem, d)
        for h in range(N_XHEADS):
            f_ref[:, :, h, :] = y[:, h * dh:(h + 1) * dh].reshape(nb, n_mem, dh)


def _mem_kv(mem, norm_w, wkv):
    bsz, n_mem, d = mem.shape
    nb = max(1, min(bsz, ROW_TILE // n_mem))
    assert bsz % nb == 0
    b3 = lambda i: (i, 0, 0)
    b4 = lambda i: (i, 0, 0, 0)
    cst = lambda i: (0, 0)
    dh = d // N_XHEADS
    return pl.pallas_call(
        _mem_kv_body,
        grid=(bsz // nb,),
        in_specs=[pl.BlockSpec((nb, n_mem, d), b3), pl.BlockSpec((1, d), cst), pl.BlockSpec((d, 2 * d), cst)],
        out_specs=[pl.BlockSpec((nb, n_mem, N_XHEADS, dh), b4)] * 2 + [pl.BlockSpec((nb, n_mem, d), b3)] * 2,
        out_shape=([jax.ShapeDtypeStruct((bsz, n_mem, N_XHEADS, dh), F32)] * 2
                   + [jax.ShapeDtypeStruct((bsz, n_mem, d), BF16)] * 2),
        compiler_params=_params(("arbitrary",)),
        name="mem_kv",
    )(mem, norm_w, wkv)


def _attn_body(h_ref, x_ref, wout_ref, nx_ref, wq_ref, mk_ref, mv_ref, wo_ref, o_ref, *, n_sub):
    nb, tm, d = x_ref.shape
    dh = d // N_XHEADS
    rows = nb * tm
    gr = rows // n_sub
    assert gr % tm == 0 or tm % gr == 0
    per = max(1, gr // tm)
    hcols = [slice(h * dh, (h + 1) * dh) for h in range(N_XHEADS)]

    def heads_of(ref, b):
        if len(ref.shape) == 4:
            return [ref[b, :, h, :].astype(BF16) for h in range(N_XHEADS)]
        return [ref[b, :, c] for c in hcols]

    def group_rows(ref, g):
        if per > 1 or gr == tm:
            return ref[g * per:(g + 1) * per].reshape(gr, d)
        return ref[(g * gr) // tm, (g * gr) % tm:(g * gr) % tm + gr, :]

    groups = range(n_sub)
    x1 = [group_rows(x_ref, g) + _dot(group_rows(h_ref, g), wout_ref[...]) for g in groups]
    xn = [_rms(x, nx_ref[...]).astype(BF16) for x in x1]
    q = [_dot(x, wq_ref[...]).astype(BF16) for x in xn]
    unit_rows = min(gr, tm)
    units = [(g, slice(u * unit_rows, (u + 1) * unit_rows), (g * gr + u * unit_rows) // tm)
             for g in groups for u in range(gr // unit_rows)]
    kv = {b: (heads_of(mk_ref, b), heads_of(mv_ref, b)) for b in sorted({b for _, _, b in units})}
    s = [[lax.dot_general(q[g][r, c], kh, _NT, preferred_element_type=F32) * (dh ** -0.5)
          for c, kh in zip(hcols, kv[b][0])] for g, r, b in units]
    mx = [[jnp.max(sh, axis=-1, keepdims=True) for sh in si] for si in s]
    e = [[jnp.exp(sh - mh) for sh, mh in zip(si, mi)] for si, mi in zip(s, mx)]
    den = [[jnp.sum(eh, axis=-1, keepdims=True) for eh in ei] for ei in e]
    p = [[(eh / dh_).astype(BF16) for eh, dh_ in zip(ei, di)] for ei, di in zip(e, den)]
    o = [[_dot(ph, vh).astype(BF16) for ph, vh in zip(pi, kv[b][1])] for pi, (_, _, b) in zip(p, units)]
    for g in groups:
        mine = [oi for oi, (ug, _, _) in zip(o, units) if ug == g]
        acc = x1[g]
        for h, c in enumerate(hcols):
            oh = mine[0][h] if len(mine) == 1 else jnp.concatenate([oi[h] for oi in mine], axis=0)
            acc = acc + _dot(oh, wo_ref[c, :])
        if per > 1 or gr == tm:
            o_ref[g * per:(g + 1) * per] = acc.reshape(per, tm, d)
        else:
            o_ref[(g * gr) // tm, (g * gr) % tm:(g * gr) % tm + gr, :] = acc


def _out_proj_attn(hcat, x, w_out, norm_w, wq, mk, mv, wo):
    bsz, t, d = x.shape
    tm = min(ROW_TILE, t)
    assert t % tm == 0
    nb = max(1, min(bsz, ATTN_MIN_ROWS // tm))
    assert bsz % nb == 0
    bj = lambda b, j: (b, j, 0)
    cst = lambda b, j: (0, 0)
    mem_spec = pl.BlockSpec((nb,) + mk.shape[1:], lambda b, j: (b,) + (0,) * (mk.ndim - 1))
    return pl.pallas_call(
        functools.partial(_attn_body, n_sub=2 if tm >= 256 else 1),
        grid=(bsz // nb, t // tm),
        in_specs=[
            pl.BlockSpec((nb, tm, d), bj),
            pl.BlockSpec((nb, tm, d), bj),
            pl.BlockSpec((d, d), cst),
            pl.BlockSpec((1, d), cst),
            pl.BlockSpec((d, d), cst),
            mem_spec,
            mem_spec,
            pl.BlockSpec((d, d), cst),
        ],
        out_specs=pl.BlockSpec((nb, tm, d), bj),
        out_shape=jax.ShapeDtypeStruct((bsz, t, d), F32),
        compiler_params=_params(("arbitrary", "arbitrary")),
        name="out_proj_attn",
    )(hcat, x, w_out, norm_w, wq, mk, mv, wo)


def _ffn_body(x_ref, nf_ref, w1_ref, w2_ref, nfin_ref, o_ref, *, hid_tile, final_norm):
    x = x_ref[...]
    xn = _rms(x, nf_ref[...]).astype(BF16)
    acc = x
    for c0 in range(0, w1_ref.shape[1], hid_tile):
        hdn = jnp.maximum(_dot(xn, w1_ref[:, c0:c0 + hid_tile]), 0.0)
        acc = acc + _dot((hdn * hdn).astype(BF16), w2_ref[c0:c0 + hid_tile, :])
    o_ref[...] = _rms(acc, nfin_ref[...]) if final_norm else acc


def _ffn(x2d, norm_w, w1, w2, final_w, *, final_norm):
    rows, d = x2d.shape
    dff = w1.shape[1]
    tm = min(ROW_TILE, rows)
    assert rows % tm == 0
    row = lambda i: (i, 0)
    cst = lambda i: (0, 0)
    return pl.pallas_call(
        functools.partial(_ffn_body, hid_tile=1024, final_norm=final_norm),
        grid=(rows // tm,),
        in_specs=[
            pl.BlockSpec((tm, d), row),
            pl.BlockSpec((1, d), cst),
            pl.BlockSpec((d, dff), cst, pipeline_mode=pl.Buffered(1)),
            pl.BlockSpec((dff, d), cst, pipeline_mode=pl.Buffered(1)),
            pl.BlockSpec((1, d), cst),
        ],
        out_specs=pl.BlockSpec((tm, d), row),
        out_shape=jax.ShapeDtypeStruct((rows, d), F32),
        compiler_params=_params(("arbitrary",)),
        name="ffn",
    )(x2d, norm_w, w1, w2, final_w)


def _split_w_in(w_in):
    w = GROUP_W
    g0 = 4 * w
    b0 = g0 + 2 * N_HEADS
    g1 = b0 + 4 * w
    main = jnp.concatenate([w_in[:, :g0], w_in[:, b0:g1]], axis=1).astype(BF16)
    gate = jnp.concatenate([w_in[:, g0:b0], w_in[:, g1:g1 + 2 * N_HEADS]], axis=1)
    return main, gate.T.astype(BF16)


def _gate_params(igate_b, fgate_b, a_log, dt_bias, width):
    zeros = jnp.zeros((N_HEADS,), F32)
    bias = jnp.concatenate([igate_b, fgate_b, dt_bias, zeros]).astype(F32)
    alog = jnp.concatenate([zeros, zeros, a_log.astype(F32), zeros])
    return jnp.broadcast_to(jnp.stack([bias, alog])[:, :, None], (2, 4 * N_HEADS, width))


def _trunk_layer(x, mem_k, mem_v, c0, n0, m0, s0, conv0, *, valid, w_main, w_gate, gate_vecs, lw, final_w,
                 final_norm):
    bsz, t, d = x.shape
    tp = CHUNK if valid < CHUNK else t
    xp = x if tp == t else jnp.pad(x, ((0, 0), (0, tp - t), (0, 0)))
    proj, gates = _in_proj(xp.reshape(bsz * tp, d), lw["norm_mix_w"], w_main, w_gate, tp)
    hcat, c1, n1, m1, s1, conv1 = _mixers(
        proj.reshape(bsz, tp, -1), gates, c0, n0, m0.reshape(bsz, 1, N_HEADS), s0, conv0,
        _gate_params(*gate_vecs, width=min(MIXER_TILE, tp)), lw["gdn_conv_w"], lw["mlstm_norm_w"],
        lw["gdn_norm_w"], valid=valid, tile=MIXER_TILE, n_seq=MIXER_SEQS if tp <= MIXER_TILE else 1)
    x2 = _out_proj_attn(hcat, x, lw["w_out"], lw["norm_x_w"], lw["wq_x"], mem_k, mem_v, lw["wo_x"])
    y = _ffn(x2.reshape(bsz * t, d), lw["norm_ffn_w"], lw["w_ff1"], lw["w_ff2"], final_w,
             final_norm=final_norm)
    return y.reshape(bsz, t, d), c1, n1, m1.reshape(bsz, N_HEADS), s1, conv1


def kernel(x_prompt, x_sample, state_mlstm_C, state_mlstm_n, state_mlstm_m, state_gdn_S, state_gdn_conv, cache_mem_k, cache_mem_v, mem_prompt, norm_mix_w, w_in, mlstm_igate_b, mlstm_fgate_b, mlstm_norm_w, gdn_conv_w, gdn_A_log, gdn_dt_bias, gdn_norm_w, w_out, norm_x_w, norm_mem_w, wq_x, wk_x, wv_x, wo_x, norm_ffn_w, w_ff1, w_ff2, norm_final_w):
    depth = w_in.shape[0]
    bp = x_prompt.shape[0]
    ts = x_sample.shape[1]
    row = lambda a: a.reshape(1, -1).astype(F32)
    hp, hs = x_prompt, x_sample
    outs_p = [[] for _ in range(7)]
    outs_s = [[] for _ in range(5)]
    for l in range(depth):
        w_main, w_gate = _split_w_in(w_in[l])
        gate_vecs = (mlstm_igate_b[l], mlstm_fgate_b[l], gdn_A_log[l], gdn_dt_bias[l])
        lw = dict(
            norm_mix_w=row(norm_mix_w[l]), gdn_conv_w=gdn_conv_w[l].astype(F32),
            mlstm_norm_w=row(mlstm_norm_w[l]), gdn_norm_w=row(gdn_norm_w[l]),
            w_out=w_out[l].astype(BF16), norm_x_w=row(norm_x_w[l]), wq_x=wq_x[l].astype(BF16),
            wo_x=wo_x[l].astype(BF16), norm_ffn_w=row(norm_ffn_w[l]),
            w_ff1=w_ff1[l].astype(BF16), w_ff2=w_ff2[l].astype(BF16))
        common = dict(w_main=w_main, w_gate=w_gate, gate_vecs=gate_vecs, lw=lw, final_w=row(norm_final_w),
                      final_norm=(l == depth - 1))
        wkv = jnp.concatenate([wk_x[l], wv_x[l]], axis=1).astype(BF16)
        mk, mv, mk_b, mv_b = _mem_kv(mem_prompt, row(norm_mem_w[l]), wkv)
        hp, c1, n1, m1, s1, cv1 = _trunk_layer(
            hp, mk_b, mv_b,
            jnp.zeros((bp, N_HEADS, HEAD_DIM, HEAD_DIM), F32), jnp.zeros((bp, N_HEADS, HEAD_DIM), F32),
            jnp.zeros((bp, N_HEADS), F32), jnp.zeros((bp, N_HEADS, HEAD_DIM, HEAD_DIM), F32),
            jnp.zeros((bp, CONV_TAPS - 1, 3 * GROUP_W), F32), valid=CHUNK, **common)
        for acc, val in zip(outs_p, (c1, n1, m1, s1, cv1, mk, mv)):
            acc.append(val)
        hs, c2, n2, m2, s2, cv2 = _trunk_layer(
            hs, cache_mem_k[l], cache_mem_v[l],
            state_mlstm_C[l], state_mlstm_n[l], state_mlstm_m[l], state_gdn_S[l], state_gdn_conv[l],
            valid=ts, **common)
        for acc, val in zip(outs_s, (c2, n2, m2, s2, cv2)):
            acc.append(val)
    return (hp, hs, *[jnp.stack(a) for a in outs_p], *[jnp.stack(a) for a in outs_s])
```

```python
import functools
import math

import jax
import jax.numpy as jnp
from jax import lax
from jax.experimental import pallas as pl
from jax.experimental.pallas import tpu as pltpu

F32 = jnp.float32
BF16 = jnp.bfloat16
EPS = 1e-6
N_HEADS = 4
HEAD_DIM = 128
GROUP_W = N_HEADS * HEAD_DIM
CONV_TAPS = 4
N_XHEADS = 4
CHUNK = 64
GATE_LANES = 128
NEG_BIG = -1e30
HIST_ROW = 8

VMEM_LIMIT_BYTES = 56 * 1024 * 1024
ROW_TILE = 1024
MIXER_TILE = 512
MIXER_SEQS = 4
ATTN_MIN_ROWS = 64

_NT = (((1,), (1,)), ((), ()))
_TN = (((0,), (0,)), ((), ()))


def _rms(x, w):
    return x * lax.rsqrt(jnp.mean(x * x, axis=-1, keepdims=True) + EPS) * w


def _dot(a, b):
    return jnp.dot(a, b, preferred_element_type=F32)


def _sigmoid(x):
    return 1.0 / (1.0 + jnp.exp(-x))


def _params(sem):
    return pltpu.CompilerParams(dimension_semantics=sem, vmem_limit_bytes=VMEM_LIMIT_BYTES)


def _in_proj_body(x_ref, nw_ref, w_ref, wg_ref, proj_ref, gate_ref, *, col_tile, n_sub):
    assert col_tile == GROUP_W
    tm = x_ref.shape[0]
    subs = [slice(i * (tm // n_sub), (i + 1) * (tm // n_sub)) for i in range(n_sub)]
    xn = [_rms(x_ref[r, :], nw_ref[...]).astype(BF16) for r in subs]
    for c0 in range(0, w_ref.shape[1], col_tile):
        for r, x in zip(subs, xn):
            y = _dot(x, w_ref[:, c0:c0 + col_tile])
            if c0 == 3 * GROUP_W:
                y = _sigmoid(y)
            elif c0 == 7 * GROUP_W:
                y = y * _sigmoid(y)
            proj_ref[r, c0:c0 + col_tile] = y.astype(BF16)
    gates = lax.dot_general(wg_ref[...], jnp.concatenate(xn, axis=0), _NT, preferred_element_type=F32)
    gate_ref[...] = gates.reshape(gate_ref.shape)


def _in_proj(x2d, norm_w, w_main, w_gate, seq_len):
    rows, d = x2d.shape
    n = w_main.shape[1]
    n_g = w_gate.shape[0]
    tm = min(ROW_TILE, rows)
    assert rows % tm == 0 and rows % seq_len == 0
    bsz = rows // seq_len
    direct = seq_len % tm == 0
    if direct:
        gate_spec = pl.BlockSpec((1, n_g, tm), lambda i: (i // (seq_len // tm), 0, i % (seq_len // tm)))
        gate_shape = jax.ShapeDtypeStruct((bsz, n_g, seq_len), F32)
    else:
        gate_spec = pl.BlockSpec((n_g, tm), lambda i: (0, i))
        gate_shape = jax.ShapeDtypeStruct((n_g, rows), F32)
    proj, gates = pl.pallas_call(
        functools.partial(_in_proj_body, col_tile=512, n_sub=2 if tm >= 512 else 1),
        grid=(rows // tm,),
        in_specs=[
            pl.BlockSpec((tm, d), lambda i: (i, 0)),
            pl.BlockSpec((1, d), lambda i: (0, 0)),
            pl.BlockSpec((d, n), lambda i: (0, 0)),
            pl.BlockSpec((n_g, d), lambda i: (0, 0)),
        ],
        out_specs=[pl.BlockSpec((tm, n), lambda i: (i, 0)), gate_spec],
        out_shape=[jax.ShapeDtypeStruct((rows, n), BF16), gate_shape],
        compiler_params=_params(("arbitrary",)),
        name="in_proj",
    )(x2d, norm_w, w_main, w_gate)
    if not direct:
        gates = gates.reshape(n_g, bsz, seq_len).transpose(1, 0, 2)
    return proj, gates


def _mixer_body(proj_ref, gate_ref, c0_ref, n0_ref, m0_ref, s0_ref, conv0_ref,
                gpar_ref, convw_ref, anw_ref, bnw_ref, spread_ref,
                h_ref, c1_ref, n1_ref, m1_ref, s1_ref, conv1_ref,
                c_sc, n_sc, m_sc, s_sc, u_sc, qn_sc, kf_sc, v_sc,
                z_sc, st_sc, dn_sc, q_sc, og_sc, zg_sc, dc_sc, sv_sc, nc_sc, lb_sc, qu_sc,
                *, n_seq, n_chunks, n_tiles, valid, pipelined):
    L = CHUNK
    T = n_chunks * L
    n_ch = n_chunks * N_HEADS
    slots = 2 if pipelined else 1
    step = pl.program_id(0)
    scale = HEAD_DIM ** -0.5
    heads = range(N_HEADS)
    chunks = range(n_chunks)
    seqs = range(n_seq)
    if pipelined:
        j_a = lax.rem(jnp.minimum(step, pl.num_programs(0) - 2), n_tiles)
        j_b = lax.rem(step + (n_tiles - 1), n_tiles)
        has_b = step >= 1
        slot_a = step & 1
        slot_b = 1 - slot_a
    else:
        j_a = j_b = slot_a = slot_b = 0
        has_b = True
    handoff = (z_sc, st_sc, dn_sc, q_sc, og_sc, zg_sc, dc_sc, sv_sc, nc_sc, lb_sc, qu_sc)

    def load_state():
        c_sc[...] = c0_ref[...].reshape(c_sc.shape)
        s_sc[...] = s0_ref[...].reshape(s_sc.shape)
        m_sc[...] = jnp.zeros(m_sc.shape, F32)
        for s in seqs:
            for h in heads:
                n_sc[s * N_HEADS + h, 0:1, :] = n0_ref[s, h:h + 1, :]
            m_sc[s, 0:1, N_HEADS:2 * N_HEADS] = m0_ref[s]

    def load_conv_history():
        u_sc[:, 0:HIST_ROW - (CONV_TAPS - 1), :] = jnp.zeros((n_seq, HIST_ROW - (CONV_TAPS - 1), u_sc.shape[2]), F32)
        u_sc[:, HIST_ROW - (CONV_TAPS - 1):HIST_ROW, :] = conv0_ref[...]

    if pipelined:
        @pl.when(step == 0)
        def _zero_scratch():
            for ref in handoff + (c_sc, n_sc, m_sc, s_sc):
                ref[...] = jnp.zeros(ref.shape, ref.dtype)

        pl.when(j_a == 0)(load_conv_history)
        pl.when(has_b & (j_b == 0))(load_state)
    else:
        load_conv_history()
        load_state()

    lane = lax.broadcasted_iota(jnp.int32, (L, 2 * L), 1)
    tok = lax.broadcasted_iota(jnp.int32, (L, 2 * L), 0)
    src = lane & (L - 1)
    left = lane < L
    tri_incl = tok >= src
    tri_strict = tok > src
    eye_left = jnp.where(left & (tok == src), 1.0, 0.0)
    lane_row = lax.broadcasted_iota(jnp.int32, (1, GATE_LANES), 1)
    head_lanes = (lane_row >= N_HEADS) & (lane_row < 2 * N_HEADS)
    base = 4 * GROUP_W
    ch = [(c, h) for c in chunks for h in heads]
    idx = {p: i for i, p in enumerate(ch)}

    def two(x):
        return jnp.concatenate([x, x], axis=0)

    def rows(c):
        return slice(c * L, (c + 1) * L)

    def hcols(h, group=0):
        return slice(group * GROUP_W + h * HEAD_DIM, group * GROUP_W + (h + 1) * HEAD_DIM)

    ones_sq = jnp.full((HEAD_DIM, HEAD_DIM), 1.0, BF16)

    def highest(a, b, dims=None):
        if dims is None:
            return jnp.dot(a, b, precision=lax.Precision.HIGHEST, preferred_element_type=F32)
        return lax.dot_general(a, b, dims, precision=lax.Precision.HIGHEST, preferred_element_type=F32)

    def phase_a(s):
        sa = s * slots + slot_a
        hd = s * N_HEADS
        n_g = 4 * N_HEADS
        row_g = lax.broadcasted_iota(jnp.int32, (n_g, T), 0)
        pre = gate_ref[s] + gpar_ref[0]
        e = jnp.exp(-jnp.abs(pre))
        t = jnp.log1p(e)
        act = jnp.where(row_g < 4, pre,
                        jnp.where(row_g < 8, -(jnp.maximum(-pre, 0.0) + t),
                                  jnp.where(row_g < 12, -jnp.exp(gpar_ref[1]) * (jnp.maximum(pre, 0.0) + t),
                                            jnp.where(pre >= 0.0, 1.0, e) / (1.0 + e))))
        if valid < L:
            tok_g = lax.broadcasted_iota(jnp.int32, (n_g, T), 1)
            act = jnp.where(tok_g >= valid, jnp.where(row_g < 4, NEG_BIG, 0.0), act)
        r_t = lax.broadcasted_iota(jnp.int32, (T, T), 0)
        c_t = lax.broadcasted_iota(jnp.int32, (T, T), 1)
        shift = int(math.log2(L))
        within = jnp.where((r_t <= c_t) & ((r_t >> shift) == (c_t >> shift)), 1.0, 0.0)
        z_g = jnp.where((row_g >= 4) & (row_g < 12), highest(act, within), act)
        pick = jnp.where(lax.broadcasted_iota(jnp.int32, (n_g, GATE_LANES), 0)
                         == lax.broadcasted_iota(jnp.int32, (n_g, GATE_LANES), 1), 1.0, 0.0)
        z_t = highest(z_g, pick, _TN)
        z_sc[sa] = z_t
        z = [z_t[rows(c), :] for c in chunks]

        def spread(x):
            hi = x.astype(BF16)
            lo = (x - hi.astype(F32)).astype(BF16)
            return lax.dot_general(jnp.concatenate([hi, lo], axis=1), spread_ref[...], _NT,
                                   preferred_element_type=F32)

        zb = [spread(z[c]) for c in chunks]

        def zcol(c, k):
            return zb[c][:, k * GATE_LANES:(k + 1) * GATE_LANES]
        r_d = lax.broadcasted_iota(jnp.int32, (T, 2 * L), 0)
        c_d = lax.broadcasted_iota(jnp.int32, (T, 2 * L), 1)
        zt = [highest(z_g, jnp.where(r_d == c * L + (c_d & (L - 1)), 1.0, 0.0)) for c in chunks]
        q_sc[sa] = proj_ref[s, :, 0:GROUP_W]
        og_sc[sa] = proj_ref[s, :, 3 * GROUP_W:4 * GROUP_W]
        zg_sc[sa] = proj_ref[s, :, base + 3 * GROUP_W:base + 4 * GROUP_W]
        yield

        u_sc[s, HIST_ROW:HIST_ROW + T, :] = proj_ref[s, :, base:base + 3 * GROUP_W].astype(F32)
        for blk in range(3 * N_HEADS):
            cols = slice(blk * HEAD_DIM, (blk + 1) * HEAD_DIM)
            u_all = u_sc[s, :, cols]
            y = convw_ref[0:1, cols] * u_all
            for tap in range(1, CONV_TAPS):
                y = pltpu.roll(y, 1, axis=0) + convw_ref[tap:tap + 1, cols] * u_all
            y = y[HIST_ROW:, :]
            y = y * _sigmoid(y)
            kind, h = divmod(blk, N_HEADS)
            if kind == 0:
                qn_sc[hd + h] = (y * lax.rsqrt(jnp.sum(y * y, axis=-1, keepdims=True) + EPS) * scale).astype(BF16)
            elif kind == 1:
                kf_sc[hd + h] = y * lax.rsqrt(jnp.sum(y * y, axis=-1, keepdims=True) + EPS)
            else:
                v_sc[hd + h] = y
        t_valid = T if valid == L else valid
        u_sc[s, HIST_ROW - 3:HIST_ROW, :] = u_sc[s, HIST_ROW + t_valid - 3:HIST_ROW + t_valid, :]
        yield

        def a_k(c, h):
            return proj_ref[s, rows(c), hcols(h, 1)]

        qk_a = {(c, h): lax.dot_general(proj_ref[s, rows(c), hcols(h)], two(a_k(c, h)), _NT,
                                        preferred_element_type=F32) for c, h in ch}
        qk_b = {}
        for c, h in ch:
            kn = kf_sc[hd + h, rows(c), :].astype(BF16)
            qk_b[c, h] = lax.dot_general(jnp.concatenate([qn_sc[hd + h, rows(c), :], kn], axis=0), two(kn), _NT,
                                         preferred_element_type=F32)
        yield

        d_a, m_a, s_a, kw_a = {}, {}, {}, {}
        for c, h in ch:
            ig_row, b_row = zt[c][h:h + 1, :], zt[c][4 + h:5 + h, :]
            d_a[c, h] = jnp.where(tri_incl, zcol(c, 4 + h) - b_row + ig_row, NEG_BIG)
        for c, h in ch:
            m_a[c, h] = jnp.max(d_a[c, h], axis=1, keepdims=True)
            st_sc[sa, rows(c), 4 + h:5 + h] = m_a[c, h]
        yield
        for c, h in ch:
            s_a[c, h] = jnp.where(left, qk_a[c, h] * scale * jnp.exp(d_a[c, h] - m_a[c, h]), 0.0)
        for c, h in ch:
            ig_b, b_b = zcol(c, h), zcol(c, 4 + h)
            w_s = jnp.exp(b_b[L - 1:L, :] - b_b + ig_b - m_a[c, h][L - 1:L, :])
            kw = a_k(c, h).astype(F32) * scale * w_s
            dn_sc[sa, idx[c, h]:idx[c, h] + 1, :] = jnp.sum(kw, axis=0, keepdims=True)
            kw_a[c, h] = kw.astype(BF16)
        yield
        for c, h in ch:
            v = proj_ref[s, rows(c), hcols(h, 2)]
            sv1 = _dot(s_a[c, h].astype(BF16), jnp.concatenate([two(v), ones_sq], axis=1))
            sv_sc[sa * n_ch + idx[c, h]] = sv1[:, :HEAD_DIM]
            st_sc[sa, rows(c), h:h + 1] = sv1[:, HEAD_DIM:HEAD_DIM + 1]
            dc_sc[sa * n_ch + idx[c, h]] = lax.dot_general(kw_a[c, h], v, _TN, preferred_element_type=F32)
        yield

        decay, r_b = {}, {}
        for c, h in ch:
            g_row = zt[c][8 + h:9 + h, :]
            decay[c, h] = jnp.exp(jnp.where(tri_incl, zcol(c, 8 + h) - g_row, NEG_BIG))
            a2 = jnp.where(tri_strict, zcol(c, 12 + h) * decay[c, h] * qk_b[c, h][L:, :], 0.0)
            r_b[c, h] = jnp.where(left, eye_left, -a2)
        yield
        for _ in range(int(math.log2(L))):
            for p in ch:
                r = r_b[p]
                out = _dot(jnp.where(left, 0.0, r).astype(BF16), two(r.astype(BF16)))
                r_b[p] = jnp.where(left, r + out, out)
            yield
        uk = {}
        for c, h in ch:
            beta = zcol(c, 12 + h)
            rhs = jnp.concatenate([beta * v_sc[hd + h, rows(c), :],
                                   beta * jnp.exp(zcol(c, 8 + h)) * kf_sc[hd + h, rows(c), :]], axis=1).astype(BF16)
            uk[c, h] = _dot(jnp.where(left, r_b[c, h], 0.0).astype(BF16), two(rhs)).astype(BF16)
        yield
        for c, h in ch:
            g_col = zcol(c, 8 + h)
            g_last = g_col[L - 1:L, :]
            qkd = jnp.where(left, qk_b[c, h][:L, :] * decay[c, h], 0.0).astype(BF16)
            qu = _dot(qkd, two(uk[c, h]))
            kd = (kf_sc[hd + h, rows(c), :] * jnp.exp(g_last - g_col)).astype(BF16)
            ku = lax.dot_general(kd, uk[c, h], _TN, preferred_element_type=F32)
            q_eff = jnp.exp(g_col) * qn_sc[hd + h, rows(c), :].astype(F32) - qu[:, HEAD_DIM:]
            i = sa * n_ch + idx[c, h]
            qu_sc[i] = qu[:, :HEAD_DIM]
            nc_sc[i] = ku[:, :HEAD_DIM]
            lb_sc[i] = jnp.concatenate([ku[:, HEAD_DIM:], q_eff], axis=0).astype(BF16)
        yield

    def phase_b(s):
        sb = s * slots + slot_b
        hd = s * N_HEADS
        for c in chunks:
            zc = z_sc[sb, rows(c), :]
            stc = st_sc[sb, rows(c), :]
            base_i = sb * n_ch + c * N_HEADS
            c_prev = [c_sc[hd + h] for h in heads]
            s_prev = [s_sc[hd + h] for h in heads]
            n_prev = [n_sc[hd + h, 0:1, :] for h in heads]
            qc = [_dot(q_sc[sb, rows(c), hcols(h)], c_prev[h].astype(BF16)) for h in heads]
            rm = [_dot(lb_sc[base_i + h], s_prev[h].astype(BF16)) for h in heads]
            m_prev = m_sc[s, 0:1, :]
            inter = zc + m_prev
            m = jnp.maximum(inter, stc)
            m_last = jnp.where(head_lanes, m[L - 1:L, :], 0.0)
            decay0_t = jnp.exp(zc[L - 1:L, :] + m_prev - m_last)
            f_new_t = jnp.exp(stc[L - 1:L, :] - m_last)
            m_sc[s, 0:1, :] = m_last
            for h in heads:
                hl = slice(N_HEADS + h, N_HEADS + h + 1)
                c_sc[hd + h] = decay0_t[:, hl] * c_prev[h] + f_new_t[:, hl] * dc_sc[base_i + h]
                n_sc[hd + h, 0:1, :] = (decay0_t[:, hl] * n_prev[h]
                                        + f_new_t[:, hl] * dn_sc[sb, c * N_HEADS + h:c * N_HEADS + h + 1, :])
            for h in heads:
                s_sc[hd + h] = (jnp.exp(zc[L - 1:L, 8 + h:9 + h]) * s_prev[h] + nc_sc[base_i + h]
                                - rm[h][:HEAD_DIM, :])
            yield
            w_inter_t = jnp.exp(inter - m)
            w_intra_t = jnp.exp(stc - m)
            inv_cap_t = jnp.exp(-m)
            qn = [jnp.sum(q_sc[sb, rows(c), hcols(h)].astype(F32) * n_prev[h], axis=-1, keepdims=True)
                  for h in heads]
            num, cap, o_b = [], [], []
            for h in heads:
                hl = slice(N_HEADS + h, N_HEADS + h + 1)
                w_inter, w_intra = w_inter_t[:, hl], w_intra_t[:, hl]
                num.append(w_inter * qc[h] + w_intra * sv_sc[base_i + h])
                den = w_inter * qn[h] + w_intra * stc[:, h:h + 1]
                cap.append(jnp.maximum(jnp.abs(den), inv_cap_t[:, hl]))
                o_b.append(rm[h][HEAD_DIM:, :] + qu_sc[base_i + h])
            ms_a = [jnp.mean(x * x, axis=-1, keepdims=True) for x in num]
            ms_b = [jnp.mean(x * x, axis=-1, keepdims=True) for x in o_b]
            for h in heads:
                hn = num[h] * lax.rsqrt(ms_a[h] + EPS * cap[h] * cap[h]) * anw_ref[...]
                h_ref[s, rows(c), hcols(h)] = (og_sc[sb, rows(c), hcols(h)].astype(F32) * hn).astype(BF16)
            for h in heads:
                on = o_b[h] * lax.rsqrt(ms_b[h] + EPS) * bnw_ref[...]
                h_ref[s, rows(c), hcols(h, 1)] = (on * zg_sc[sb, rows(c), hcols(h)].astype(F32)).astype(BF16)
            yield

    steps_a, steps_b = 9 + int(math.log2(L)), 2 * n_chunks
    gens_a = [phase_a(s) for s in seqs]
    gens_b = [phase_b(s) for s in seqs]

    def advance(gens):
        for g in gens:
            next(g)

    if pipelined:
        done_b = 0
        for k in range(steps_a):
            while done_b < steps_b and done_b * steps_a <= k * steps_b:
                advance(gens_b)
                done_b += 1
            advance(gens_a)
        for _ in range(steps_b - done_b):
            advance(gens_b)
    else:
        for _ in range(steps_a):
            advance(gens_a)
        for _ in range(steps_b):
            advance(gens_b)
    assert all(next(g, None) is None for g in gens_a + gens_b)

    def store_state():
        c1_ref[...] = c_sc[...].reshape(c1_ref.shape)
        s1_ref[...] = s_sc[...].reshape(s1_ref.shape)
        for s in seqs:
            for h in heads:
                n1_ref[s, h:h + 1, :] = n_sc[s * N_HEADS + h, 0:1, :]
            m1_ref[s] = m_sc[s, 0:1, N_HEADS:2 * N_HEADS]

    def store_conv_history():
        conv1_ref[...] = u_sc[:, HIST_ROW - 3:HIST_ROW, :]

    if pipelined:
        pl.when(has_b & (j_b == n_tiles - 1))(store_state)
        pl.when(j_a == n_tiles - 1)(store_conv_history)
    else:
        store_state()
        store_conv_history()


def _spread_matrix():
    n = lax.broadcasted_iota(jnp.int32, (4 * N_HEADS * GATE_LANES, 2 * GATE_LANES), 0)
    k = lax.broadcasted_iota(jnp.int32, (4 * N_HEADS * GATE_LANES, 2 * GATE_LANES), 1)
    return ((k % GATE_LANES) == (n // GATE_LANES)).astype(BF16)


def _mixers(proj, gates, c0, n0, m0, s0, conv0, gate_par, conv_w, a_norm_w, b_norm_w, *, valid, tile, n_seq):
    bsz, t, n = proj.shape
    tile = min(tile, t)
    assert t % tile == 0 and tile % CHUNK == 0 and bsz % n_seq == 0
    assert valid == CHUNK or (t == CHUNK and CONV_TAPS - 1 <= valid < CHUNK)
    n_chunks = tile // CHUNK
    n_tiles = t // tile
    pipelined = n_tiles > 1
    slots = 2 if pipelined else 1
    n_ch = n_chunks * N_HEADS
    cw = 3 * GROUP_W
    n_groups = bsz // n_seq
    total = n_groups * n_tiles
    if pipelined:
        pair_a = lambda s: jnp.minimum(s, total - 1)
        pair_b = lambda s: jnp.maximum(s - 1, 0)
    else:
        pair_a = pair_b = lambda s: s
    tile_a = lambda s: (pair_a(s) // n_tiles, pair_a(s) % n_tiles)
    tile_b = lambda s: (pair_b(s) // n_tiles, pair_b(s) % n_tiles)
    st4 = lambda s: (tile_b(s)[0], 0, 0, 0)
    st3 = lambda s: (tile_b(s)[0], 0, 0)
    cst2 = lambda s: (0, 0)
    cst3 = lambda s: (0, 0, 0)
    conv_spec = pl.BlockSpec((n_seq, CONV_TAPS - 1, cw), lambda s: (tile_a(s)[0], 0, 0))
    state_specs = [
        pl.BlockSpec((n_seq, N_HEADS, HEAD_DIM, HEAD_DIM), st4),
        pl.BlockSpec((n_seq, N_HEADS, HEAD_DIM), st3),
        pl.BlockSpec((n_seq, 1, N_HEADS), st3),
        pl.BlockSpec((n_seq, N_HEADS, HEAD_DIM, HEAD_DIM), st4),
        conv_spec,
    ]
    state_shapes = [
        jax.ShapeDtypeStruct((bsz, N_HEADS, HEAD_DIM, HEAD_DIM), F32),
        jax.ShapeDtypeStruct((bsz, N_HEADS, HEAD_DIM), F32),
        jax.ShapeDtypeStruct((bsz, 1, N_HEADS), F32),
        jax.ShapeDtypeStruct((bsz, N_HEADS, HEAD_DIM, HEAD_DIM), F32),
        jax.ShapeDtypeStruct((bsz, CONV_TAPS - 1, cw), F32),
    ]
    ns, nh = n_seq * slots, n_seq * N_HEADS
    return pl.pallas_call(
        functools.partial(_mixer_body, n_seq=n_seq, n_chunks=n_chunks, n_tiles=n_tiles, valid=valid,
                          pipelined=pipelined),
        grid=(total + (1 if pipelined else 0),),
        in_specs=[
            pl.BlockSpec((n_seq, tile, n), lambda s: (*tile_a(s), 0)),
            pl.BlockSpec((n_seq, 4 * N_HEADS, tile), lambda s: (tile_a(s)[0], 0, tile_a(s)[1])),
            *state_specs,
            pl.BlockSpec((2, 4 * N_HEADS, tile), cst3),
            pl.BlockSpec((CONV_TAPS, cw), cst2),
            pl.BlockSpec((1, HEAD_DIM), cst2),
            pl.BlockSpec((1, HEAD_DIM), cst2),
            pl.BlockSpec((4 * N_HEADS * GATE_LANES, 2 * GATE_LANES), cst2),
        ],
        out_specs=[pl.BlockSpec((n_seq, tile, 2 * GROUP_W), lambda s: (*tile_b(s), 0)), *state_specs],
        out_shape=[jax.ShapeDtypeStruct((bsz, t, 2 * GROUP_W), BF16), *state_shapes],
        scratch_shapes=[
            pltpu.VMEM((nh, HEAD_DIM, HEAD_DIM), F32),
            pltpu.VMEM((nh, 8, HEAD_DIM), F32),
            pltpu.VMEM((n_seq, 8, GATE_LANES), F32),
            pltpu.VMEM((nh, HEAD_DIM, HEAD_DIM), F32),
            pltpu.VMEM((n_seq, HIST_ROW + tile, cw), F32),
            pltpu.VMEM((nh, tile, HEAD_DIM), BF16),
            pltpu.VMEM((nh, tile, HEAD_DIM), F32),
            pltpu.VMEM((nh, tile, HEAD_DIM), F32),
            pltpu.VMEM((ns, tile, GATE_LANES), F32),
            pltpu.VMEM((ns, tile, GATE_LANES), F32),
            pltpu.VMEM((ns, n_ch, HEAD_DIM), F32),
            pltpu.VMEM((ns, tile, GROUP_W), BF16),
            pltpu.VMEM((ns, tile, GROUP_W), BF16),
            pltpu.VMEM((ns, tile, GROUP_W), BF16),
            pltpu.VMEM((ns * n_ch, HEAD_DIM, HEAD_DIM), F32),
            pltpu.VMEM((ns * n_ch, CHUNK, HEAD_DIM), F32),
            pltpu.VMEM((ns * n_ch, HEAD_DIM, HEAD_DIM), F32),
            pltpu.VMEM((ns * n_ch, HEAD_DIM + CHUNK, HEAD_DIM), BF16),
            pltpu.VMEM((ns * n_ch, CHUNK, HEAD_DIM), F32),
        ],
        compiler_params=_params(("arbitrary",)),
        name="mixers",
    )(proj, gates, c0, n0, m0, s0, conv0, gate_par, conv_w, a_norm_w, b_norm_w, _spread_matrix())


def _mem_kv_body(mem_ref, nw_ref, wkv_ref, k_ref, v_ref, kb_ref, vb_ref):
    nb, n_mem, d = mem_ref.shape
    dh = d // N_XHEADS
    mn = _rms(mem_ref[...].reshape(nb * n_mem, d), nw_ref[...]).astype(BF16)
    for w0, f_ref, b_ref in ((0, k_ref, kb_ref), (d, v_ref, vb_ref)):
        y = _dot(mn, wkv_ref[:, w0:w0 + d])
        b_ref[...] = y.astype(BF16).reshape(nb, n_mem, d)
        for h in range(N_XHEADS):
            f_ref[:, :, h, :] = y[:, h * dh:(h + 1) * dh].reshape(nb, n_mem, dh)


def _mem_kv(mem, norm_w, wkv):
    bsz, n_mem, d = mem.shape
    nb = max(1, min(bsz, ROW_TILE // n_mem))
    assert bsz % nb == 0
    b3 = lambda i: (i, 0, 0)
    b4 = lambda i: (i, 0, 0, 0)
    cst = lambda i: (0, 0)
    dh = d // N_XHEADS
    return pl.pallas_call(
        _mem_kv_body,
        grid=(bsz // nb,),
        in_specs=[pl.BlockSpec((nb, n_mem, d), b3), pl.BlockSpec((1, d), cst), pl.BlockSpec((d, 2 * d), cst)],
        out_specs=[pl.BlockSpec((nb, n_mem, N_XHEADS, dh), b4)] * 2 + [pl.BlockSpec((nb, n_mem, d), b3)] * 2,
        out_shape=([jax.ShapeDtypeStruct((bsz, n_mem, N_XHEADS, dh), F32)] * 2
                   + [jax.ShapeDtypeStruct((bsz, n_mem, d), BF16)] * 2),
        compiler_params=_params(("arbitrary",)),
        name="mem_kv",
    )(mem, norm_w, wkv)


def _attn_body(h_ref, x_ref, wout_ref, nx_ref, wq_ref, mk_ref, mv_ref, wo_ref, o_ref, *, n_sub):
    nb, tm, d = x_ref.shape
    dh = d // N_XHEADS
    rows = nb * tm
    gr = rows // n_sub
    assert gr % tm == 0 or tm % gr == 0
    per = max(1, gr // tm)
    hcols = [slice(h * dh, (h + 1) * dh) for h in range(N_XHEADS)]

    def heads_of(ref, b):
        if len(ref.shape) == 4:
            return [ref[b, :, h, :].astype(BF16) for h in range(N_XHEADS)]
        return [ref[b, :, c] for c in hcols]

    def group_rows(ref, g):
        if per > 1 or gr == tm:
            return ref[g * per:(g + 1) * per].reshape(gr, d)
        return ref[(g * gr) // tm, (g * gr) % tm:(g * gr) % tm + gr, :]

    groups = range(n_sub)
    x1 = [group_rows(x_ref, g) + _dot(group_rows(h_ref, g), wout_ref[...]) for g in groups]
    xn = [_rms(x, nx_ref[...]).astype(BF16) for x in x1]
    q = [_dot(x, wq_ref[...]).astype(BF16) for x in xn]
    unit_rows = min(gr, tm)
    units = [(g, slice(u * unit_rows, (u + 1) * unit_rows), (g * gr + u * unit_rows) // tm)
             for g in groups for u in range(gr // unit_rows)]
    kv = {b: (heads_of(mk_ref, b), heads_of(mv_ref, b)) for b in sorted({b for _, _, b in units})}
    s = [[lax.dot_general(q[g][r, c], kh, _NT, preferred_element_type=F32) * (dh ** -0.5)
          for c, kh in zip(hcols, kv[b][0])] for g, r, b in units]
    mx = [[jnp.max(sh, axis=-1, keepdims=True) for sh in si] for si in s]
    e = [[jnp.exp(sh - mh) for sh, mh in zip(si, mi)] for si, mi in zip(s, mx)]
    den = [[jnp.sum(eh, axis=-1, keepdims=True) for eh in ei] for ei in e]
    p = [[(eh / dh_).astype(BF16) for eh, dh_ in zip(ei, di)] for ei, di in zip(e, den)]
    o = [[_dot(ph, vh).astype(BF16) for ph, vh in zip(pi, kv[b][1])] for pi, (_, _, b) in zip(p, units)]
    for g in groups:
        mine = [oi for oi, (ug, _, _) in zip(o, units) if ug == g]
        acc = x1[g]
        for h, c in enumerate(hcols):
            oh = mine[0][h] if len(mine) == 1 else jnp.concatenate([oi[h] for oi in mine], axis=0)
            acc = acc + _dot(oh, wo_ref[c, :])
        if per > 1 or gr == tm:
            o_ref[g * per:(g + 1) * per] = acc.reshape(per, tm, d)
        else:
            o_ref[(g * gr) // tm, (g * gr) % tm:(g * gr) % tm + gr, :] = acc


def _out_proj_attn(hcat, x, w_out, norm_w, wq, mk, mv, wo):
    bsz, t, d = x.shape
    tm = min(ROW_TILE, t)
    assert t % tm == 0
    nb = max(1, min(bsz, ATTN_MIN_ROWS // tm))
    assert bsz % nb == 0
    bj = lambda b, j: (b, j, 0)
    cst = lambda b, j: (0, 0)
    mem_spec = pl.BlockSpec((nb,) + mk.shape[1:], lambda b, j: (b,) + (0,) * (mk.ndim - 1))
    return pl.pallas_call(
        functools.partial(_attn_body, n_sub=2 if tm >= 256 else 1),
        grid=(bsz // nb, t // tm),
        in_specs=[
            pl.BlockSpec((nb, tm, d), bj),
            pl.BlockSpec((nb, tm, d), bj),
            pl.BlockSpec((d, d), cst),
            pl.BlockSpec((1, d), cst),
            pl.BlockSpec((d, d), cst),
            mem_spec,
            mem_spec,
            pl.BlockSpec((d, d), cst),
        ],
        out_specs=pl.BlockSpec((nb, tm, d), bj),
        out_shape=jax.ShapeDtypeStruct((bsz, t, d), F32),
        compiler_params=_params(("arbitrary", "arbitrary")),
        name="out_proj_attn",
    )(hcat, x, w_out, norm_w, wq, mk, mv, wo)


def _ffn_body(x_ref, nf_ref, w1_ref, w2_ref, nfin_ref, o_ref, *, hid_tile, final_norm, n_sub):
    tm = x_ref.shape[0]
    subs = [slice(i * (tm // n_sub), (i + 1) * (tm // n_sub)) for i in range(n_sub)]
    acc = [x_ref[r, :] for r in subs]
    xn = [_rms(a, nf_ref[...]).astype(BF16) for a in acc]
    for c0 in range(0, w1_ref.shape[1], hid_tile):
        hdn = [jnp.maximum(_dot(x, w1_ref[:, c0:c0 + hid_tile]), 0.0) for x in xn]
        act = [(h * h).astype(BF16) for h in hdn]
        acc = [a + _dot(h, w2_ref[c0:c0 + hid_tile, :]) for a, h in zip(acc, act)]
    for r, a in zip(subs, acc):
        o_ref[r, :] = _rms(a, nfin_ref[...]) if final_norm else a


def _ffn(x2d, norm_w, w1, w2, final_w, *, final_norm):
    rows, d = x2d.shape
    dff = w1.shape[1]
    tm = min(ROW_TILE, rows)
    assert rows % tm == 0
    row = lambda i: (i, 0)
    cst = lambda i: (0, 0)
    return pl.pallas_call(
        functools.partial(_ffn_body, hid_tile=1024, final_norm=final_norm, n_sub=2 if tm >= 512 else 1),
        grid=(rows // tm,),
        in_specs=[
            pl.BlockSpec((tm, d), row),
            pl.BlockSpec((1, d), cst),
            pl.BlockSpec((d, dff), cst, pipeline_mode=pl.Buffered(1)),
            pl.BlockSpec((dff, d), cst, pipeline_mode=pl.Buffered(1)),
            pl.BlockSpec((1, d), cst),
        ],
        out_specs=pl.BlockSpec((tm, d), row),
        out_shape=jax.ShapeDtypeStruct((rows, d), F32),
        compiler_params=_params(("arbitrary",)),
        name="ffn",
    )(x2d, norm_w, w1, w2, final_w)


def _split_w_in(w_in):
    w = GROUP_W
    g0 = 4 * w
    b0 = g0 + 2 * N_HEADS
    g1 = b0 + 4 * w
    main = jnp.concatenate([w_in[:, :g0], w_in[:, b0:g1]], axis=1).astype(BF16)
    gate = jnp.concatenate([w_in[:, g0:b0], w_in[:, g1:g1 + 2 * N_HEADS]], axis=1)
    return main, gate.T.astype(BF16)


def _gate_params(igate_b, fgate_b, a_log, dt_bias, width):
    zeros = jnp.zeros((N_HEADS,), F32)
    bias = jnp.concatenate([igate_b, fgate_b, dt_bias, zeros]).astype(F32)
    alog = jnp.concatenate([zeros, zeros, a_log.astype(F32), zeros])
    return jnp.broadcast_to(jnp.stack([bias, alog])[:, :, None], (2, 4 * N_HEADS, width))


def _trunk_layer(x, mem_k, mem_v, c0, n0, m0, s0, conv0, *, valid, w_main, w_gate, gate_vecs, lw, final_w,
                 final_norm):
    bsz, t, d = x.shape
    tp = CHUNK if valid < CHUNK else t
    xp = x if tp == t else jnp.pad(x, ((0, 0), (0, tp - t), (0, 0)))
    proj, gates = _in_proj(xp.reshape(bsz * tp, d), lw["norm_mix_w"], w_main, w_gate, tp)
    hcat, c1, n1, m1, s1, conv1 = _mixers(
        proj.reshape(bsz, tp, -1), gates, c0, n0, m0.reshape(bsz, 1, N_HEADS), s0, conv0,
        _gate_params(*gate_vecs, width=min(MIXER_TILE, tp)), lw["gdn_conv_w"], lw["mlstm_norm_w"],
        lw["gdn_norm_w"], valid=valid, tile=MIXER_TILE, n_seq=MIXER_SEQS if tp <= MIXER_TILE else 1)
    x2 = _out_proj_attn(hcat, x, lw["w_out"], lw["norm_x_w"], lw["wq_x"], mem_k, mem_v, lw["wo_x"])
    y = _ffn(x2.reshape(bsz * t, d), lw["norm_ffn_w"], lw["w_ff1"], lw["w_ff2"], final_w,
             final_norm=final_norm)
    return y.reshape(bsz, t, d), c1, n1, m1.reshape(bsz, N_HEADS), s1, conv1


def kernel(x_prompt, x_sample, state_mlstm_C, state_mlstm_n, state_mlstm_m, state_gdn_S, state_gdn_conv, cache_mem_k, cache_mem_v, mem_prompt, norm_mix_w, w_in, mlstm_igate_b, mlstm_fgate_b, mlstm_norm_w, gdn_conv_w, gdn_A_log, gdn_dt_bias, gdn_norm_w, w_out, norm_x_w, norm_mem_w, wq_x, wk_x, wv_x, wo_x, norm_ffn_w, w_ff1, w_ff2, norm_final_w):
    depth = w_in.shape[0]
    bp = x_prompt.shape[0]
    ts = x_sample.shape[1]
    row = lambda a: a.reshape(1, -1).astype(F32)
    hp, hs = x_prompt, x_sample
    outs_p = [[] for _ in range(7)]
    outs_s = [[] for _ in range(5)]
    for l in range(depth):
        w_main, w_gate = _split_w_in(w_in[l])
        gate_vecs = (mlstm_igate_b[l], mlstm_fgate_b[l], gdn_A_log[l], gdn_dt_bias[l])
        lw = dict(
            norm_mix_w=row(norm_mix_w[l]), gdn_conv_w=gdn_conv_w[l].astype(F32),
            mlstm_norm_w=row(mlstm_norm_w[l]), gdn_norm_w=row(gdn_norm_w[l]),
            w_out=w_out[l].astype(BF16), norm_x_w=row(norm_x_w[l]), wq_x=wq_x[l].astype(BF16),
            wo_x=wo_x[l].astype(BF16), norm_ffn_w=row(norm_ffn_w[l]),
            w_ff1=w_ff1[l].astype(BF16), w_ff2=w_ff2[l].astype(BF16))
        common = dict(w_main=w_main, w_gate=w_gate, gate_vecs=gate_vecs, lw=lw, final_w=row(norm_final_w),
                      final_norm=(l == depth - 1))
        wkv = jnp.concatenate([wk_x[l], wv_x[l]], axis=1).astype(BF16)
        mk, mv, mk_b, mv_b = _mem_kv(mem_prompt, row(norm_mem_w[l]), wkv)
        hp, c1, n1, m1, s1, cv1 = _trunk_layer(
            hp, mk_b, mv_b,
            jnp.zeros((bp, N_HEADS, HEAD_DIM, HEAD_DIM), F32), jnp.zeros((bp, N_HEADS, HEAD_DIM), F32),
            jnp.zeros((bp, N_HEADS), F32), jnp.zeros((bp, N_HEADS, HEAD_DIM, HEAD_DIM), F32),
            jnp.zeros((bp, CONV_TAPS - 1, 3 * GROUP_W), F32), valid=CHUNK, **common)
        for acc, val in zip(outs_p, (c1, n1, m1, s1, cv1, mk, mv)):
            acc.append(val)
        hs, c2, n2, m2, s2, cv2 = _trunk_layer(
            hs, cache_mem_k[l], cache_mem_v[l],
            state_mlstm_C[l], state_mlstm_n[l], state_mlstm_m[l], state_gdn_S[l], state_gdn_conv[l],
            valid=ts, **common)
        for acc, val in zip(outs_s, (c2, n2, m2, s2, cv2)):
            acc.append(val)
    return (hp, hs, *[jnp.stack(a) for a in outs_p], *[jnp.stack(a) for a in outs_s])
```

```python
import functools
import math

import jax
import jax.numpy as jnp
from jax import lax
from jax.experimental import pallas as pl
from jax.experimental.pallas import tpu as pltpu

F32 = jnp.float32
BF16 = jnp.bfloat16
EPS = 1e-6
N_HEADS = 4
HEAD_DIM = 128
GROUP_W = N_HEADS * HEAD_DIM
CONV_TAPS = 4
N_XHEADS = 4
CHUNK = 64
GATE_LANES = 128
NEG_BIG = -1e30
HIST_ROW = 8

VMEM_LIMIT_BYTES = 56 * 1024 * 1024
ROW_TILE = 1024
MIXER_TILE = 512
MIXER_SEQS = 4
ATTN_MIN_ROWS = 64

_NT = (((1,), (1,)), ((), ()))
_TN = (((0,), (0,)), ((), ()))


def _rms(x, w):
    return x * lax.rsqrt(jnp.mean(x * x, axis=-1, keepdims=True) + EPS) * w


def _dot(a, b):
    return jnp.dot(a, b, preferred_element_type=F32)


def _sigmoid(x):
    return 1.0 / (1.0 + jnp.exp(-x))


def _params(sem):
    return pltpu.CompilerParams(dimension_semantics=sem, vmem_limit_bytes=VMEM_LIMIT_BYTES)


def _in_proj_body(x_ref, nw_ref, w_ref, wg_ref, proj_ref, gate_ref, *, col_tile, n_sub):
    assert col_tile == GROUP_W
    tm = x_ref.shape[0]
    subs = [slice(i * (tm // n_sub), (i + 1) * (tm // n_sub)) for i in range(n_sub)]
    xn = [_rms(x_ref[r, :], nw_ref[...]).astype(BF16) for r in subs]
    for c0 in range(0, w_ref.shape[1], col_tile):
        for r, x in zip(subs, xn):
            y = _dot(x, w_ref[:, c0:c0 + col_tile])
            if c0 == 3 * GROUP_W:
                y = _sigmoid(y)
            elif c0 == 7 * GROUP_W:
                y = y * _sigmoid(y)
            proj_ref[r, c0:c0 + col_tile] = y.astype(BF16)
    gates = lax.dot_general(wg_ref[...], jnp.concatenate(xn, axis=0), _NT, preferred_element_type=F32)
    gate_ref[...] = gates.reshape(gate_ref.shape)


def _in_proj(x2d, norm_w, w_main, w_gate, seq_len):
    rows, d = x2d.shape
    n = w_main.shape[1]
    n_g = w_gate.shape[0]
    tm = min(ROW_TILE, rows)
    assert rows % tm == 0 and rows % seq_len == 0
    bsz = rows // seq_len
    direct = seq_len % tm == 0
    if direct:
        gate_spec = pl.BlockSpec((1, n_g, tm), lambda i: (i // (seq_len // tm), 0, i % (seq_len // tm)))
        gate_shape = jax.ShapeDtypeStruct((bsz, n_g, seq_len), F32)
    else:
        gate_spec = pl.BlockSpec((n_g, tm), lambda i: (0, i))
        gate_shape = jax.ShapeDtypeStruct((n_g, rows), F32)
    proj, gates = pl.pallas_call(
        functools.partial(_in_proj_body, col_tile=512, n_sub=2 if tm >= 512 else 1),
        grid=(rows // tm,),
        in_specs=[
            pl.BlockSpec((tm, d), lambda i: (i, 0)),
            pl.BlockSpec((1, d), lambda i: (0, 0)),
            pl.BlockSpec((d, n), lambda i: (0, 0)),
            pl.BlockSpec((n_g, d), lambda i: (0, 0)),
        ],
        out_specs=[pl.BlockSpec((tm, n), lambda i: (i, 0)), gate_spec],
        out_shape=[jax.ShapeDtypeStruct((rows, n), BF16), gate_shape],
        compiler_params=_params(("arbitrary",)),
        name="in_proj",
    )(x2d, norm_w, w_main, w_gate)
    if not direct:
        gates = gates.reshape(n_g, bsz, seq_len).transpose(1, 0, 2)
    return proj, gates


def _mixer_body(proj_ref, gate_ref, c0_ref, n0_ref, m0_ref, s0_ref, conv0_ref,
                gpar_ref, convw_ref, anw_ref, bnw_ref, spread_ref,
                h_ref, c1_ref, n1_ref, m1_ref, s1_ref, conv1_ref,
                c_sc, n_sc, m_sc, s_sc, u_sc, qn_sc, kf_sc, v_sc,
                z_sc, st_sc, dn_sc, q_sc, og_sc, zg_sc, dc_sc, sv_sc, nc_sc, lb_sc, qu_sc,
                *, n_seq, n_chunks, n_tiles, valid, pipelined):
    L = CHUNK
    T = n_chunks * L
    n_ch = n_chunks * N_HEADS
    slots = 2 if pipelined else 1
    step = pl.program_id(0)
    scale = HEAD_DIM ** -0.5
    heads = range(N_HEADS)
    chunks = range(n_chunks)
    seqs = range(n_seq)
    if pipelined:
        j_a = lax.rem(jnp.minimum(step, pl.num_programs(0) - 2), n_tiles)
        j_b = lax.rem(step + (n_tiles - 1), n_tiles)
        has_b = step >= 1
        slot_a = step & 1
        slot_b = 1 - slot_a
    else:
        j_a = j_b = slot_a = slot_b = 0
        has_b = True
    handoff = (z_sc, st_sc, dn_sc, q_sc, og_sc, zg_sc, dc_sc, sv_sc, nc_sc, lb_sc, qu_sc)

    def load_state():
        c_sc[...] = c0_ref[...].reshape(c_sc.shape)
        s_sc[...] = s0_ref[...].reshape(s_sc.shape)
        m_sc[...] = jnp.zeros(m_sc.shape, F32)
        for s in seqs:
            for h in heads:
                n_sc[s * N_HEADS + h, 0:1, :] = n0_ref[s, h:h + 1, :]
            m_sc[s, 0:1, N_HEADS:2 * N_HEADS] = m0_ref[s]

    def load_conv_history():
        u_sc[:, 0:HIST_ROW - (CONV_TAPS - 1), :] = jnp.zeros((n_seq, HIST_ROW - (CONV_TAPS - 1), u_sc.shape[2]), F32)
        u_sc[:, HIST_ROW - (CONV_TAPS - 1):HIST_ROW, :] = conv0_ref[...]

    if pipelined:
        @pl.when(step == 0)
        def _zero_scratch():
            for ref in handoff + (c_sc, n_sc, m_sc, s_sc):
                ref[...] = jnp.zeros(ref.shape, ref.dtype)

        pl.when(j_a == 0)(load_conv_history)
        pl.when(has_b & (j_b == 0))(load_state)
    else:
        load_conv_history()
        load_state()

    lane = lax.broadcasted_iota(jnp.int32, (L, 2 * L), 1)
    tok = lax.broadcasted_iota(jnp.int32, (L, 2 * L), 0)
    src = lane & (L - 1)
    left = lane < L
    tri_incl = tok >= src
    tri_strict = tok > src
    eye_left = jnp.where(left & (tok == src), 1.0, 0.0)
    lane_row = lax.broadcasted_iota(jnp.int32, (1, GATE_LANES), 1)
    head_lanes = (lane_row >= N_HEADS) & (lane_row < 2 * N_HEADS)
    base = 4 * GROUP_W
    ch = [(c, h) for c in chunks for h in heads]
    idx = {p: i for i, p in enumerate(ch)}

    def two(x):
        return jnp.concatenate([x, x], axis=0)

    def rows(c):
        return slice(c * L, (c + 1) * L)

    def hcols(h, group=0):
        return slice(group * GROUP_W + h * HEAD_DIM, group * GROUP_W + (h + 1) * HEAD_DIM)

    ones_sq = jnp.full((HEAD_DIM, HEAD_DIM), 1.0, BF16)

    def highest(a, b, dims=None):
        if dims is None:
            return jnp.dot(a, b, precision=lax.Precision.HIGHEST, preferred_element_type=F32)
        return lax.dot_general(a, b, dims, precision=lax.Precision.HIGHEST, preferred_element_type=F32)

    def phase_a(s):
        sa = s * slots + slot_a
        hd = s * N_HEADS
        n_g = 4 * N_HEADS
        row_g = lax.broadcasted_iota(jnp.int32, (n_g, T), 0)
        pre = gate_ref[s] + gpar_ref[0]
        e = jnp.exp(-jnp.abs(pre))
        t = jnp.log1p(e)
        act = jnp.where(row_g < 4, pre,
                        jnp.where(row_g < 8, -(jnp.maximum(-pre, 0.0) + t),
                                  jnp.where(row_g < 12, -jnp.exp(gpar_ref[1]) * (jnp.maximum(pre, 0.0) + t),
                                            jnp.where(pre >= 0.0, 1.0, e) / (1.0 + e))))
        if valid < L:
            tok_g = lax.broadcasted_iota(jnp.int32, (n_g, T), 1)
            act = jnp.where(tok_g >= valid, jnp.where(row_g < 4, NEG_BIG, 0.0), act)
        r_t = lax.broadcasted_iota(jnp.int32, (T, T), 0)
        c_t = lax.broadcasted_iota(jnp.int32, (T, T), 1)
        shift = int(math.log2(L))
        within = jnp.where((r_t <= c_t) & ((r_t >> shift) == (c_t >> shift)), 1.0, 0.0)
        z_g = jnp.where((row_g >= 4) & (row_g < 12), highest(act, within), act)
        pick = jnp.where(lax.broadcasted_iota(jnp.int32, (n_g, GATE_LANES), 0)
                         == lax.broadcasted_iota(jnp.int32, (n_g, GATE_LANES), 1), 1.0, 0.0)
        z_t = highest(z_g, pick, _TN)
        z_sc[sa] = z_t
        z = [z_t[rows(c), :] for c in chunks]

        def spread(x):
            hi = x.astype(BF16)
            lo = (x - hi.astype(F32)).astype(BF16)
            return lax.dot_general(jnp.concatenate([hi, lo], axis=1), spread_ref[...], _NT,
                                   preferred_element_type=F32)

        zb = [spread(z[c]) for c in chunks]

        def zcol(c, k):
            return zb[c][:, k * GATE_LANES:(k + 1) * GATE_LANES]
        r_d = lax.broadcasted_iota(jnp.int32, (T, 2 * L), 0)
        c_d = lax.broadcasted_iota(jnp.int32, (T, 2 * L), 1)
        zt = [highest(z_g, jnp.where(r_d == c * L + (c_d & (L - 1)), 1.0, 0.0)) for c in chunks]
        q_sc[sa] = proj_ref[s, :, 0:GROUP_W]
        og_sc[sa] = proj_ref[s, :, 3 * GROUP_W:4 * GROUP_W]
        zg_sc[sa] = proj_ref[s, :, base + 3 * GROUP_W:base + 4 * GROUP_W]
        yield

        u_sc[s, HIST_ROW:HIST_ROW + T, :] = proj_ref[s, :, base:base + 3 * GROUP_W].astype(F32)
        for blk in range(3 * N_HEADS):
            cols = slice(blk * HEAD_DIM, (blk + 1) * HEAD_DIM)
            u_all = u_sc[s, :, cols]
            y = convw_ref[0:1, cols] * u_all
            for tap in range(1, CONV_TAPS):
                y = pltpu.roll(y, 1, axis=0) + convw_ref[tap:tap + 1, cols] * u_all
            y = y[HIST_ROW:, :]
            y = y * _sigmoid(y)
            kind, h = divmod(blk, N_HEADS)
            if kind == 0:
                qn_sc[hd + h] = (y * lax.rsqrt(jnp.sum(y * y, axis=-1, keepdims=True) + EPS) * scale).astype(BF16)
            elif kind == 1:
                kf_sc[hd + h] = y * lax.rsqrt(jnp.sum(y * y, axis=-1, keepdims=True) + EPS)
            else:
                v_sc[hd + h] = y
        t_valid = T if valid == L else valid
        u_sc[s, HIST_ROW - 3:HIST_ROW, :] = u_sc[s, HIST_ROW + t_valid - 3:HIST_ROW + t_valid, :]
        yield

        def a_k(c, h):
            return proj_ref[s, rows(c), hcols(h, 1)]

        qk_a = {(c, h): lax.dot_general(proj_ref[s, rows(c), hcols(h)], two(a_k(c, h)), _NT,
                                        preferred_element_type=F32) for c, h in ch}
        qk_b = {}
        for c, h in ch:
            kn = kf_sc[hd + h, rows(c), :].astype(BF16)
            qk_b[c, h] = lax.dot_general(jnp.concatenate([qn_sc[hd + h, rows(c), :], kn], axis=0), two(kn), _NT,
                                         preferred_element_type=F32)
        yield

        d_a, m_a, s_a, kw_a = {}, {}, {}, {}
        for c, h in ch:
            ig_row, b_row = zt[c][h:h + 1, :], zt[c][4 + h:5 + h, :]
            d_a[c, h] = jnp.where(tri_incl, zcol(c, 4 + h) - b_row + ig_row, NEG_BIG)
        for c, h in ch:
            m_a[c, h] = jnp.max(d_a[c, h], axis=1, keepdims=True)
            st_sc[sa, rows(c), 4 + h:5 + h] = m_a[c, h]
        yield
        for c, h in ch:
            s_a[c, h] = jnp.where(left, qk_a[c, h] * scale * jnp.exp(d_a[c, h] - m_a[c, h]), 0.0)
        for c, h in ch:
            ig_b, b_b = zcol(c, h), zcol(c, 4 + h)
            w_s = jnp.exp(b_b[L - 1:L, :] - b_b + ig_b - m_a[c, h][L - 1:L, :])
            kw = a_k(c, h).astype(F32) * scale * w_s
            dn_sc[sa, idx[c, h]:idx[c, h] + 1, :] = jnp.sum(kw, axis=0, keepdims=True)
            kw_a[c, h] = kw.astype(BF16)
        yield
        for c, h in ch:
            v = proj_ref[s, rows(c), hcols(h, 2)]
            sv1 = _dot(s_a[c, h].astype(BF16), jnp.concatenate([two(v), ones_sq], axis=1))
            sv_sc[sa * n_ch + idx[c, h]] = sv1[:, :HEAD_DIM]
            st_sc[sa, rows(c), h:h + 1] = sv1[:, HEAD_DIM:HEAD_DIM + 1]
            dc_sc[sa * n_ch + idx[c, h]] = lax.dot_general(kw_a[c, h], v, _TN, preferred_element_type=F32)
        yield

        decay, r_b = {}, {}
        for c, h in ch:
            g_row = zt[c][8 + h:9 + h, :]
            decay[c, h] = jnp.exp(jnp.where(tri_incl, zcol(c, 8 + h) - g_row, NEG_BIG))
            a2 = jnp.where(tri_strict, zcol(c, 12 + h) * decay[c, h] * qk_b[c, h][L:, :], 0.0)
            r_b[c, h] = jnp.where(left, eye_left, -a2)
        yield
        for _ in range(int(math.log2(L))):
            for p in ch:
                r = r_b[p]
                out = _dot(jnp.where(left, 0.0, r).astype(BF16), two(r.astype(BF16)))
                r_b[p] = jnp.where(left, r + out, out)
            yield
        uk = {}
        for c, h in ch:
            beta = zcol(c, 12 + h)
            rhs = jnp.concatenate([beta * v_sc[hd + h, rows(c), :],
                                   beta * jnp.exp(zcol(c, 8 + h)) * kf_sc[hd + h, rows(c), :]], axis=1).astype(BF16)
            uk[c, h] = _dot(jnp.where(left, r_b[c, h], 0.0).astype(BF16), two(rhs)).astype(BF16)
        yield
        for c, h in ch:
            g_col = zcol(c, 8 + h)
            g_last = g_col[L - 1:L, :]
            qkd = jnp.where(left, qk_b[c, h][:L, :] * decay[c, h], 0.0).astype(BF16)
            qu = _dot(qkd, two(uk[c, h]))
            kd = (kf_sc[hd + h, rows(c), :] * jnp.exp(g_last - g_col)).astype(BF16)
            ku = lax.dot_general(kd, uk[c, h], _TN, preferred_element_type=F32)
            q_eff = jnp.exp(g_col) * qn_sc[hd + h, rows(c), :].astype(F32) - qu[:, HEAD_DIM:]
            i = sa * n_ch + idx[c, h]
            qu_sc[i] = qu[:, :HEAD_DIM]
            nc_sc[i] = ku[:, :HEAD_DIM]
            lb_sc[i] = jnp.concatenate([ku[:, HEAD_DIM:], q_eff], axis=0).astype(BF16)
        yield

    def phase_b(s):
        sb = s * slots + slot_b
        hd = s * N_HEADS
        for c in chunks:
            zc = z_sc[sb, rows(c), :]
            stc = st_sc[sb, rows(c), :]
            base_i = sb * n_ch + c * N_HEADS
            c_prev = [c_sc[hd + h] for h in heads]
            s_prev = [s_sc[hd + h] for h in heads]
            n_prev = [n_sc[hd + h, 0:1, :] for h in heads]
            qc = [_dot(q_sc[sb, rows(c), hcols(h)], c_prev[h].astype(BF16)) for h in heads]
            rm = [_dot(lb_sc[base_i + h], s_prev[h].astype(BF16)) for h in heads]
            m_prev = m_sc[s, 0:1, :]
            inter = zc + m_prev
            m = jnp.maximum(inter, stc)
            m_last = jnp.where(head_lanes, m[L - 1:L, :], 0.0)
            decay0_t = jnp.exp(zc[L - 1:L, :] + m_prev - m_last)
            f_new_t = jnp.exp(stc[L - 1:L, :] - m_last)
            m_sc[s, 0:1, :] = m_last
            for h in heads:
                hl = slice(N_HEADS + h, N_HEADS + h + 1)
                c_sc[hd + h] = decay0_t[:, hl] * c_prev[h] + f_new_t[:, hl] * dc_sc[base_i + h]
                n_sc[hd + h, 0:1, :] = (decay0_t[:, hl] * n_prev[h]
                                        + f_new_t[:, hl] * dn_sc[sb, c * N_HEADS + h:c * N_HEADS + h + 1, :])
            for h in heads:
                s_sc[hd + h] = (jnp.exp(zc[L - 1:L, 8 + h:9 + h]) * s_prev[h] + nc_sc[base_i + h]
                                - rm[h][:HEAD_DIM, :])
            yield
            w_inter_t = jnp.exp(inter - m)
            w_intra_t = jnp.exp(stc - m)
            inv_cap_t = jnp.exp(-m)
            qn = [jnp.sum(q_sc[sb, rows(c), hcols(h)].astype(F32) * n_prev[h], axis=-1, keepdims=True)
                  for h in heads]
            num, cap, o_b = [], [], []
            for h in heads:
                hl = slice(N_HEADS + h, N_HEADS + h + 1)
                w_inter, w_intra = w_inter_t[:, hl], w_intra_t[:, hl]
                num.append(w_inter * qc[h] + w_intra * sv_sc[base_i + h])
                den = w_inter * qn[h] + w_intra * stc[:, h:h + 1]
                cap.append(jnp.maximum(jnp.abs(den), inv_cap_t[:, hl]))
                o_b.append(rm[h][HEAD_DIM:, :] + qu_sc[base_i + h])
            ms_a = [jnp.mean(x * x, axis=-1, keepdims=True) for x in num]
            ms_b = [jnp.mean(x * x, axis=-1, keepdims=True) for x in o_b]
            for h in heads:
                hn = num[h] * lax.rsqrt(ms_a[h] + EPS * cap[h] * cap[h]) * anw_ref[...]
                h_ref[s, rows(c), hcols(h)] = (og_sc[sb, rows(c), hcols(h)].astype(F32) * hn).astype(BF16)
            for h in heads:
                on = o_b[h] * lax.rsqrt(ms_b[h] + EPS) * bnw_ref[...]
                h_ref[s, rows(c), hcols(h, 1)] = (on * zg_sc[sb, rows(c), hcols(h)].astype(F32)).astype(BF16)
            yield

    steps_a, steps_b = 9 + int(math.log2(L)), 2 * n_chunks
    gens_a = [phase_a(s) for s in seqs]
    gens_b = [phase_b(s) for s in seqs]

    def advance(gens):
        for g in gens:
            next(g)

    if pipelined:
        done_b = 0
        for k in range(steps_a):
            while done_b < steps_b and done_b * steps_a <= k * steps_b:
                advance(gens_b)
                done_b += 1
            advance(gens_a)
        for _ in range(steps_b - done_b):
            advance(gens_b)
    else:
        for _ in range(steps_a):
            advance(gens_a)
        for _ in range(steps_b):
            advance(gens_b)
    assert all(next(g, None) is None for g in gens_a + gens_b)

    def store_state():
        c1_ref[...] = c_sc[...].reshape(c1_ref.shape)
        s1_ref[...] = s_sc[...].reshape(s1_ref.shape)
        for s in seqs:
            for h in heads:
                n1_ref[s, h:h + 1, :] = n_sc[s * N_HEADS + h, 0:1, :]
            m1_ref[s] = m_sc[s, 0:1, N_HEADS:2 * N_HEADS]

    def store_conv_history():
        conv1_ref[...] = u_sc[:, HIST_ROW - 3:HIST_ROW, :]

    if pipelined:
        pl.when(has_b & (j_b == n_tiles - 1))(store_state)
        pl.when(j_a == n_tiles - 1)(store_conv_history)
    else:
        store_state()
        store_conv_history()


def _spread_matrix():
    n = lax.broadcasted_iota(jnp.int32, (4 * N_HEADS * GATE_LANES, 2 * GATE_LANES), 0)
    k = lax.broadcasted_iota(jnp.int32, (4 * N_HEADS * GATE_LANES, 2 * GATE_LANES), 1)
    return ((k % GATE_LANES) == (n // GATE_LANES)).astype(BF16)


def _mixers(proj, gates, c0, n0, m0, s0, conv0, gate_par, conv_w, a_norm_w, b_norm_w, *, valid, tile, n_seq):
    bsz, t, n = proj.shape
    tile = min(tile, t)
    assert t % tile == 0 and tile % CHUNK == 0 and bsz % n_seq == 0
    assert valid == CHUNK or (t == CHUNK and CONV_TAPS - 1 <= valid < CHUNK)
    n_chunks = tile // CHUNK
    n_tiles = t // tile
    pipelined = n_tiles > 1
    slots = 2 if pipelined else 1
    n_ch = n_chunks * N_HEADS
    cw = 3 * GROUP_W
    n_groups = bsz // n_seq
    total = n_groups * n_tiles
    if pipelined:
        pair_a = lambda s: jnp.minimum(s, total - 1)
        pair_b = lambda s: jnp.maximum(s - 1, 0)
    else:
        pair_a = pair_b = lambda s: s
    tile_a = lambda s: (pair_a(s) // n_tiles, pair_a(s) % n_tiles)
    tile_b = lambda s: (pair_b(s) // n_tiles, pair_b(s) % n_tiles)
    st4 = lambda s: (tile_b(s)[0], 0, 0, 0)
    st3 = lambda s: (tile_b(s)[0], 0, 0)
    cst2 = lambda s: (0, 0)
    cst3 = lambda s: (0, 0, 0)
    conv_spec = pl.BlockSpec((n_seq, CONV_TAPS - 1, cw), lambda s: (tile_a(s)[0], 0, 0))
    state_specs = [
        pl.BlockSpec((n_seq, N_HEADS, HEAD_DIM, HEAD_DIM), st4),
        pl.BlockSpec((n_seq, N_HEADS, HEAD_DIM), st3),
        pl.BlockSpec((n_seq, 1, N_HEADS), st3),
        pl.BlockSpec((n_seq, N_HEADS, HEAD_DIM, HEAD_DIM), st4),
        conv_spec,
    ]
    state_shapes = [
        jax.ShapeDtypeStruct((bsz, N_HEADS, HEAD_DIM, HEAD_DIM), F32),
        jax.ShapeDtypeStruct((bsz, N_HEADS, HEAD_DIM), F32),
        jax.ShapeDtypeStruct((bsz, 1, N_HEADS), F32),
        jax.ShapeDtypeStruct((bsz, N_HEADS, HEAD_DIM, HEAD_DIM), F32),
        jax.ShapeDtypeStruct((bsz, CONV_TAPS - 1, cw), F32),
    ]
    ns, nh = n_seq * slots, n_seq * N_HEADS
    return pl.pallas_call(
        functools.partial(_mixer_body, n_seq=n_seq, n_chunks=n_chunks, n_tiles=n_tiles, valid=valid,
                          pipelined=pipelined),
        grid=(total + (1 if pipelined else 0),),
        in_specs=[
            pl.BlockSpec((n_seq, tile, n), lambda s: (*tile_a(s), 0)),
            pl.BlockSpec((n_seq, 4 * N_HEADS, tile), lambda s: (tile_a(s)[0], 0, tile_a(s)[1])),
            *state_specs,
            pl.BlockSpec((2, 4 * N_HEADS, tile), cst3),
            pl.BlockSpec((CONV_TAPS, cw), cst2),
            pl.BlockSpec((1, HEAD_DIM), cst2),
            pl.BlockSpec((1, HEAD_DIM), cst2),
            pl.BlockSpec((4 * N_HEADS * GATE_LANES, 2 * GATE_LANES), cst2),
        ],
        out_specs=[pl.BlockSpec((n_seq, tile, 2 * GROUP_W), lambda s: (*tile_b(s), 0)), *state_specs],
        out_shape=[jax.ShapeDtypeStruct((bsz, t, 2 * GROUP_W), BF16), *state_shapes],
        scratch_shapes=[
            pltpu.VMEM((nh, HEAD_DIM, HEAD_DIM), F32),
            pltpu.VMEM((nh, 8, HEAD_DIM), F32),
            pltpu.VMEM((n_seq, 8, GATE_LANES), F32),
            pltpu.VMEM((nh, HEAD_DIM, HEAD_DIM), F32),
            pltpu.VMEM((n_seq, HIST_ROW + tile, cw), F32),
            pltpu.VMEM((nh, tile, HEAD_DIM), BF16),
            pltpu.VMEM((nh, tile, HEAD_DIM), F32),
            pltpu.VMEM((nh, tile, HEAD_DIM), F32),
            pltpu.VMEM((ns, tile, GATE_LANES), F32),
            pltpu.VMEM((ns, tile, GATE_LANES), F32),
            pltpu.VMEM((ns, n_ch, HEAD_DIM), F32),
            pltpu.VMEM((ns, tile, GROUP_W), BF16),
            pltpu.VMEM((ns, tile, GROUP_W), BF16),
            pltpu.VMEM((ns, tile, GROUP_W), BF16),
            pltpu.VMEM((ns * n_ch, HEAD_DIM, HEAD_DIM), F32),
            pltpu.VMEM((ns * n_ch, CHUNK, HEAD_DIM), F32),
            pltpu.VMEM((ns * n_ch, HEAD_DIM, HEAD_DIM), F32),
            pltpu.VMEM((ns * n_ch, HEAD_DIM + CHUNK, HEAD_DIM), BF16),
            pltpu.VMEM((ns * n_ch, CHUNK, HEAD_DIM), F32),
        ],
        compiler_params=_params(("arbitrary",)),
        name="mixers",
    )(proj, gates, c0, n0, m0, s0, conv0, gate_par, conv_w, a_norm_w, b_norm_w, _spread_matrix())


def _mem_kv_body(mem_ref, nw_ref, wkv_ref, k_ref, v_ref, kb_ref, vb_ref):
    nb, n_mem, d = mem_ref.shape
    dh = d // N_XHEADS
    mn = _rms(mem_ref[...].reshape(nb * n_mem, d), nw_ref[...]).astype(BF16)
    for w0, f_ref, b_ref in ((0, k_ref, kb_ref), (d, v_ref, vb_ref)):
        y = _dot(mn, wkv_ref[:, w0:w0 + d])
        b_ref[...] = y.astype(BF16).reshape(nb, n_mem, d)
        for h in range(N_XHEADS):
            f_ref[:, :, h, :] = y[:, h * dh:(h + 1) * dh].reshape(nb, n_mem, dh)


def _mem_kv(mem, norm_w, wkv):
    bsz, n_mem, d = mem.shape
    nb = max(1, min(bsz, ROW_TILE // n_mem))
    assert bsz % nb == 0
    b3 = lambda i: (i, 0, 0)
    b4 = lambda i: (i, 0, 0, 0)
    cst = lambda i: (0, 0)
    dh = d // N_XHEADS
    return pl.pallas_call(
        _mem_kv_body,
        grid=(bsz // nb,),
        in_specs=[pl.BlockSpec((nb, n_mem, d), b3), pl.BlockSpec((1, d), cst), pl.BlockSpec((d, 2 * d), cst)],
        out_specs=[pl.BlockSpec((nb, n_mem, N_XHEADS, dh), b4)] * 2 + [pl.BlockSpec((nb, n_mem, d), b3)] * 2,
        out_shape=([jax.ShapeDtypeStruct((bsz, n_mem, N_XHEADS, dh), F32)] * 2
                   + [jax.ShapeDtypeStruct((bsz, n_mem, d), BF16)] * 2),
        compiler_params=_params(("arbitrary",)),
        name="mem_kv",
    )(mem, norm_w, wkv)


def _attn_body(h_ref, x_ref, wout_ref, nx_ref, wq_ref, mk_ref, mv_ref, wo_ref, o_ref, *, n_sub):
    nb, tm, d = x_ref.shape
    dh = d // N_XHEADS
    rows = nb * tm
    gr = rows // n_sub
    assert gr % tm == 0 or tm % gr == 0
    per = max(1, gr // tm)
    hcols = [slice(h * dh, (h + 1) * dh) for h in range(N_XHEADS)]

    def heads_of(ref, b):
        return [ref[b, :, c].astype(BF16) for c in hcols]

    def group_rows(ref, g):
        if per > 1 or gr == tm:
            return ref[g * per:(g + 1) * per].reshape(gr, d)
        return ref[(g * gr) // tm, (g * gr) % tm:(g * gr) % tm + gr, :]

    groups = range(n_sub)
    x1 = [group_rows(x_ref, g) + _dot(group_rows(h_ref, g), wout_ref[...]) for g in groups]
    xn = [_rms(x, nx_ref[...]).astype(BF16) for x in x1]
    q = [_dot(x, wq_ref[...]).astype(BF16) for x in xn]
    unit_rows = min(gr, tm)
    units = [(g, slice(u * unit_rows, (u + 1) * unit_rows), (g * gr + u * unit_rows) // tm)
             for g in groups for u in range(gr // unit_rows)]
    kv = {b: (heads_of(mk_ref, b), heads_of(mv_ref, b)) for b in sorted({b for _, _, b in units})}
    s = [[lax.dot_general(q[g][r, c], kh, _NT, preferred_element_type=F32) * (dh ** -0.5)
          for c, kh in zip(hcols, kv[b][0])] for g, r, b in units]
    mx = [[jnp.max(sh, axis=-1, keepdims=True) for sh in si] for si in s]
    e = [[jnp.exp(sh - mh) for sh, mh in zip(si, mi)] for si, mi in zip(s, mx)]
    den = [[jnp.sum(eh, axis=-1, keepdims=True) for eh in ei] for ei in e]
    p = [[(eh / dh_).astype(BF16) for eh, dh_ in zip(ei, di)] for ei, di in zip(e, den)]
    o = [[_dot(ph, vh).astype(BF16) for ph, vh in zip(pi, kv[b][1])] for pi, (_, _, b) in zip(p, units)]
    for g in groups:
        mine = [oi for oi, (ug, _, _) in zip(o, units) if ug == g]
        acc = x1[g]
        for h, c in enumerate(hcols):
            oh = mine[0][h] if len(mine) == 1 else jnp.concatenate([oi[h] for oi in mine], axis=0)
            acc = acc + _dot(oh, wo_ref[c, :])
        if per > 1 or gr == tm:
            o_ref[g * per:(g + 1) * per] = acc.reshape(per, tm, d)
        else:
            o_ref[(g * gr) // tm, (g * gr) % tm:(g * gr) % tm + gr, :] = acc


def _out_proj_attn(hcat, x, w_out, norm_w, wq, mk, mv, wo):
    bsz, t, d = x.shape
    tm = min(ROW_TILE, t)
    assert t % tm == 0
    nb = max(1, min(bsz, ATTN_MIN_ROWS // tm))
    assert bsz % nb == 0
    bj = lambda b, j: (b, j, 0)
    cst = lambda b, j: (0, 0)
    mem_spec = pl.BlockSpec((nb,) + mk.shape[1:], lambda b, j: (b,) + (0,) * (mk.ndim - 1))
    return pl.pallas_call(
        functools.partial(_attn_body, n_sub=2 if tm >= 256 else 1),
        grid=(bsz // nb, t // tm),
        in_specs=[
            pl.BlockSpec((nb, tm, d), bj),
            pl.BlockSpec((nb, tm, d), bj),
            pl.BlockSpec((d, d), cst),
            pl.BlockSpec((1, d), cst),
            pl.BlockSpec((d, d), cst),
            mem_spec,
            mem_spec,
            pl.BlockSpec((d, d), cst),
        ],
        out_specs=pl.BlockSpec((nb, tm, d), bj),
        out_shape=jax.ShapeDtypeStruct((bsz, t, d), F32),
        compiler_params=_params(("arbitrary", "arbitrary")),
        name="out_proj_attn",
    )(hcat, x, w_out, norm_w, wq, mk, mv, wo)


def _ffn_body(x_ref, nf_ref, w1_ref, w2_ref, nfin_ref, o_ref, *, hid_tile, final_norm, n_sub):
    tm = x_ref.shape[0]
    subs = [slice(i * (tm // n_sub), (i + 1) * (tm // n_sub)) for i in range(n_sub)]
    acc = [x_ref[r, :] for r in subs]
    xn = [_rms(a, nf_ref[...]).astype(BF16) for a in acc]
    for c0 in range(0, w1_ref.shape[1], hid_tile):
        hdn = [jnp.maximum(_dot(x, w1_ref[:, c0:c0 + hid_tile]), 0.0) for x in xn]
        act = [(h * h).astype(BF16) for h in hdn]
        acc = [a + _dot(h, w2_ref[c0:c0 + hid_tile, :]) for a, h in zip(acc, act)]
    for r, a in zip(subs, acc):
        o_ref[r, :] = _rms(a, nfin_ref[...]) if final_norm else a


def _ffn(x2d, norm_w, w1, w2, final_w, *, final_norm):
    rows, d = x2d.shape
    dff = w1.shape[1]
    tm = min(ROW_TILE, rows)
    assert rows % tm == 0
    row = lambda i: (i, 0)
    cst = lambda i: (0, 0)
    return pl.pallas_call(
        functools.partial(_ffn_body, hid_tile=1024, final_norm=final_norm, n_sub=2 if tm >= 512 else 1),
        grid=(rows // tm,),
        in_specs=[
            pl.BlockSpec((tm, d), row),
            pl.BlockSpec((1, d), cst),
            pl.BlockSpec((d, dff), cst, pipeline_mode=pl.Buffered(1)),
            pl.BlockSpec((dff, d), cst, pipeline_mode=pl.Buffered(1)),
            pl.BlockSpec((1, d), cst),
        ],
        out_specs=pl.BlockSpec((tm, d), row),
        out_shape=jax.ShapeDtypeStruct((rows, d), F32),
        compiler_params=_params(("arbitrary",)),
        name="ffn",
    )(x2d, norm_w, w1, w2, final_w)


def _split_w_in(w_in):
    w = GROUP_W
    g0 = 4 * w
    b0 = g0 + 2 * N_HEADS
    g1 = b0 + 4 * w
    main = jnp.concatenate([w_in[:, :g0], w_in[:, b0:g1]], axis=1).astype(BF16)
    gate = jnp.concatenate([w_in[:, g0:b0], w_in[:, g1:g1 + 2 * N_HEADS]], axis=1)
    return main, gate.T.astype(BF16)


def _gate_params(igate_b, fgate_b, a_log, dt_bias, width):
    zeros = jnp.zeros((N_HEADS,), F32)
    bias = jnp.concatenate([igate_b, fgate_b, dt_bias, zeros]).astype(F32)
    alog = jnp.concatenate([zeros, zeros, a_log.astype(F32), zeros])
    return jnp.broadcast_to(jnp.stack([bias, alog])[:, :, None], (2, 4 * N_HEADS, width))


def _trunk_layer(x, mem_k, mem_v, c0, n0, m0, s0, conv0, *, valid, w_main, w_gate, gate_vecs, lw, final_w,
                 final_norm):
    bsz, t, d = x.shape
    tp = CHUNK if valid < CHUNK else t
    xp = x if tp == t else jnp.pad(x, ((0, 0), (0, tp - t), (0, 0)))
    proj, gates = _in_proj(xp.reshape(bsz * tp, d), lw["norm_mix_w"], w_main, w_gate, tp)
    hcat, c1, n1, m1, s1, conv1 = _mixers(
        proj.reshape(bsz, tp, -1), gates, c0, n0, m0.reshape(bsz, 1, N_HEADS), s0, conv0,
        _gate_params(*gate_vecs, width=min(MIXER_TILE, tp)), lw["gdn_conv_w"], lw["mlstm_norm_w"],
        lw["gdn_norm_w"], valid=valid, tile=MIXER_TILE, n_seq=MIXER_SEQS if tp <= MIXER_TILE else 1)
    x2 = _out_proj_attn(hcat, x, lw["w_out"], lw["norm_x_w"], lw["wq_x"], mem_k, mem_v, lw["wo_x"])
    y = _ffn(x2.reshape(bsz * t, d), lw["norm_ffn_w"], lw["w_ff1"], lw["w_ff2"], final_w,
             final_norm=final_norm)
    return y.reshape(bsz, t, d), c1, n1, m1.reshape(bsz, N_HEADS), s1, conv1


def kernel(x_prompt, x_sample, state_mlstm_C, state_mlstm_n, state_mlstm_m, state_gdn_S, state_gdn_conv, cache_mem_k, cache_mem_v, mem_prompt, norm_mix_w, w_in, mlstm_igate_b, mlstm_fgate_b, mlstm_norm_w, gdn_conv_w, gdn_A_log, gdn_dt_bias, gdn_norm_w, w_out, norm_x_w, norm_mem_w, wq_x, wk_x, wv_x, wo_x, norm_ffn_w, w_ff1, w_ff2, norm_final_w):
    depth = w_in.shape[0]
    bp = x_prompt.shape[0]
    bs, ts = x_sample.shape[:2]
    n_mem = mem_prompt.shape[1]
    row = lambda a: a.reshape(1, -1).astype(F32)
    hp, hs = x_prompt, x_sample
    outs_p = [[] for _ in range(7)]
    outs_s = [[] for _ in range(5)]
    for l in range(depth):
        w_main, w_gate = _split_w_in(w_in[l])
        gate_vecs = (mlstm_igate_b[l], mlstm_fgate_b[l], gdn_A_log[l], gdn_dt_bias[l])
        lw = dict(
            norm_mix_w=row(norm_mix_w[l]), gdn_conv_w=gdn_conv_w[l].astype(F32),
            mlstm_norm_w=row(mlstm_norm_w[l]), gdn_norm_w=row(gdn_norm_w[l]),
            w_out=w_out[l].astype(BF16), norm_x_w=row(norm_x_w[l]), wq_x=wq_x[l].astype(BF16),
            wo_x=wo_x[l].astype(BF16), norm_ffn_w=row(norm_ffn_w[l]),
            w_ff1=w_ff1[l].astype(BF16), w_ff2=w_ff2[l].astype(BF16))
        common = dict(w_main=w_main, w_gate=w_gate, gate_vecs=gate_vecs, lw=lw, final_w=row(norm_final_w),
                      final_norm=(l == depth - 1))
        wkv = jnp.concatenate([wk_x[l], wv_x[l]], axis=1).astype(BF16)
        mk, mv, mk_b, mv_b = _mem_kv(mem_prompt, row(norm_mem_w[l]), wkv)
        hp, c1, n1, m1, s1, cv1 = _trunk_layer(
            hp, mk_b, mv_b,
            jnp.zeros((bp, N_HEADS, HEAD_DIM, HEAD_DIM), F32), jnp.zeros((bp, N_HEADS, HEAD_DIM), F32),
            jnp.zeros((bp, N_HEADS), F32), jnp.zeros((bp, N_HEADS, HEAD_DIM, HEAD_DIM), F32),
            jnp.zeros((bp, CONV_TAPS - 1, 3 * GROUP_W), F32), valid=CHUNK, **common)
        for acc, val in zip(outs_p, (c1, n1, m1, s1, cv1, mk, mv)):
            acc.append(val)
        hs, c2, n2, m2, s2, cv2 = _trunk_layer(
            hs, cache_mem_k[l].reshape(bs, n_mem, -1), cache_mem_v[l].reshape(bs, n_mem, -1),
            state_mlstm_C[l], state_mlstm_n[l], state_mlstm_m[l], state_gdn_S[l], state_gdn_conv[l],
            valid=ts, **common)
        for acc, val in zip(outs_s, (c2, n2, m2, s2, cv2)):
            acc.append(val)
    return (hp, hs, *[jnp.stack(a) for a in outs_p], *[jnp.stack(a) for a in outs_s])
```

```python
import functools
import math

import jax
import jax.numpy as jnp
from jax import lax
from jax.experimental import pallas as pl
from jax.experimental.pallas import tpu as pltpu

F32 = jnp.float32
BF16 = jnp.bfloat16
EPS = 1e-6
N_HEADS = 4
HEAD_DIM = 128
GROUP_W = N_HEADS * HEAD_DIM
CONV_TAPS = 4
N_XHEADS = 4
CHUNK = 64
GATE_LANES = 128
NEG_BIG = -1e30
HIST_ROW = 8

VMEM_LIMIT_BYTES = 56 * 1024 * 1024
ROW_TILE = 1024
MIXER_TILE = 512
MIXER_SEQS = 4
ATTN_MIN_ROWS = 64

_NT = (((1,), (1,)), ((), ()))
_TN = (((0,), (0,)), ((), ()))


def _rms(x, w):
    return x * lax.rsqrt(jnp.mean(x * x, axis=-1, keepdims=True) + EPS) * w


def _dot(a, b):
    return jnp.dot(a, b, preferred_element_type=F32)


def _sigmoid(x):
    return 1.0 / (1.0 + jnp.exp(-x))


def _params(sem):
    return pltpu.CompilerParams(dimension_semantics=sem, vmem_limit_bytes=VMEM_LIMIT_BYTES)


def _in_proj_body(x_ref, nw_ref, w_ref, wg_ref, proj_ref, gate_ref, *, col_tile, n_sub):
    assert col_tile == GROUP_W
    tm = x_ref.shape[0]
    subs = [slice(i * (tm // n_sub), (i + 1) * (tm // n_sub)) for i in range(n_sub)]
    xn = [_rms(x_ref[r, :], nw_ref[...]).astype(BF16) for r in subs]
    for c0 in range(0, w_ref.shape[1], col_tile):
        for r, x in zip(subs, xn):
            y = _dot(x, w_ref[:, c0:c0 + col_tile])
            if c0 == 3 * GROUP_W:
                y = _sigmoid(y)
            elif c0 == 7 * GROUP_W:
                y = y * _sigmoid(y)
            proj_ref[r, c0:c0 + col_tile] = y.astype(BF16)
    gates = lax.dot_general(wg_ref[...], jnp.concatenate(xn, axis=0), _NT, preferred_element_type=F32)
    gate_ref[...] = gates.reshape(gate_ref.shape)


def _in_proj(x2d, norm_w, w_main, w_gate, seq_len):
    rows, d = x2d.shape
    n = w_main.shape[1]
    n_g = w_gate.shape[0]
    tm = min(ROW_TILE, rows)
    assert rows % tm == 0 and rows % seq_len == 0
    bsz = rows // seq_len
    direct = seq_len % tm == 0
    if direct:
        gate_spec = pl.BlockSpec((1, n_g, tm), lambda i: (i // (seq_len // tm), 0, i % (seq_len // tm)))
        gate_shape = jax.ShapeDtypeStruct((bsz, n_g, seq_len), F32)
    else:
        gate_spec = pl.BlockSpec((n_g, tm), lambda i: (0, i))
        gate_shape = jax.ShapeDtypeStruct((n_g, rows), F32)
    proj, gates = pl.pallas_call(
        functools.partial(_in_proj_body, col_tile=512, n_sub=2 if tm >= 512 else 1),
        grid=(rows // tm,),
        in_specs=[
            pl.BlockSpec((tm, d), lambda i: (i, 0)),
            pl.BlockSpec((1, d), lambda i: (0, 0)),
            pl.BlockSpec((d, n), lambda i: (0, 0)),
            pl.BlockSpec((n_g, d), lambda i: (0, 0)),
        ],
        out_specs=[pl.BlockSpec((tm, n), lambda i: (i, 0)), gate_spec],
        out_shape=[jax.ShapeDtypeStruct((rows, n), BF16), gate_shape],
        compiler_params=_params(("arbitrary",)),
        name="in_proj",
    )(x2d, norm_w, w_main, w_gate)
    if not direct:
        gates = gates.reshape(n_g, bsz, seq_len).transpose(1, 0, 2)
    return proj, gates


def _mixer_body(proj_ref, gate_ref, c0_ref, n0_ref, m0_ref, s0_ref, conv0_ref,
                gpar_ref, convw_ref, anw_ref, bnw_ref, spread_ref,
                h_ref, c1_ref, n1_ref, m1_ref, s1_ref, conv1_ref,
                c_sc, n_sc, m_sc, s_sc, u_sc, qn_sc, kf_sc, v_sc,
                z_sc, st_sc, dn_sc, q_sc, og_sc, zg_sc, dc_sc, sv_sc, nc_sc, lb_sc, qu_sc,
                *, n_seq, n_chunks, n_tiles, valid, pipelined):
    L = CHUNK
    T = n_chunks * L
    n_ch = n_chunks * N_HEADS
    slots = 2 if pipelined else 1
    step = pl.program_id(0)
    scale = HEAD_DIM ** -0.5
    heads = range(N_HEADS)
    chunks = range(n_chunks)
    seqs = range(n_seq)
    if pipelined:
        j_a = lax.rem(jnp.minimum(step, pl.num_programs(0) - 2), n_tiles)
        j_b = lax.rem(step + (n_tiles - 1), n_tiles)
        has_b = step >= 1
        slot_a = step & 1
        slot_b = 1 - slot_a
    else:
        j_a = j_b = slot_a = slot_b = 0
        has_b = True
    handoff = (z_sc, st_sc, dn_sc, q_sc, og_sc, zg_sc, dc_sc, sv_sc, nc_sc, lb_sc, qu_sc)

    def load_state():
        c_sc[...] = c0_ref[...].reshape(c_sc.shape)
        s_sc[...] = s0_ref[...].reshape(s_sc.shape)
        m_sc[...] = jnp.zeros(m_sc.shape, F32)
        for s in seqs:
            for h in heads:
                n_sc[s * N_HEADS + h, 0:1, :] = n0_ref[s, h:h + 1, :]
            m_sc[s, 0:1, N_HEADS:2 * N_HEADS] = m0_ref[s]

    def load_conv_history():
        u_sc[:, 0:HIST_ROW - (CONV_TAPS - 1), :] = jnp.zeros((n_seq, HIST_ROW - (CONV_TAPS - 1), u_sc.shape[2]), F32)
        u_sc[:, HIST_ROW - (CONV_TAPS - 1):HIST_ROW, :] = conv0_ref[...]

    if pipelined:
        @pl.when(step == 0)
        def _zero_scratch():
            for ref in handoff + (c_sc, n_sc, m_sc, s_sc):
                ref[...] = jnp.zeros(ref.shape, ref.dtype)

        pl.when(j_a == 0)(load_conv_history)
        pl.when(has_b & (j_b == 0))(load_state)
    else:
        load_conv_history()
        load_state()

    lane = lax.broadcasted_iota(jnp.int32, (L, 2 * L), 1)
    tok = lax.broadcasted_iota(jnp.int32, (L, 2 * L), 0)
    src = lane & (L - 1)
    left = lane < L
    tri_incl = tok >= src
    tri_strict = tok > src
    eye_left = jnp.where(left & (tok == src), 1.0, 0.0)
    lane_row = lax.broadcasted_iota(jnp.int32, (1, GATE_LANES), 1)
    head_lanes = (lane_row >= N_HEADS) & (lane_row < 2 * N_HEADS)
    base = 4 * GROUP_W
    ch = [(c, h) for c in chunks for h in heads]
    idx = {p: i for i, p in enumerate(ch)}

    def two(x):
        return jnp.concatenate([x, x], axis=0)

    def rows(c):
        return slice(c * L, (c + 1) * L)

    def hcols(h, group=0):
        return slice(group * GROUP_W + h * HEAD_DIM, group * GROUP_W + (h + 1) * HEAD_DIM)

    ones_sq = jnp.full((HEAD_DIM, HEAD_DIM), 1.0, BF16)

    def highest(a, b, dims=None):
        if dims is None:
            return jnp.dot(a, b, precision=lax.Precision.HIGHEST, preferred_element_type=F32)
        return lax.dot_general(a, b, dims, precision=lax.Precision.HIGHEST, preferred_element_type=F32)

    def phase_a(s):
        sa = s * slots + slot_a
        hd = s * N_HEADS
        n_g = 4 * N_HEADS
        row_g = lax.broadcasted_iota(jnp.int32, (n_g, T), 0)
        pre = gate_ref[s] + gpar_ref[0]
        e = jnp.exp(-jnp.abs(pre))
        t = jnp.log1p(e)
        act = jnp.where(row_g < 4, pre,
                        jnp.where(row_g < 8, -(jnp.maximum(-pre, 0.0) + t),
                                  jnp.where(row_g < 12, -jnp.exp(gpar_ref[1]) * (jnp.maximum(pre, 0.0) + t),
                                            jnp.where(pre >= 0.0, 1.0, e) / (1.0 + e))))
        if valid < L:
            tok_g = lax.broadcasted_iota(jnp.int32, (n_g, T), 1)
            act = jnp.where(tok_g >= valid, jnp.where(row_g < 4, NEG_BIG, 0.0), act)
        r_t = lax.broadcasted_iota(jnp.int32, (T, T), 0)
        c_t = lax.broadcasted_iota(jnp.int32, (T, T), 1)
        shift = int(math.log2(L))
        within = jnp.where((r_t <= c_t) & ((r_t >> shift) == (c_t >> shift)), 1.0, 0.0)
        z_g = jnp.where((row_g >= 4) & (row_g < 12), highest(act, within), act)
        pick = jnp.where(lax.broadcasted_iota(jnp.int32, (n_g, GATE_LANES), 0)
                         == lax.broadcasted_iota(jnp.int32, (n_g, GATE_LANES), 1), 1.0, 0.0)
        z_t = highest(z_g, pick, _TN)
        z_sc[sa] = z_t
        z = [z_t[rows(c), :] for c in chunks]

        def spread(x):
            hi = x.astype(BF16)
            lo = (x - hi.astype(F32)).astype(BF16)
            return lax.dot_general(jnp.concatenate([hi, lo], axis=1), spread_ref[...], _NT,
                                   preferred_element_type=F32)

        zb = [spread(z[c]) for c in chunks]

        def zcol(c, k):
            return zb[c][:, k * GATE_LANES:(k + 1) * GATE_LANES]
        r_d = lax.broadcasted_iota(jnp.int32, (T, 2 * L), 0)
        c_d = lax.broadcasted_iota(jnp.int32, (T, 2 * L), 1)
        zt = [highest(z_g, jnp.where(r_d == c * L + (c_d & (L - 1)), 1.0, 0.0)) for c in chunks]
        q_sc[sa] = proj_ref[s, :, 0:GROUP_W]
        og_sc[sa] = proj_ref[s, :, 3 * GROUP_W:4 * GROUP_W]
        zg_sc[sa] = proj_ref[s, :, base + 3 * GROUP_W:base + 4 * GROUP_W]
        yield

        u_sc[s, HIST_ROW:HIST_ROW + T, :] = proj_ref[s, :, base:base + 3 * GROUP_W].astype(F32)
        for blk in range(3 * N_HEADS):
            cols = slice(blk * HEAD_DIM, (blk + 1) * HEAD_DIM)
            u_all = u_sc[s, :, cols]
            y = convw_ref[0:1, cols] * u_all
            for tap in range(1, CONV_TAPS):
                y = pltpu.roll(y, 1, axis=0) + convw_ref[tap:tap + 1, cols] * u_all
            y = y[HIST_ROW:, :]
            y = y * _sigmoid(y)
            kind, h = divmod(blk, N_HEADS)
            if kind == 0:
                qn_sc[hd + h] = (y * lax.rsqrt(jnp.sum(y * y, axis=-1, keepdims=True) + EPS) * scale).astype(BF16)
            elif kind == 1:
                kf_sc[hd + h] = y * lax.rsqrt(jnp.sum(y * y, axis=-1, keepdims=True) + EPS)
            else:
                v_sc[hd + h] = y
        t_valid = T if valid == L else valid
        u_sc[s, HIST_ROW - 3:HIST_ROW, :] = u_sc[s, HIST_ROW + t_valid - 3:HIST_ROW + t_valid, :]
        yield

        def a_k(c, h):
            return proj_ref[s, rows(c), hcols(h, 1)]

        qk_a = {(c, h): lax.dot_general(proj_ref[s, rows(c), hcols(h)], two(a_k(c, h)), _NT,
                                        preferred_element_type=F32) for c, h in ch}
        qk_b = {}
        for c, h in ch:
            kn = kf_sc[hd + h, rows(c), :].astype(BF16)
            qk_b[c, h] = lax.dot_general(jnp.concatenate([qn_sc[hd + h, rows(c), :], kn], axis=0), two(kn), _NT,
                                         preferred_element_type=F32)
        yield

        d_a, m_a, s_a, kw_a = {}, {}, {}, {}
        for c, h in ch:
            ig_row, b_row = zt[c][h:h + 1, :], zt[c][4 + h:5 + h, :]
            d_a[c, h] = jnp.where(tri_incl, zcol(c, 4 + h) - b_row + ig_row, NEG_BIG)
        for c, h in ch:
            m_a[c, h] = jnp.max(d_a[c, h], axis=1, keepdims=True)
            st_sc[sa, rows(c), 4 + h:5 + h] = m_a[c, h]
        yield
        for c, h in ch:
            s_a[c, h] = jnp.where(left, qk_a[c, h] * scale * jnp.exp(d_a[c, h] - m_a[c, h]), 0.0)
        for c, h in ch:
            ig_b, b_b = zcol(c, h), zcol(c, 4 + h)
            w_s = jnp.exp(b_b[L - 1:L, :] - b_b + ig_b - m_a[c, h][L - 1:L, :])
            kw = a_k(c, h).astype(F32) * scale * w_s
            dn_sc[sa, idx[c, h]:idx[c, h] + 1, :] = jnp.sum(kw, axis=0, keepdims=True)
            kw_a[c, h] = kw.astype(BF16)
        yield
        for c, h in ch:
            v = proj_ref[s, rows(c), hcols(h, 2)]
            sv1 = _dot(s_a[c, h].astype(BF16), jnp.concatenate([two(v), ones_sq], axis=1))
            sv_sc[sa * n_ch + idx[c, h]] = sv1[:, :HEAD_DIM]
            st_sc[sa, rows(c), h:h + 1] = sv1[:, HEAD_DIM:HEAD_DIM + 1]
            dc_sc[sa * n_ch + idx[c, h]] = lax.dot_general(kw_a[c, h], v, _TN, preferred_element_type=F32)
        yield

        decay, r_b = {}, {}
        for c, h in ch:
            g_row = zt[c][8 + h:9 + h, :]
            decay[c, h] = jnp.exp(jnp.where(tri_incl, zcol(c, 8 + h) - g_row, NEG_BIG))
            a2 = jnp.where(tri_strict, zcol(c, 12 + h) * decay[c, h] * qk_b[c, h][L:, :], 0.0)
            r_b[c, h] = jnp.where(left, eye_left, -a2)
        yield
        for _ in range(int(math.log2(L))):
            for p in ch:
                r = r_b[p]
                out = _dot(jnp.where(left, 0.0, r).astype(BF16), two(r.astype(BF16)))
                r_b[p] = jnp.where(left, r + out, out)
            yield
        uk = {}
        for c, h in ch:
            beta = zcol(c, 12 + h)
            rhs = jnp.concatenate([beta * v_sc[hd + h, rows(c), :],
                                   beta * jnp.exp(zcol(c, 8 + h)) * kf_sc[hd + h, rows(c), :]], axis=1).astype(BF16)
            uk[c, h] = _dot(jnp.where(left, r_b[c, h], 0.0).astype(BF16), two(rhs)).astype(BF16)
        yield
        for c, h in ch:
            g_col = zcol(c, 8 + h)
            g_last = g_col[L - 1:L, :]
            qkd = jnp.where(left, qk_b[c, h][:L, :] * decay[c, h], 0.0).astype(BF16)
            qu = _dot(qkd, two(uk[c, h]))
            kd = (kf_sc[hd + h, rows(c), :] * jnp.exp(g_last - g_col)).astype(BF16)
            ku = lax.dot_general(kd, uk[c, h], _TN, preferred_element_type=F32)
            q_eff = jnp.exp(g_col) * qn_sc[hd + h, rows(c), :].astype(F32) - qu[:, HEAD_DIM:]
            i = sa * n_ch + idx[c, h]
            qu_sc[i] = qu[:, :HEAD_DIM]
            nc_sc[i] = ku[:, :HEAD_DIM]
            lb_sc[i] = jnp.concatenate([ku[:, HEAD_DIM:], q_eff], axis=0).astype(BF16)
        yield

    def phase_b(s):
        sb = s * slots + slot_b
        hd = s * N_HEADS
        for c in chunks:
            zc = z_sc[sb, rows(c), :]
            stc = st_sc[sb, rows(c), :]
            base_i = sb * n_ch + c * N_HEADS
            c_prev = [c_sc[hd + h] for h in heads]
            s_prev = [s_sc[hd + h] for h in heads]
            n_prev = [n_sc[hd + h, 0:1, :] for h in heads]
            qc = [_dot(q_sc[sb, rows(c), hcols(h)], c_prev[h].astype(BF16)) for h in heads]
            rm = [_dot(lb_sc[base_i + h], s_prev[h].astype(BF16)) for h in heads]
            m_prev = m_sc[s, 0:1, :]
            inter = zc + m_prev
            m = jnp.maximum(inter, stc)
            m_last = jnp.where(head_lanes, m[L - 1:L, :], 0.0)
            decay0_t = jnp.exp(zc[L - 1:L, :] + m_prev - m_last)
            f_new_t = jnp.exp(stc[L - 1:L, :] - m_last)
            m_sc[s, 0:1, :] = m_last
            for h in heads:
                hl = slice(N_HEADS + h, N_HEADS + h + 1)
                c_sc[hd + h] = decay0_t[:, hl] * c_prev[h] + f_new_t[:, hl] * dc_sc[base_i + h]
                n_sc[hd + h, 0:1, :] = (decay0_t[:, hl] * n_prev[h]
                                        + f_new_t[:, hl] * dn_sc[sb, c * N_HEADS + h:c * N_HEADS + h + 1, :])
            for h in heads:
                s_sc[hd + h] = (jnp.exp(zc[L - 1:L, 8 + h:9 + h]) * s_prev[h] + nc_sc[base_i + h]
                                - rm[h][:HEAD_DIM, :])
            yield
            w_inter_t = jnp.exp(inter - m)
            w_intra_t = jnp.exp(stc - m)
            inv_cap_t = jnp.exp(-m)
            qn = [jnp.sum(q_sc[sb, rows(c), hcols(h)].astype(F32) * n_prev[h], axis=-1, keepdims=True)
                  for h in heads]
            num, cap, o_b = [], [], []
            for h in heads:
                hl = slice(N_HEADS + h, N_HEADS + h + 1)
                w_inter, w_intra = w_inter_t[:, hl], w_intra_t[:, hl]
                num.append(w_inter * qc[h] + w_intra * sv_sc[base_i + h])
                den = w_inter * qn[h] + w_intra * stc[:, h:h + 1]
                cap.append(jnp.maximum(jnp.abs(den), inv_cap_t[:, hl]))
                o_b.append(rm[h][HEAD_DIM:, :] + qu_sc[base_i + h])
            ms_a = [jnp.mean(x * x, axis=-1, keepdims=True) for x in num]
            ms_b = [jnp.mean(x * x, axis=-1, keepdims=True) for x in o_b]
            for h in heads:
                hn = num[h] * lax.rsqrt(ms_a[h] + EPS * cap[h] * cap[h]) * anw_ref[...]
                h_ref[s, rows(c), hcols(h)] = (og_sc[sb, rows(c), hcols(h)].astype(F32) * hn).astype(BF16)
            for h in heads:
                on = o_b[h] * lax.rsqrt(ms_b[h] + EPS) * bnw_ref[...]
                h_ref[s, rows(c), hcols(h, 1)] = (on * zg_sc[sb, rows(c), hcols(h)].astype(F32)).astype(BF16)
            yield

    steps_a, steps_b = 9 + int(math.log2(L)), 2 * n_chunks
    gens_a = [phase_a(s) for s in seqs]
    gens_b = [phase_b(s) for s in seqs]

    def advance(gens):
        for g in gens:
            next(g)

    if pipelined:
        done_b = 0
        for k in range(steps_a):
            while done_b < steps_b and done_b * steps_a <= k * steps_b:
                advance(gens_b)
                done_b += 1
            advance(gens_a)
        for _ in range(steps_b - done_b):
            advance(gens_b)
    else:
        for _ in range(steps_a):
            advance(gens_a)
        for _ in range(steps_b):
            advance(gens_b)
    assert all(next(g, None) is None for g in gens_a + gens_b)

    def store_state():
        c1_ref[...] = c_sc[...].reshape(c1_ref.shape)
        s1_ref[...] = s_sc[...].reshape(s1_ref.shape)
        for s in seqs:
            for h in heads:
                n1_ref[s, h:h + 1, :] = n_sc[s * N_HEADS + h, 0:1, :]
            m1_ref[s] = m_sc[s, 0:1, N_HEADS:2 * N_HEADS]

    def store_conv_history():
        conv1_ref[...] = u_sc[:, HIST_ROW - 3:HIST_ROW, :]

    if pipelined:
        pl.when(has_b & (j_b == n_tiles - 1))(store_state)
        pl.when(j_a == n_tiles - 1)(store_conv_history)
    else:
        store_state()
        store_conv_history()


def _spread_matrix():
    n = lax.broadcasted_iota(jnp.int32, (4 * N_HEADS * GATE_LANES, 2 * GATE_LANES), 0)
    k = lax.broadcasted_iota(jnp.int32, (4 * N_HEADS * GATE_LANES, 2 * GATE_LANES), 1)
    return ((k % GATE_LANES) == (n // GATE_LANES)).astype(BF16)


def _mixers(proj, gates, c0, n0, m0, s0, conv0, gate_par, conv_w, a_norm_w, b_norm_w, *, valid, tile, n_seq):
    bsz, t, n = proj.shape
    tile = min(tile, t)
    assert t % tile == 0 and tile % CHUNK == 0 and bsz % n_seq == 0
    assert valid == CHUNK or (t == CHUNK and CONV_TAPS - 1 <= valid < CHUNK)
    n_chunks = tile // CHUNK
    n_tiles = t // tile
    pipelined = n_tiles > 1
    slots = 2 if pipelined else 1
    n_ch = n_chunks * N_HEADS
    cw = 3 * GROUP_W
    n_groups = bsz // n_seq
    total = n_groups * n_tiles
    if pipelined:
        pair_a = lambda s: jnp.minimum(s, total - 1)
        pair_b = lambda s: jnp.maximum(s - 1, 0)
    else:
        pair_a = pair_b = lambda s: s
    tile_a = lambda s: (pair_a(s) // n_tiles, pair_a(s) % n_tiles)
    tile_b = lambda s: (pair_b(s) // n_tiles, pair_b(s) % n_tiles)
    st4 = lambda s: (tile_b(s)[0], 0, 0, 0)
    st3 = lambda s: (tile_b(s)[0], 0, 0)
    cst2 = lambda s: (0, 0)
    cst3 = lambda s: (0, 0, 0)
    conv_spec = pl.BlockSpec((n_seq, CONV_TAPS - 1, cw), lambda s: (tile_a(s)[0], 0, 0))
    state_specs = [
        pl.BlockSpec((n_seq, N_HEADS, HEAD_DIM, HEAD_DIM), st4),
        pl.BlockSpec((n_seq, N_HEADS, HEAD_DIM), st3),
        pl.BlockSpec((n_seq, 1, N_HEADS), st3),
        pl.BlockSpec((n_seq, N_HEADS, HEAD_DIM, HEAD_DIM), st4),
        conv_spec,
    ]
    state_shapes = [
        jax.ShapeDtypeStruct((bsz, N_HEADS, HEAD_DIM, HEAD_DIM), F32),
        jax.ShapeDtypeStruct((bsz, N_HEADS, HEAD_DIM), F32),
        jax.ShapeDtypeStruct((bsz, 1, N_HEADS), F32),
        jax.ShapeDtypeStruct((bsz, N_HEADS, HEAD_DIM, HEAD_DIM), F32),
        jax.ShapeDtypeStruct((bsz, CONV_TAPS - 1, cw), F32),
    ]
    ns, nh = n_seq * slots, n_seq * N_HEADS
    return pl.pallas_call(
        functools.partial(_mixer_body, n_seq=n_seq, n_chunks=n_chunks, n_tiles=n_tiles, valid=valid,
                          pipelined=pipelined),
        grid=(total + (1 if pipelined else 0),),
        in_specs=[
            pl.BlockSpec((n_seq, tile, n), lambda s: (*tile_a(s), 0)),
            pl.BlockSpec((n_seq, 4 * N_HEADS, tile), lambda s: (tile_a(s)[0], 0, tile_a(s)[1])),
            *state_specs,
            pl.BlockSpec((2, 4 * N_HEADS, tile), cst3),
            pl.BlockSpec((CONV_TAPS, cw), cst2),
            pl.BlockSpec((1, HEAD_DIM), cst2),
            pl.BlockSpec((1, HEAD_DIM), cst2),
            pl.BlockSpec((4 * N_HEADS * GATE_LANES, 2 * GATE_LANES), cst2),
        ],
        out_specs=[pl.BlockSpec((n_seq, tile, 2 * GROUP_W), lambda s: (*tile_b(s), 0)), *state_specs],
        out_shape=[jax.ShapeDtypeStruct((bsz, t, 2 * GROUP_W), BF16), *state_shapes],
        scratch_shapes=[
            pltpu.VMEM((nh, HEAD_DIM, HEAD_DIM), F32),
            pltpu.VMEM((nh, 8, HEAD_DIM), F32),
            pltpu.VMEM((n_seq, 8, GATE_LANES), F32),
            pltpu.VMEM((nh, HEAD_DIM, HEAD_DIM), F32),
            pltpu.VMEM((n_seq, HIST_ROW + tile, cw), F32),
            pltpu.VMEM((nh, tile, HEAD_DIM), BF16),
            pltpu.VMEM((nh, tile, HEAD_DIM), F32),
            pltpu.VMEM((nh, tile, HEAD_DIM), F32),
            pltpu.VMEM((ns, tile, GATE_LANES), F32),
            pltpu.VMEM((ns, tile, GATE_LANES), F32),
            pltpu.VMEM((ns, n_ch, HEAD_DIM), F32),
            pltpu.VMEM((ns, tile, GROUP_W), BF16),
            pltpu.VMEM((ns, tile, GROUP_W), BF16),
            pltpu.VMEM((ns, tile, GROUP_W), BF16),
            pltpu.VMEM((ns * n_ch, HEAD_DIM, HEAD_DIM), F32),
            pltpu.VMEM((ns * n_ch, CHUNK, HEAD_DIM), F32),
            pltpu.VMEM((ns * n_ch, HEAD_DIM, HEAD_DIM), F32),
            pltpu.VMEM((ns * n_ch, HEAD_DIM + CHUNK, HEAD_DIM), BF16),
            pltpu.VMEM((ns * n_ch, CHUNK, HEAD_DIM), F32),
        ],
        compiler_params=_params(("arbitrary",)),
        name="mixers",
    )(proj, gates, c0, n0, m0, s0, conv0, gate_par, conv_w, a_norm_w, b_norm_w, _spread_matrix())


def _mem_kv_body(mem_ref, nw_ref, wkv_ref, k_ref, v_ref, kb_ref, vb_ref):
    nb, n_mem, d = mem_ref.shape
    dh = d // N_XHEADS
    mn = _rms(mem_ref[...].reshape(nb * n_mem, d), nw_ref[...]).astype(BF16)
    for w0, f_ref, b_ref in ((0, k_ref, kb_ref), (d, v_ref, vb_ref)):
        y = _dot(mn, wkv_ref[:, w0:w0 + d])
        b_ref[...] = y.astype(BF16).reshape(nb, n_mem, d)
        for h in range(N_XHEADS):
            f_ref[:, :, h, :] = y[:, h * dh:(h + 1) * dh].reshape(nb, n_mem, dh)


def _mem_kv(mem, norm_w, wkv):
    bsz, n_mem, d = mem.shape
    nb = max(1, min(bsz, ROW_TILE // n_mem))
    assert bsz % nb == 0
    b3 = lambda i: (i, 0, 0)
    b4 = lambda i: (i, 0, 0, 0)
    cst = lambda i: (0, 0)
    dh = d // N_XHEADS
    return pl.pallas_call(
        _mem_kv_body,
        grid=(bsz // nb,),
        in_specs=[pl.BlockSpec((nb, n_mem, d), b3), pl.BlockSpec((1, d), cst), pl.BlockSpec((d, 2 * d), cst)],
        out_specs=[pl.BlockSpec((nb, n_mem, N_XHEADS, dh), b4)] * 2 + [pl.BlockSpec((nb, n_mem, d), b3)] * 2,
        out_shape=([jax.ShapeDtypeStruct((bsz, n_mem, N_XHEADS, dh), F32)] * 2
                   + [jax.ShapeDtypeStruct((bsz, n_mem, d), BF16)] * 2),
        compiler_params=_params(("arbitrary",)),
        name="mem_kv",
    )(mem, norm_w, wkv)


def _attn_body(h_ref, x_ref, wout_ref, nx_ref, wq_ref, mk_ref, mv_ref, wo_ref, o_ref, *kv_scratch, n_sub):
    nb, tm, d = x_ref.shape
    dh = d // N_XHEADS
    rows = nb * tm
    gr = rows // n_sub
    assert gr % tm == 0 or tm % gr == 0
    per = max(1, gr // tm)
    hcols = [slice(h * dh, (h + 1) * dh) for h in range(N_XHEADS)]

    if kv_scratch:
        k_buf, v_buf, sems = kv_scratch
        first = pl.program_id(0) * nb

        def kv_copy(which, src, dst, b, h):
            return pltpu.make_async_copy(src.at[first + b, :, h, :], dst.at[b, h], sems.at[which, b, h])

        copies = [kv_copy(w, src, dst, b, h) for w, (src, dst) in enumerate(((mk_ref, k_buf), (mv_ref, v_buf)))
                  for b in range(nb) for h in range(N_XHEADS)]
        for cp in copies:
            cp.start()

    def heads_of(which, b):
        if kv_scratch:
            return [(k_buf, v_buf)[which][b, h].astype(BF16) for h in range(N_XHEADS)]
        return [(mk_ref, mv_ref)[which][b, :, c] for c in hcols]

    def group_rows(ref, g):
        if per > 1 or gr == tm:
            return ref[g * per:(g + 1) * per].reshape(gr, d)
        return ref[(g * gr) // tm, (g * gr) % tm:(g * gr) % tm + gr, :]

    groups = range(n_sub)
    x1 = [group_rows(x_ref, g) + _dot(group_rows(h_ref, g), wout_ref[...]) for g in groups]
    xn = [_rms(x, nx_ref[...]).astype(BF16) for x in x1]
    q = [_dot(x, wq_ref[...]).astype(BF16) for x in xn]
    unit_rows = min(gr, tm)
    units = [(g, slice(u * unit_rows, (u + 1) * unit_rows), (g * gr + u * unit_rows) // tm)
             for g in groups for u in range(gr // unit_rows)]
    if kv_scratch:
        for cp in copies:
            cp.wait()
    kv = {b: (heads_of(0, b), heads_of(1, b)) for b in sorted({b for _, _, b in units})}
    s = [[lax.dot_general(q[g][r, c], kh, _NT, preferred_element_type=F32) * (dh ** -0.5)
          for c, kh in zip(hcols, kv[b][0])] for g, r, b in units]
    mx = [[jnp.max(sh, axis=-1, keepdims=True) for sh in si] for si in s]
    e = [[jnp.exp(sh - mh) for sh, mh in zip(si, mi)] for si, mi in zip(s, mx)]
    den = [[jnp.sum(eh, axis=-1, keepdims=True) for eh in ei] for ei in e]
    p = [[(eh / dh_).astype(BF16) for eh, dh_ in zip(ei, di)] for ei, di in zip(e, den)]
    o = [[_dot(ph, vh).astype(BF16) for ph, vh in zip(pi, kv[b][1])] for pi, (_, _, b) in zip(p, units)]
    for g in groups:
        mine = [oi for oi, (ug, _, _) in zip(o, units) if ug == g]
        acc = x1[g]
        for h, c in enumerate(hcols):
            oh = mine[0][h] if len(mine) == 1 else jnp.concatenate([oi[h] for oi in mine], axis=0)
            acc = acc + _dot(oh, wo_ref[c, :])
        if per > 1 or gr == tm:
            o_ref[g * per:(g + 1) * per] = acc.reshape(per, tm, d)
        else:
            o_ref[(g * gr) // tm, (g * gr) % tm:(g * gr) % tm + gr, :] = acc


def _out_proj_attn(hcat, x, w_out, norm_w, wq, mk, mv, wo):
    bsz, t, d = x.shape
    tm = min(ROW_TILE, t)
    assert t % tm == 0
    nb = max(1, min(bsz, ATTN_MIN_ROWS // tm))
    assert bsz % nb == 0
    bj = lambda b, j: (b, j, 0)
    cst = lambda b, j: (0, 0)
    gather = mk.ndim == 4
    if gather:
        assert t == tm
        mem_spec = pl.BlockSpec(memory_space=pl.ANY)
        n_mem, n_h, dh = mk.shape[1:]
        scratch = [pltpu.VMEM((nb, n_h, n_mem, dh), mk.dtype), pltpu.VMEM((nb, n_h, n_mem, dh), mv.dtype),
                   pltpu.SemaphoreType.DMA((2, nb, n_h))]
    else:
        mem_spec = pl.BlockSpec((nb,) + mk.shape[1:], lambda b, j: (b, 0, 0))
        scratch = []
    return pl.pallas_call(
        functools.partial(_attn_body, n_sub=2 if tm >= 256 else 1),
        grid=(bsz // nb, t // tm),
        in_specs=[
            pl.BlockSpec((nb, tm, d), bj),
            pl.BlockSpec((nb, tm, d), bj),
            pl.BlockSpec((d, d), cst),
            pl.BlockSpec((1, d), cst),
            pl.BlockSpec((d, d), cst),
            mem_spec,
            mem_spec,
            pl.BlockSpec((d, d), cst),
        ],
        out_specs=pl.BlockSpec((nb, tm, d), bj),
        out_shape=jax.ShapeDtypeStruct((bsz, t, d), F32),
        scratch_shapes=scratch,
        compiler_params=_params(("arbitrary", "arbitrary")),
        name="out_proj_attn",
    )(hcat, x, w_out, norm_w, wq, mk, mv, wo)


def _ffn_body(x_ref, nf_ref, w1_ref, w2_ref, nfin_ref, o_ref, *, hid_tile, final_norm, n_sub):
    tm = x_ref.shape[0]
    subs = [slice(i * (tm // n_sub), (i + 1) * (tm // n_sub)) for i in range(n_sub)]
    acc = [x_ref[r, :] for r in subs]
    xn = [_rms(a, nf_ref[...]).astype(BF16) for a in acc]
    for c0 in range(0, w1_ref.shape[1], hid_tile):
        hdn = [jnp.maximum(_dot(x, w1_ref[:, c0:c0 + hid_tile]), 0.0) for x in xn]
        act = [(h * h).astype(BF16) for h in hdn]
        acc = [a + _dot(h, w2_ref[c0:c0 + hid_tile, :]) for a, h in zip(acc, act)]
    for r, a in zip(subs, acc):
        o_ref[r, :] = _rms(a, nfin_ref[...]) if final_norm else a


def _ffn(x2d, norm_w, w1, w2, final_w, *, final_norm):
    rows, d = x2d.shape
    dff = w1.shape[1]
    tm = min(ROW_TILE, rows)
    assert rows % tm == 0
    row = lambda i: (i, 0)
    cst = lambda i: (0, 0)
    return pl.pallas_call(
        functools.partial(_ffn_body, hid_tile=1024, final_norm=final_norm, n_sub=2 if tm >= 512 else 1),
        grid=(rows // tm,),
        in_specs=[
            pl.BlockSpec((tm, d), row),
            pl.BlockSpec((1, d), cst),
            pl.BlockSpec((d, dff), cst, pipeline_mode=pl.Buffered(1)),
            pl.BlockSpec((dff, d), cst, pipeline_mode=pl.Buffered(1)),
            pl.BlockSpec((1, d), cst),
        ],
        out_specs=pl.BlockSpec((tm, d), row),
        out_shape=jax.ShapeDtypeStruct((rows, d), F32),
        compiler_params=_params(("arbitrary",)),
        name="ffn",
    )(x2d, norm_w, w1, w2, final_w)


def _split_w_in(w_in):
    w = GROUP_W
    g0 = 4 * w
    b0 = g0 + 2 * N_HEADS
    g1 = b0 + 4 * w
    main = jnp.concatenate([w_in[:, :g0], w_in[:, b0:g1]], axis=1).astype(BF16)
    gate = jnp.concatenate([w_in[:, g0:b0], w_in[:, g1:g1 + 2 * N_HEADS]], axis=1)
    return main, gate.T.astype(BF16)


def _gate_params(igate_b, fgate_b, a_log, dt_bias, width):
    zeros = jnp.zeros((N_HEADS,), F32)
    bias = jnp.concatenate([igate_b, fgate_b, dt_bias, zeros]).astype(F32)
    alog = jnp.concatenate([zeros, zeros, a_log.astype(F32), zeros])
    return jnp.broadcast_to(jnp.stack([bias, alog])[:, :, None], (2, 4 * N_HEADS, width))


def _trunk_layer(x, mem_k, mem_v, c0, n0, m0, s0, conv0, *, valid, w_main, w_gate, gate_vecs, lw, final_w,
                 final_norm):
    bsz, t, d = x.shape
    tp = CHUNK if valid < CHUNK else t
    xp = x if tp == t else jnp.pad(x, ((0, 0), (0, tp - t), (0, 0)))
    proj, gates = _in_proj(xp.reshape(bsz * tp, d), lw["norm_mix_w"], w_main, w_gate, tp)
    hcat, c1, n1, m1, s1, conv1 = _mixers(
        proj.reshape(bsz, tp, -1), gates, c0, n0, m0.reshape(bsz, 1, N_HEADS), s0, conv0,
        _gate_params(*gate_vecs, width=min(MIXER_TILE, tp)), lw["gdn_conv_w"], lw["mlstm_norm_w"],
        lw["gdn_norm_w"], valid=valid, tile=MIXER_TILE, n_seq=MIXER_SEQS if tp <= MIXER_TILE else 1)
    x2 = _out_proj_attn(hcat, x, lw["w_out"], lw["norm_x_w"], lw["wq_x"], mem_k, mem_v, lw["wo_x"])
    y = _ffn(x2.reshape(bsz * t, d), lw["norm_ffn_w"], lw["w_ff1"], lw["w_ff2"], final_w,
             final_norm=final_norm)
    return y.reshape(bsz, t, d), c1, n1, m1.reshape(bsz, N_HEADS), s1, conv1


def kernel(x_prompt, x_sample, state_mlstm_C, state_mlstm_n, state_mlstm_m, state_gdn_S, state_gdn_conv, cache_mem_k, cache_mem_v, mem_prompt, norm_mix_w, w_in, mlstm_igate_b, mlstm_fgate_b, mlstm_norm_w, gdn_conv_w, gdn_A_log, gdn_dt_bias, gdn_norm_w, w_out, norm_x_w, norm_mem_w, wq_x, wk_x, wv_x, wo_x, norm_ffn_w, w_ff1, w_ff2, norm_final_w):
    depth = w_in.shape[0]
    bp = x_prompt.shape[0]
    ts = x_sample.shape[1]
    row = lambda a: a.reshape(1, -1).astype(F32)
    hp, hs = x_prompt, x_sample
    outs_p = [[] for _ in range(7)]
    outs_s = [[] for _ in range(5)]
    for l in range(depth):
        w_main, w_gate = _split_w_in(w_in[l])
        gate_vecs = (mlstm_igate_b[l], mlstm_fgate_b[l], gdn_A_log[l], gdn_dt_bias[l])
        lw = dict(
            norm_mix_w=row(norm_mix_w[l]), gdn_conv_w=gdn_conv_w[l].astype(F32),
            mlstm_norm_w=row(mlstm_norm_w[l]), gdn_norm_w=row(gdn_norm_w[l]),
            w_out=w_out[l].astype(BF16), norm_x_w=row(norm_x_w[l]), wq_x=wq_x[l].astype(BF16),
            wo_x=wo_x[l].astype(BF16), norm_ffn_w=row(norm_ffn_w[l]),
            w_ff1=w_ff1[l].astype(BF16), w_ff2=w_ff2[l].astype(BF16))
        common = dict(w_main=w_main, w_gate=w_gate, gate_vecs=gate_vecs, lw=lw, final_w=row(norm_final_w),
                      final_norm=(l == depth - 1))
        wkv = jnp.concatenate([wk_x[l], wv_x[l]], axis=1).astype(BF16)
        mk, mv, mk_b, mv_b = _mem_kv(mem_prompt, row(norm_mem_w[l]), wkv)
        hp, c1, n1, m1, s1, cv1 = _trunk_layer(
            hp, mk_b, mv_b,
            jnp.zeros((bp, N_HEADS, HEAD_DIM, HEAD_DIM), F32), jnp.zeros((bp, N_HEADS, HEAD_DIM), F32),
            jnp.zeros((bp, N_HEADS), F32), jnp.zeros((bp, N_HEADS, HEAD_DIM, HEAD_DIM), F32),
            jnp.zeros((bp, CONV_TAPS - 1, 3 * GROUP_W), F32), valid=CHUNK, **common)
        for acc, val in zip(outs_p, (c1, n1, m1, s1, cv1, mk, mv)):
            acc.append(val)
        hs, c2, n2, m2, s2, cv2 = _trunk_layer(
            hs, cache_mem_k[l], cache_mem_v[l],
            state_mlstm_C[l], state_mlstm_n[l], state_mlstm_m[l], state_gdn_S[l], state_gdn_conv[l],
            valid=ts, **common)
        for acc, val in zip(outs_s, (c2, n2, m2, s2, cv2)):
            acc.append(val)
    return (hp, hs, *[jnp.stack(a) for a in outs_p], *[jnp.stack(a) for a in outs_s])
```

```python
import functools
import math

import jax
import jax.numpy as jnp
from jax import lax
from jax.experimental import pallas as pl
from jax.experimental.pallas import tpu as pltpu

F32 = jnp.float32
BF16 = jnp.bfloat16
EPS = 1e-6
N_HEADS = 4
HEAD_DIM = 128
GROUP_W = N_HEADS * HEAD_DIM
CONV_TAPS = 4
N_XHEADS = 4
CHUNK = 64
GATE_LANES = 128
NEG_BIG = -1e30
HIST_ROW = 8

VMEM_LIMIT_BYTES = 56 * 1024 * 1024
ROW_TILE = 1024
MIXER_TILE = 512
MIXER_SEQS = 4
ATTN_MIN_ROWS = 128

_NT = (((1,), (1,)), ((), ()))
_TN = (((0,), (0,)), ((), ()))


def _rms(x, w):
    return x * lax.rsqrt(jnp.mean(x * x, axis=-1, keepdims=True) + EPS) * w


def _dot(a, b):
    return jnp.dot(a, b, preferred_element_type=F32)


def _sigmoid(x):
    return 1.0 / (1.0 + jnp.exp(-x))


def _params(sem):
    return pltpu.CompilerParams(dimension_semantics=sem, vmem_limit_bytes=VMEM_LIMIT_BYTES)


def _in_proj_body(x_ref, nw_ref, w_ref, wg_ref, proj_ref, gate_ref, *, col_tile, n_sub):
    assert col_tile == GROUP_W
    tm = x_ref.shape[0]
    subs = [slice(i * (tm // n_sub), (i + 1) * (tm // n_sub)) for i in range(n_sub)]
    xn = [_rms(x_ref[r, :], nw_ref[...]).astype(BF16) for r in subs]
    for c0 in range(0, w_ref.shape[1], col_tile):
        for r, x in zip(subs, xn):
            y = _dot(x, w_ref[:, c0:c0 + col_tile])
            if c0 == 3 * GROUP_W:
                y = _sigmoid(y)
            elif c0 == 7 * GROUP_W:
                y = y * _sigmoid(y)
            proj_ref[r, c0:c0 + col_tile] = y.astype(BF16)
    gates = lax.dot_general(wg_ref[...], jnp.concatenate(xn, axis=0), _NT, preferred_element_type=F32)
    gate_ref[...] = gates.reshape(gate_ref.shape)


def _in_proj(x2d, norm_w, w_main, w_gate, seq_len):
    rows, d = x2d.shape
    n = w_main.shape[1]
    n_g = w_gate.shape[0]
    tm = min(ROW_TILE, rows)
    assert rows % tm == 0 and rows % seq_len == 0
    bsz = rows // seq_len
    direct = seq_len % tm == 0
    if direct:
        gate_spec = pl.BlockSpec((1, n_g, tm), lambda i: (i // (seq_len // tm), 0, i % (seq_len // tm)))
        gate_shape = jax.ShapeDtypeStruct((bsz, n_g, seq_len), F32)
    else:
        gate_spec = pl.BlockSpec((n_g, tm), lambda i: (0, i))
        gate_shape = jax.ShapeDtypeStruct((n_g, rows), F32)
    proj, gates = pl.pallas_call(
        functools.partial(_in_proj_body, col_tile=512, n_sub=2 if tm >= 512 else 1),
        grid=(rows // tm,),
        in_specs=[
            pl.BlockSpec((tm, d), lambda i: (i, 0)),
            pl.BlockSpec((1, d), lambda i: (0, 0)),
            pl.BlockSpec((d, n), lambda i: (0, 0)),
            pl.BlockSpec((n_g, d), lambda i: (0, 0)),
        ],
        out_specs=[pl.BlockSpec((tm, n), lambda i: (i, 0)), gate_spec],
        out_shape=[jax.ShapeDtypeStruct((rows, n), BF16), gate_shape],
        compiler_params=_params(("arbitrary",)),
        name="in_proj",
    )(x2d, norm_w, w_main, w_gate)
    if not direct:
        gates = gates.reshape(n_g, bsz, seq_len).transpose(1, 0, 2)
    return proj, gates


def _mixer_body(proj_ref, gate_ref, c0_ref, n0_ref, m0_ref, s0_ref, conv0_ref,
                gpar_ref, convw_ref, anw_ref, bnw_ref, spread_ref,
                h_ref, c1_ref, n1_ref, m1_ref, s1_ref, conv1_ref,
                c_sc, n_sc, m_sc, s_sc, u_sc, qn_sc, kf_sc, v_sc,
                z_sc, st_sc, dn_sc, q_sc, og_sc, zg_sc, dc_sc, sv_sc, nc_sc, lb_sc, qu_sc,
                *, n_seq, n_chunks, n_tiles, valid, pipelined):
    L = CHUNK
    T = n_chunks * L
    n_ch = n_chunks * N_HEADS
    slots = 2 if pipelined else 1
    step = pl.program_id(0)
    scale = HEAD_DIM ** -0.5
    heads = range(N_HEADS)
    chunks = range(n_chunks)
    seqs = range(n_seq)
    if pipelined:
        j_a = lax.rem(jnp.minimum(step, pl.num_programs(0) - 2), n_tiles)
        j_b = lax.rem(step + (n_tiles - 1), n_tiles)
        has_b = step >= 1
        slot_a = step & 1
        slot_b = 1 - slot_a
    else:
        j_a = j_b = slot_a = slot_b = 0
        has_b = True
    handoff = (z_sc, st_sc, dn_sc, q_sc, og_sc, zg_sc, dc_sc, sv_sc, nc_sc, lb_sc, qu_sc)

    def load_state():
        c_sc[...] = c0_ref[...].reshape(c_sc.shape)
        s_sc[...] = s0_ref[...].reshape(s_sc.shape)
        m_sc[...] = jnp.zeros(m_sc.shape, F32)
        for s in seqs:
            for h in heads:
                n_sc[s * N_HEADS + h, 0:1, :] = n0_ref[s, h:h + 1, :]
            m_sc[s, 0:1, N_HEADS:2 * N_HEADS] = m0_ref[s]

    def load_conv_history():
        u_sc[:, 0:HIST_ROW - (CONV_TAPS - 1), :] = jnp.zeros((n_seq, HIST_ROW - (CONV_TAPS - 1), u_sc.shape[2]), F32)
        u_sc[:, HIST_ROW - (CONV_TAPS - 1):HIST_ROW, :] = conv0_ref[...]

    if pipelined:
        @pl.when(step == 0)
        def _zero_scratch():
            for ref in handoff + (c_sc, n_sc, m_sc, s_sc):
                ref[...] = jnp.zeros(ref.shape, ref.dtype)

        pl.when(j_a == 0)(load_conv_history)
        pl.when(has_b & (j_b == 0))(load_state)
    else:
        load_conv_history()
        load_state()

    lane = lax.broadcasted_iota(jnp.int32, (L, 2 * L), 1)
    tok = lax.broadcasted_iota(jnp.int32, (L, 2 * L), 0)
    src = lane & (L - 1)
    left = lane < L
    tri_incl = tok >= src
    tri_strict = tok > src
    eye_left = jnp.where(left & (tok == src), 1.0, 0.0)
    lane_row = lax.broadcasted_iota(jnp.int32, (1, GATE_LANES), 1)
    head_lanes = (lane_row >= N_HEADS) & (lane_row < 2 * N_HEADS)
    base = 4 * GROUP_W
    ch = [(c, h) for c in chunks for h in heads]
    idx = {p: i for i, p in enumerate(ch)}

    def two(x):
        return jnp.concatenate([x, x], axis=0)

    def rows(c):
        return slice(c * L, (c + 1) * L)

    def hcols(h, group=0):
        return slice(group * GROUP_W + h * HEAD_DIM, group * GROUP_W + (h + 1) * HEAD_DIM)

    ones_sq = jnp.full((HEAD_DIM, HEAD_DIM), 1.0, BF16)

    def highest(a, b, dims=None):
        if dims is None:
            return jnp.dot(a, b, precision=lax.Precision.HIGHEST, preferred_element_type=F32)
        return lax.dot_general(a, b, dims, precision=lax.Precision.HIGHEST, preferred_element_type=F32)

    def phase_a(s):
        sa = s * slots + slot_a
        hd = s * N_HEADS
        n_g = 4 * N_HEADS
        row_g = lax.broadcasted_iota(jnp.int32, (n_g, T), 0)
        pre = gate_ref[s] + gpar_ref[0]
        e = jnp.exp(-jnp.abs(pre))
        t = jnp.log1p(e)
        act = jnp.where(row_g < 4, pre,
                        jnp.where(row_g < 8, -(jnp.maximum(-pre, 0.0) + t),
                                  jnp.where(row_g < 12, -jnp.exp(gpar_ref[1]) * (jnp.maximum(pre, 0.0) + t),
                                            jnp.where(pre >= 0.0, 1.0, e) / (1.0 + e))))
        if valid < L:
            tok_g = lax.broadcasted_iota(jnp.int32, (n_g, T), 1)
            act = jnp.where(tok_g >= valid, jnp.where(row_g < 4, NEG_BIG, 0.0), act)
        r_t = lax.broadcasted_iota(jnp.int32, (T, T), 0)
        c_t = lax.broadcasted_iota(jnp.int32, (T, T), 1)
        shift = int(math.log2(L))
        within = jnp.where((r_t <= c_t) & ((r_t >> shift) == (c_t >> shift)), 1.0, 0.0)
        z_g = jnp.where((row_g >= 4) & (row_g < 12), highest(act, within), act)
        pick = jnp.where(lax.broadcasted_iota(jnp.int32, (n_g, GATE_LANES), 0)
                         == lax.broadcasted_iota(jnp.int32, (n_g, GATE_LANES), 1), 1.0, 0.0)
        z_t = highest(z_g, pick, _TN)
        z_sc[sa] = z_t
        z = [z_t[rows(c), :] for c in chunks]

        def spread(x):
            hi = x.astype(BF16)
            lo = (x - hi.astype(F32)).astype(BF16)
            return lax.dot_general(jnp.concatenate([hi, lo], axis=1), spread_ref[...], _NT,
                                   preferred_element_type=F32)

        zb = [spread(z[c]) for c in chunks]

        def zcol(c, k):
            return zb[c][:, k * GATE_LANES:(k + 1) * GATE_LANES]
        r_d = lax.broadcasted_iota(jnp.int32, (T, 2 * L), 0)
        c_d = lax.broadcasted_iota(jnp.int32, (T, 2 * L), 1)
        zt = [highest(z_g, jnp.where(r_d == c * L + (c_d & (L - 1)), 1.0, 0.0)) for c in chunks]
        q_sc[sa] = proj_ref[s, :, 0:GROUP_W]
        og_sc[sa] = proj_ref[s, :, 3 * GROUP_W:4 * GROUP_W]
        zg_sc[sa] = proj_ref[s, :, base + 3 * GROUP_W:base + 4 * GROUP_W]
        yield

        u_sc[s, HIST_ROW:HIST_ROW + T, :] = proj_ref[s, :, base:base + 3 * GROUP_W].astype(F32)
        for blk in range(3 * N_HEADS):
            cols = slice(blk * HEAD_DIM, (blk + 1) * HEAD_DIM)
            u_all = u_sc[s, :, cols]
            y = convw_ref[0:1, cols] * u_all
            for tap in range(1, CONV_TAPS):
                y = pltpu.roll(y, 1, axis=0) + convw_ref[tap:tap + 1, cols] * u_all
            y = y[HIST_ROW:, :]
            y = y * _sigmoid(y)
            kind, h = divmod(blk, N_HEADS)
            if kind == 0:
                qn_sc[hd + h] = (y * lax.rsqrt(jnp.sum(y * y, axis=-1, keepdims=True) + EPS) * scale).astype(BF16)
            elif kind == 1:
                kf_sc[hd + h] = y * lax.rsqrt(jnp.sum(y * y, axis=-1, keepdims=True) + EPS)
            else:
                v_sc[hd + h] = y
        t_valid = T if valid == L else valid
        u_sc[s, HIST_ROW - 3:HIST_ROW, :] = u_sc[s, HIST_ROW + t_valid - 3:HIST_ROW + t_valid, :]
        yield

        def a_k(c, h):
            return proj_ref[s, rows(c), hcols(h, 1)]

        qk_a = {(c, h): lax.dot_general(proj_ref[s, rows(c), hcols(h)], two(a_k(c, h)), _NT,
                                        preferred_element_type=F32) for c, h in ch}
        qk_b = {}
        for c, h in ch:
            kn = kf_sc[hd + h, rows(c), :].astype(BF16)
            qk_b[c, h] = lax.dot_general(jnp.concatenate([qn_sc[hd + h, rows(c), :], kn], axis=0), two(kn), _NT,
                                         preferred_element_type=F32)
        yield

        d_a, m_a, s_a, kw_a = {}, {}, {}, {}
        for c, h in ch:
            ig_row, b_row = zt[c][h:h + 1, :], zt[c][4 + h:5 + h, :]
            d_a[c, h] = jnp.where(tri_incl, zcol(c, 4 + h) - b_row + ig_row, NEG_BIG)
        for c, h in ch:
            m_a[c, h] = jnp.max(d_a[c, h], axis=1, keepdims=True)
            st_sc[sa, rows(c), 4 + h:5 + h] = m_a[c, h]
        yield
        for c, h in ch:
            s_a[c, h] = jnp.where(left, qk_a[c, h] * scale * jnp.exp(d_a[c, h] - m_a[c, h]), 0.0)
        for c, h in ch:
            ig_b, b_b = zcol(c, h), zcol(c, 4 + h)
            w_s = jnp.exp(b_b[L - 1:L, :] - b_b + ig_b - m_a[c, h][L - 1:L, :])
            kw = a_k(c, h).astype(F32) * scale * w_s
            dn_sc[sa, idx[c, h]:idx[c, h] + 1, :] = jnp.sum(kw, axis=0, keepdims=True)
            kw_a[c, h] = kw.astype(BF16)
        yield
        for c, h in ch:
            v = proj_ref[s, rows(c), hcols(h, 2)]
            sv1 = _dot(s_a[c, h].astype(BF16), jnp.concatenate([two(v), ones_sq], axis=1))
            sv_sc[sa * n_ch + idx[c, h]] = sv1[:, :HEAD_DIM]
            st_sc[sa, rows(c), h:h + 1] = sv1[:, HEAD_DIM:HEAD_DIM + 1]
            dc_sc[sa * n_ch + idx[c, h]] = lax.dot_general(kw_a[c, h], v, _TN, preferred_element_type=F32)
        yield

        decay, r_b = {}, {}
        for c, h in ch:
            g_row = zt[c][8 + h:9 + h, :]
            decay[c, h] = jnp.exp(jnp.where(tri_incl, zcol(c, 8 + h) - g_row, NEG_BIG))
            a2 = jnp.where(tri_strict, zcol(c, 12 + h) * decay[c, h] * qk_b[c, h][L:, :], 0.0)
            r_b[c, h] = jnp.where(left, eye_left, -a2)
        yield
        for _ in range(int(math.log2(L))):
            for p in ch:
                r = r_b[p]
                out = _dot(jnp.where(left, 0.0, r).astype(BF16), two(r.astype(BF16)))
                r_b[p] = jnp.where(left, r + out, out)
            yield
        uk = {}
        for c, h in ch:
            beta = zcol(c, 12 + h)
            rhs = jnp.concatenate([beta * v_sc[hd + h, rows(c), :],
                                   beta * jnp.exp(zcol(c, 8 + h)) * kf_sc[hd + h, rows(c), :]], axis=1).astype(BF16)
            uk[c, h] = _dot(jnp.where(left, r_b[c, h], 0.0).astype(BF16), two(rhs)).astype(BF16)
        yield
        for c, h in ch:
            g_col = zcol(c, 8 + h)
            g_last = g_col[L - 1:L, :]
            qkd = jnp.where(left, qk_b[c, h][:L, :] * decay[c, h], 0.0).astype(BF16)
            qu = _dot(qkd, two(uk[c, h]))
            kd = (kf_sc[hd + h, rows(c), :] * jnp.exp(g_last - g_col)).astype(BF16)
            ku = lax.dot_general(kd, uk[c, h], _TN, preferred_element_type=F32)
            q_eff = jnp.exp(g_col) * qn_sc[hd + h, rows(c), :].astype(F32) - qu[:, HEAD_DIM:]
            i = sa * n_ch + idx[c, h]
            qu_sc[i] = qu[:, :HEAD_DIM]
            nc_sc[i] = ku[:, :HEAD_DIM]
            lb_sc[i] = jnp.concatenate([ku[:, HEAD_DIM:], q_eff], axis=0).astype(BF16)
        yield

    def phase_b(s):
        sb = s * slots + slot_b
        hd = s * N_HEADS
        for c in chunks:
            zc = z_sc[sb, rows(c), :]
            stc = st_sc[sb, rows(c), :]
            base_i = sb * n_ch + c * N_HEADS
            c_prev = [c_sc[hd + h] for h in heads]
            s_prev = [s_sc[hd + h] for h in heads]
            n_prev = [n_sc[hd + h, 0:1, :] for h in heads]
            qc = [_dot(q_sc[sb, rows(c), hcols(h)], c_prev[h].astype(BF16)) for h in heads]
            rm = [_dot(lb_sc[base_i + h], s_prev[h].astype(BF16)) for h in heads]
            m_prev = m_sc[s, 0:1, :]
            inter = zc + m_prev
            m = jnp.maximum(inter, stc)
            m_last = jnp.where(head_lanes, m[L - 1:L, :], 0.0)
            decay0_t = jnp.exp(zc[L - 1:L, :] + m_prev - m_last)
            f_new_t = jnp.exp(stc[L - 1:L, :] - m_last)
            m_sc[s, 0:1, :] = m_last
            for h in heads:
                hl = slice(N_HEADS + h, N_HEADS + h + 1)
                c_sc[hd + h] = decay0_t[:, hl] * c_prev[h] + f_new_t[:, hl] * dc_sc[base_i + h]
                n_sc[hd + h, 0:1, :] = (decay0_t[:, hl] * n_prev[h]
                                        + f_new_t[:, hl] * dn_sc[sb, c * N_HEADS + h:c * N_HEADS + h + 1, :])
            for h in heads:
                s_sc[hd + h] = (jnp.exp(zc[L - 1:L, 8 + h:9 + h]) * s_prev[h] + nc_sc[base_i + h]
                                - rm[h][:HEAD_DIM, :])
            yield
            w_inter_t = jnp.exp(inter - m)
            w_intra_t = jnp.exp(stc - m)
            inv_cap_t = jnp.exp(-m)
            qn = [jnp.sum(q_sc[sb, rows(c), hcols(h)].astype(F32) * n_prev[h], axis=-1, keepdims=True)
                  for h in heads]
            num, cap, o_b = [], [], []
            for h in heads:
                hl = slice(N_HEADS + h, N_HEADS + h + 1)
                w_inter, w_intra = w_inter_t[:, hl], w_intra_t[:, hl]
                num.append(w_inter * qc[h] + w_intra * sv_sc[base_i + h])
                den = w_inter * qn[h] + w_intra * stc[:, h:h + 1]
                cap.append(jnp.maximum(jnp.abs(den), inv_cap_t[:, hl]))
                o_b.append(rm[h][HEAD_DIM:, :] + qu_sc[base_i + h])
            ms_a = [jnp.mean(x * x, axis=-1, keepdims=True) for x in num]
            ms_b = [jnp.mean(x * x, axis=-1, keepdims=True) for x in o_b]
            for h in heads:
                hn = num[h] * lax.rsqrt(ms_a[h] + EPS * cap[h] * cap[h]) * anw_ref[...]
                h_ref[s, rows(c), hcols(h)] = (og_sc[sb, rows(c), hcols(h)].astype(F32) * hn).astype(BF16)
            for h in heads:
                on = o_b[h] * lax.rsqrt(ms_b[h] + EPS) * bnw_ref[...]
                h_ref[s, rows(c), hcols(h, 1)] = (on * zg_sc[sb, rows(c), hcols(h)].astype(F32)).astype(BF16)
            yield

    steps_a, steps_b = 9 + int(math.log2(L)), 2 * n_chunks
    gens_a = [phase_a(s) for s in seqs]
    gens_b = [phase_b(s) for s in seqs]

    def advance(gens):
        for g in gens:
            next(g)

    if pipelined:
        done_b = 0
        for k in range(steps_a):
            while done_b < steps_b and done_b * steps_a <= k * steps_b:
                advance(gens_b)
                done_b += 1
            advance(gens_a)
        for _ in range(steps_b - done_b):
            advance(gens_b)
    else:
        for _ in range(steps_a):
            advance(gens_a)
        for _ in range(steps_b):
            advance(gens_b)
    assert all(next(g, None) is None for g in gens_a + gens_b)

    def store_state():
        c1_ref[...] = c_sc[...].reshape(c1_ref.shape)
        s1_ref[...] = s_sc[...].reshape(s1_ref.shape)
        for s in seqs:
            for h in heads:
                n1_ref[s, h:h + 1, :] = n_sc[s * N_HEADS + h, 0:1, :]
            m1_ref[s] = m_sc[s, 0:1, N_HEADS:2 * N_HEADS]

    def store_conv_history():
        conv1_ref[...] = u_sc[:, HIST_ROW - 3:HIST_ROW, :]

    if pipelined:
        pl.when(has_b & (j_b == n_tiles - 1))(store_state)
        pl.when(j_a == n_tiles - 1)(store_conv_history)
    else:
        store_state()
        store_conv_history()


def _spread_matrix():
    n = lax.broadcasted_iota(jnp.int32, (4 * N_HEADS * GATE_LANES, 2 * GATE_LANES), 0)
    k = lax.broadcasted_iota(jnp.int32, (4 * N_HEADS * GATE_LANES, 2 * GATE_LANES), 1)
    return ((k % GATE_LANES) == (n // GATE_LANES)).astype(BF16)


def _mixers(proj, gates, c0, n0, m0, s0, conv0, gate_par, conv_w, a_norm_w, b_norm_w, *, valid, tile, n_seq):
    bsz, t, n = proj.shape
    tile = min(tile, t)
    assert t % tile == 0 and tile % CHUNK == 0 and bsz % n_seq == 0
    assert valid == CHUNK or (t == CHUNK and CONV_TAPS - 1 <= valid < CHUNK)
    n_chunks = tile // CHUNK
    n_tiles = t // tile
    pipelined = n_tiles > 1
    slots = 2 if pipelined else 1
    n_ch = n_chunks * N_HEADS
    cw = 3 * GROUP_W
    n_groups = bsz // n_seq
    total = n_groups * n_tiles
    if pipelined:
        pair_a = lambda s: jnp.minimum(s, total - 1)
        pair_b = lambda s: jnp.maximum(s - 1, 0)
    else:
        pair_a = pair_b = lambda s: s
    tile_a = lambda s: (pair_a(s) // n_tiles, pair_a(s) % n_tiles)
    tile_b = lambda s: (pair_b(s) // n_tiles, pair_b(s) % n_tiles)
    st4 = lambda s: (tile_b(s)[0], 0, 0, 0)
    st3 = lambda s: (tile_b(s)[0], 0, 0)
    cst2 = lambda s: (0, 0)
    cst3 = lambda s: (0, 0, 0)
    conv_spec = pl.BlockSpec((n_seq, CONV_TAPS - 1, cw), lambda s: (tile_a(s)[0], 0, 0))
    state_specs = [
        pl.BlockSpec((n_seq, N_HEADS, HEAD_DIM, HEAD_DIM), st4),
        pl.BlockSpec((n_seq, N_HEADS, HEAD_DIM), st3),
        pl.BlockSpec((n_seq, 1, N_HEADS), st3),
        pl.BlockSpec((n_seq, N_HEADS, HEAD_DIM, HEAD_DIM), st4),
        conv_spec,
    ]
    state_shapes = [
        jax.ShapeDtypeStruct((bsz, N_HEADS, HEAD_DIM, HEAD_DIM), F32),
        jax.ShapeDtypeStruct((bsz, N_HEADS, HEAD_DIM), F32),
        jax.ShapeDtypeStruct((bsz, 1, N_HEADS), F32),
        jax.ShapeDtypeStruct((bsz, N_HEADS, HEAD_DIM, HEAD_DIM), F32),
        jax.ShapeDtypeStruct((bsz, CONV_TAPS - 1, cw), F32),
    ]
    ns, nh = n_seq * slots, n_seq * N_HEADS
    return pl.pallas_call(
        functools.partial(_mixer_body, n_seq=n_seq, n_chunks=n_chunks, n_tiles=n_tiles, valid=valid,
                          pipelined=pipelined),
        grid=(total + (1 if pipelined else 0),),
        in_specs=[
            pl.BlockSpec((n_seq, tile, n), lambda s: (*tile_a(s), 0)),
            pl.BlockSpec((n_seq, 4 * N_HEADS, tile), lambda s: (tile_a(s)[0], 0, tile_a(s)[1])),
            *state_specs,
            pl.BlockSpec((2, 4 * N_HEADS, tile), cst3),
            pl.BlockSpec((CONV_TAPS, cw), cst2),
            pl.BlockSpec((1, HEAD_DIM), cst2),
            pl.BlockSpec((1, HEAD_DIM), cst2),
            pl.BlockSpec((4 * N_HEADS * GATE_LANES, 2 * GATE_LANES), cst2),
        ],
        out_specs=[pl.BlockSpec((n_seq, tile, 2 * GROUP_W), lambda s: (*tile_b(s), 0)), *state_specs],
        out_shape=[jax.ShapeDtypeStruct((bsz, t, 2 * GROUP_W), BF16), *state_shapes],
        scratch_shapes=[
            pltpu.VMEM((nh, HEAD_DIM, HEAD_DIM), F32),
            pltpu.VMEM((nh, 8, HEAD_DIM), F32),
            pltpu.VMEM((n_seq, 8, GATE_LANES), F32),
            pltpu.VMEM((nh, HEAD_DIM, HEAD_DIM), F32),
            pltpu.VMEM((n_seq, HIST_ROW + tile, cw), F32),
            pltpu.VMEM((nh, tile, HEAD_DIM), BF16),
            pltpu.VMEM((nh, tile, HEAD_DIM), F32),
            pltpu.VMEM((nh, tile, HEAD_DIM), F32),
            pltpu.VMEM((ns, tile, GATE_LANES), F32),
            pltpu.VMEM((ns, tile, GATE_LANES), F32),
            pltpu.VMEM((ns, n_ch, HEAD_DIM), F32),
            pltpu.VMEM((ns, tile, GROUP_W), BF16),
            pltpu.VMEM((ns, tile, GROUP_W), BF16),
            pltpu.VMEM((ns, tile, GROUP_W), BF16),
            pltpu.VMEM((ns * n_ch, HEAD_DIM, HEAD_DIM), F32),
            pltpu.VMEM((ns * n_ch, CHUNK, HEAD_DIM), F32),
            pltpu.VMEM((ns * n_ch, HEAD_DIM, HEAD_DIM), F32),
            pltpu.VMEM((ns * n_ch, HEAD_DIM + CHUNK, HEAD_DIM), BF16),
            pltpu.VMEM((ns * n_ch, CHUNK, HEAD_DIM), F32),
        ],
        compiler_params=_params(("arbitrary",)),
        name="mixers",
    )(proj, gates, c0, n0, m0, s0, conv0, gate_par, conv_w, a_norm_w, b_norm_w, _spread_matrix())


def _mem_kv_body(mem_ref, nw_ref, wkv_ref, k_ref, v_ref, kb_ref, vb_ref):
    nb, n_mem, d = mem_ref.shape
    dh = d // N_XHEADS
    mn = _rms(mem_ref[...].reshape(nb * n_mem, d), nw_ref[...]).astype(BF16)
    for w0, f_ref, b_ref in ((0, k_ref, kb_ref), (d, v_ref, vb_ref)):
        y = _dot(mn, wkv_ref[:, w0:w0 + d])
        b_ref[...] = y.astype(BF16).reshape(nb, n_mem, d)
        for h in range(N_XHEADS):
            f_ref[:, :, h, :] = y[:, h * dh:(h + 1) * dh].reshape(nb, n_mem, dh)


def _mem_kv(mem, norm_w, wkv):
    bsz, n_mem, d = mem.shape
    nb = max(1, min(bsz, ROW_TILE // n_mem))
    assert bsz % nb == 0
    b3 = lambda i: (i, 0, 0)
    b4 = lambda i: (i, 0, 0, 0)
    cst = lambda i: (0, 0)
    dh = d // N_XHEADS
    return pl.pallas_call(
        _mem_kv_body,
        grid=(bsz // nb,),
        in_specs=[pl.BlockSpec((nb, n_mem, d), b3), pl.BlockSpec((1, d), cst), pl.BlockSpec((d, 2 * d), cst)],
        out_specs=[pl.BlockSpec((nb, n_mem, N_XHEADS, dh), b4)] * 2 + [pl.BlockSpec((nb, n_mem, d), b3)] * 2,
        out_shape=([jax.ShapeDtypeStruct((bsz, n_mem, N_XHEADS, dh), F32)] * 2
                   + [jax.ShapeDtypeStruct((bsz, n_mem, d), BF16)] * 2),
        compiler_params=_params(("arbitrary",)),
        name="mem_kv",
    )(mem, norm_w, wkv)


def _attn_body(h_ref, x_ref, wout_ref, nx_ref, wq_ref, mk_ref, mv_ref, wo_ref, o_ref, *kv_scratch, n_sub):
    nb, tm, d = x_ref.shape
    dh = d // N_XHEADS
    rows = nb * tm
    gr = rows // n_sub
    assert gr % tm == 0 or tm % gr == 0
    per = max(1, gr // tm)
    hcols = [slice(h * dh, (h + 1) * dh) for h in range(N_XHEADS)]

    if kv_scratch:
        k_buf, v_buf, sems = kv_scratch
        first = pl.program_id(0) * nb

        def kv_copy(which, src, dst, b, h):
            return pltpu.make_async_copy(src.at[first + b, :, h, :], dst.at[b, h], sems.at[which, b, h])

        copies = [kv_copy(w, src, dst, b, h) for w, (src, dst) in enumerate(((mk_ref, k_buf), (mv_ref, v_buf)))
                  for b in range(nb) for h in range(N_XHEADS)]
        for cp in copies:
            cp.start()

    def heads_of(which, b):
        if kv_scratch:
            return [(k_buf, v_buf)[which][b, h].astype(BF16) for h in range(N_XHEADS)]
        return [(mk_ref, mv_ref)[which][b, :, c] for c in hcols]

    def group_rows(ref, g):
        if per > 1 or gr == tm:
            return ref[g * per:(g + 1) * per].reshape(gr, d)
        return ref[(g * gr) // tm, (g * gr) % tm:(g * gr) % tm + gr, :]

    groups = range(n_sub)
    x1 = [group_rows(x_ref, g) + _dot(group_rows(h_ref, g), wout_ref[...]) for g in groups]
    xn = [_rms(x, nx_ref[...]).astype(BF16) for x in x1]
    q = [_dot(x, wq_ref[...]).astype(BF16) for x in xn]
    unit_rows = min(gr, tm)
    units = [(g, slice(u * unit_rows, (u + 1) * unit_rows), (g * gr + u * unit_rows) // tm)
             for g in groups for u in range(gr // unit_rows)]
    if kv_scratch:
        for cp in copies:
            cp.wait()
    kv = {b: (heads_of(0, b), heads_of(1, b)) for b in sorted({b for _, _, b in units})}
    s = [[lax.dot_general(q[g][r, c], kh, _NT, preferred_element_type=F32) * (dh ** -0.5)
          for c, kh in zip(hcols, kv[b][0])] for g, r, b in units]
    mx = [[jnp.max(sh, axis=-1, keepdims=True) for sh in si] for si in s]
    e = [[jnp.exp(sh - mh) for sh, mh in zip(si, mi)] for si, mi in zip(s, mx)]
    den = [[jnp.sum(eh, axis=-1, keepdims=True) for eh in ei] for ei in e]
    p = [[(eh / dh_).astype(BF16) for eh, dh_ in zip(ei, di)] for ei, di in zip(e, den)]
    o = [[_dot(ph, vh).astype(BF16) for ph, vh in zip(pi, kv[b][1])] for pi, (_, _, b) in zip(p, units)]
    for g in groups:
        mine = [oi for oi, (ug, _, _) in zip(o, units) if ug == g]
        acc = x1[g]
        for h, c in enumerate(hcols):
            oh = mine[0][h] if len(mine) == 1 else jnp.concatenate([oi[h] for oi in mine], axis=0)
            acc = acc + _dot(oh, wo_ref[c, :])
        if per > 1 or gr == tm:
            o_ref[g * per:(g + 1) * per] = acc.reshape(per, tm, d)
        else:
            o_ref[(g * gr) // tm, (g * gr) % tm:(g * gr) % tm + gr, :] = acc


def _out_proj_attn(hcat, x, w_out, norm_w, wq, mk, mv, wo):
    bsz, t, d = x.shape
    tm = min(ROW_TILE, t)
    assert t % tm == 0
    nb = max(1, min(bsz, ATTN_MIN_ROWS // tm))
    assert bsz % nb == 0
    bj = lambda b, j: (b, j, 0)
    cst = lambda b, j: (0, 0)
    gather = mk.ndim == 4
    if gather:
        assert t == tm
        mem_spec = pl.BlockSpec(memory_space=pl.ANY)
        n_mem, n_h, dh = mk.shape[1:]
        scratch = [pltpu.VMEM((nb, n_h, n_mem, dh), mk.dtype), pltpu.VMEM((nb, n_h, n_mem, dh), mv.dtype),
                   pltpu.SemaphoreType.DMA((2, nb, n_h))]
    else:
        mem_spec = pl.BlockSpec((nb,) + mk.shape[1:], lambda b, j: (b, 0, 0))
        scratch = []
    return pl.pallas_call(
        functools.partial(_attn_body, n_sub=2 if tm >= 256 else 1),
        grid=(bsz // nb, t // tm),
        in_specs=[
            pl.BlockSpec((nb, tm, d), bj),
            pl.BlockSpec((nb, tm, d), bj),
            pl.BlockSpec((d, d), cst),
            pl.BlockSpec((1, d), cst),
            pl.BlockSpec((d, d), cst),
            mem_spec,
            mem_spec,
            pl.BlockSpec((d, d), cst),
        ],
        out_specs=pl.BlockSpec((nb, tm, d), bj),
        out_shape=jax.ShapeDtypeStruct((bsz, t, d), F32),
        scratch_shapes=scratch,
        compiler_params=_params(("arbitrary", "arbitrary")),
        name="out_proj_attn",
    )(hcat, x, w_out, norm_w, wq, mk, mv, wo)


def _ffn_body(x_ref, nf_ref, w1_ref, w2_ref, nfin_ref, o_ref, *, hid_tile, final_norm, n_sub):
    tm = x_ref.shape[0]
    subs = [slice(i * (tm // n_sub), (i + 1) * (tm // n_sub)) for i in range(n_sub)]
    acc = [x_ref[r, :] for r in subs]
    xn = [_rms(a, nf_ref[...]).astype(BF16) for a in acc]
    for c0 in range(0, w1_ref.shape[1], hid_tile):
        hdn = [jnp.maximum(_dot(x, w1_ref[:, c0:c0 + hid_tile]), 0.0) for x in xn]
        act = [(h * h).astype(BF16) for h in hdn]
        acc = [a + _dot(h, w2_ref[c0:c0 + hid_tile, :]) for a, h in zip(acc, act)]
    for r, a in zip(subs, acc):
        o_ref[r, :] = _rms(a, nfin_ref[...]) if final_norm else a


def _ffn(x2d, norm_w, w1, w2, final_w, *, final_norm):
    rows, d = x2d.shape
    dff = w1.shape[1]
    tm = min(ROW_TILE, rows)
    assert rows % tm == 0
    row = lambda i: (i, 0)
    cst = lambda i: (0, 0)
    return pl.pallas_call(
        functools.partial(_ffn_body, hid_tile=1024, final_norm=final_norm, n_sub=2 if tm >= 512 else 1),
        grid=(rows // tm,),
        in_specs=[
            pl.BlockSpec((tm, d), row),
            pl.BlockSpec((1, d), cst),
            pl.BlockSpec((d, dff), cst, pipeline_mode=pl.Buffered(1)),
            pl.BlockSpec((dff, d), cst, pipeline_mode=pl.Buffered(1)),
            pl.BlockSpec((1, d), cst),
        ],
        out_specs=pl.BlockSpec((tm, d), row),
        out_shape=jax.ShapeDtypeStruct((rows, d), F32),
        compiler_params=_params(("arbitrary",)),
        name="ffn",
    )(x2d, norm_w, w1, w2, final_w)


def _split_w_in(w_in):
    w = GROUP_W
    g0 = 4 * w
    b0 = g0 + 2 * N_HEADS
    g1 = b0 + 4 * w
    main = jnp.concatenate([w_in[:, :g0], w_in[:, b0:g1]], axis=1).astype(BF16)
    gate = jnp.concatenate([w_in[:, g0:b0], w_in[:, g1:g1 + 2 * N_HEADS]], axis=1)
    return main, gate.T.astype(BF16)


def _gate_params(igate_b, fgate_b, a_log, dt_bias, width):
    zeros = jnp.zeros((N_HEADS,), F32)
    bias = jnp.concatenate([igate_b, fgate_b, dt_bias, zeros]).astype(F32)
    alog = jnp.concatenate([zeros, zeros, a_log.astype(F32), zeros])
    return jnp.broadcast_to(jnp.stack([bias, alog])[:, :, None], (2, 4 * N_HEADS, width))


def _trunk_layer(x, mem_k, mem_v, c0, n0, m0, s0, conv0, *, valid, w_main, w_gate, gate_vecs, lw, final_w,
                 final_norm):
    bsz, t, d = x.shape
    tp = CHUNK if valid < CHUNK else t
    xp = x if tp == t else jnp.pad(x, ((0, 0), (0, tp - t), (0, 0)))
    proj, gates = _in_proj(xp.reshape(bsz * tp, d), lw["norm_mix_w"], w_main, w_gate, tp)
    hcat, c1, n1, m1, s1, conv1 = _mixers(
        proj.reshape(bsz, tp, -1), gates, c0, n0, m0.reshape(bsz, 1, N_HEADS), s0, conv0,
        _gate_params(*gate_vecs, width=min(MIXER_TILE, tp)), lw["gdn_conv_w"], lw["mlstm_norm_w"],
        lw["gdn_norm_w"], valid=valid, tile=MIXER_TILE, n_seq=MIXER_SEQS if tp <= MIXER_TILE else 1)
    x2 = _out_proj_attn(hcat, x, lw["w_out"], lw["norm_x_w"], lw["wq_x"], mem_k, mem_v, lw["wo_x"])
    y = _ffn(x2.reshape(bsz * t, d), lw["norm_ffn_w"], lw["w_ff1"], lw["w_ff2"], final_w,
             final_norm=final_norm)
    return y.reshape(bsz, t, d), c1, n1, m1.reshape(bsz, N_HEADS), s1, conv1


def kernel(x_prompt, x_sample, state_mlstm_C, state_mlstm_n, state_mlstm_m, state_gdn_S, state_gdn_conv, cache_mem_k, cache_mem_v, mem_prompt, norm_mix_w, w_in, mlstm_igate_b, mlstm_fgate_b, mlstm_norm_w, gdn_conv_w, gdn_A_log, gdn_dt_bias, gdn_norm_w, w_out, norm_x_w, norm_mem_w, wq_x, wk_x, wv_x, wo_x, norm_ffn_w, w_ff1, w_ff2, norm_final_w):
    depth = w_in.shape[0]
    bp = x_prompt.shape[0]
    ts = x_sample.shape[1]
    row = lambda a: a.reshape(1, -1).astype(F32)
    hp, hs = x_prompt, x_sample
    outs_p = [[] for _ in range(7)]
    outs_s = [[] for _ in range(5)]
    for l in range(depth):
        w_main, w_gate = _split_w_in(w_in[l])
        gate_vecs = (mlstm_igate_b[l], mlstm_fgate_b[l], gdn_A_log[l], gdn_dt_bias[l])
        lw = dict(
            norm_mix_w=row(norm_mix_w[l]), gdn_conv_w=gdn_conv_w[l].astype(F32),
            mlstm_norm_w=row(mlstm_norm_w[l]), gdn_norm_w=row(gdn_norm_w[l]),
            w_out=w_out[l].astype(BF16), norm_x_w=row(norm_x_w[l]), wq_x=wq_x[l].astype(BF16),
            wo_x=wo_x[l].astype(BF16), norm_ffn_w=row(norm_ffn_w[l]),
            w_ff1=w_ff1[l].astype(BF16), w_ff2=w_ff2[l].astype(BF16))
        common = dict(w_main=w_main, w_gate=w_gate, gate_vecs=gate_vecs, lw=lw, final_w=row(norm_final_w),
                      final_norm=(l == depth - 1))
        wkv = jnp.concatenate([wk_x[l], wv_x[l]], axis=1).astype(BF16)
        mk, mv, mk_b, mv_b = _mem_kv(mem_prompt, row(norm_mem_w[l]), wkv)
        hp, c1, n1, m1, s1, cv1 = _trunk_layer(
            hp, mk_b, mv_b,
            jnp.zeros((bp, N_HEADS, HEAD_DIM, HEAD_DIM), F32), jnp.zeros((bp, N_HEADS, HEAD_DIM), F32),
            jnp.zeros((bp, N_HEADS), F32), jnp.zeros((bp, N_HEADS, HEAD_DIM, HEAD_DIM), F32),
            jnp.zeros((bp, CONV_TAPS - 1, 3 * GROUP_W), F32), valid=CHUNK, **common)
        for acc, val in zip(outs_p, (c1, n1, m1, s1, cv1, mk, mv)):
            acc.append(val)
        hs, c2, n2, m2, s2, cv2 = _trunk_layer(
            hs, cache_mem_k[l], cache_mem_v[l],
            state_mlstm_C[l], state_mlstm_n[l], state_mlstm_m[l], state_gdn_S[l], state_gdn_conv[l],
            valid=ts, **common)
        for acc, val in zip(outs_s, (c2, n2, m2, s2, cv2)):
            acc.append(val)
    return (hp, hs, *[jnp.stack(a) for a in outs_p], *[jnp.stack(a) for a in outs_s])
```

```python
import functools
import math

import jax
import jax.numpy as jnp
from jax import lax
from jax.experimental import pallas as pl
from jax.experimental.pallas import tpu as pltpu

F32 = jnp.float32
BF16 = jnp.bfloat16
EPS = 1e-6
N_HEADS = 4
HEAD_DIM = 128
GROUP_W = N_HEADS * HEAD_DIM
CONV_TAPS = 4
N_XHEADS = 4
CHUNK = 64
GATE_LANES = 128
NEG_BIG = -1e30
HIST_ROW = 8

VMEM_LIMIT_BYTES = 56 * 1024 * 1024
ROW_TILE = 1024
MIXER_TILE = 512
MIXER_SEQS = 4
ATTN_MIN_ROWS = 128

_NT = (((1,), (1,)), ((), ()))
_TN = (((0,), (0,)), ((), ()))


def _rms(x, w):
    return x * lax.rsqrt(jnp.mean(x * x, axis=-1, keepdims=True) + EPS) * w


def _dot(a, b):
    return jnp.dot(a, b, preferred_element_type=F32)


def _sigmoid(x):
    return 1.0 / (1.0 + jnp.exp(-x))


def _params(sem):
    return pltpu.CompilerParams(dimension_semantics=sem, vmem_limit_bytes=VMEM_LIMIT_BYTES)


def _in_proj_body(x_ref, nw_ref, w_ref, wg_ref, proj_ref, gate_ref, *, col_tile, n_sub):
    assert col_tile == GROUP_W
    tm = x_ref.shape[0]
    subs = [slice(i * (tm // n_sub), (i + 1) * (tm // n_sub)) for i in range(n_sub)]
    xn = [_rms(x_ref[r, :], nw_ref[...]).astype(BF16) for r in subs]
    for c0 in range(0, w_ref.shape[1], col_tile):
        for r, x in zip(subs, xn):
            y = _dot(x, w_ref[:, c0:c0 + col_tile])
            if c0 == 3 * GROUP_W:
                y = _sigmoid(y)
            elif c0 == 7 * GROUP_W:
                y = y * _sigmoid(y)
            proj_ref[r, c0:c0 + col_tile] = y.astype(BF16)
    gates = lax.dot_general(wg_ref[...], jnp.concatenate(xn, axis=0), _NT, preferred_element_type=F32)
    gate_ref[...] = gates.reshape(gate_ref.shape)


def _in_proj(x2d, norm_w, w_main, w_gate, seq_len):
    rows, d = x2d.shape
    n = w_main.shape[1]
    n_g = w_gate.shape[0]
    tm = min(ROW_TILE, rows)
    assert rows % tm == 0 and rows % seq_len == 0
    bsz = rows // seq_len
    direct = seq_len % tm == 0
    if direct:
        gate_spec = pl.BlockSpec((1, n_g, tm), lambda i: (i // (seq_len // tm), 0, i % (seq_len // tm)))
        gate_shape = jax.ShapeDtypeStruct((bsz, n_g, seq_len), F32)
    else:
        gate_spec = pl.BlockSpec((n_g, tm), lambda i: (0, i))
        gate_shape = jax.ShapeDtypeStruct((n_g, rows), F32)
    proj, gates = pl.pallas_call(
        functools.partial(_in_proj_body, col_tile=512, n_sub=2 if tm >= 512 else 1),
        grid=(rows // tm,),
        in_specs=[
            pl.BlockSpec((tm, d), lambda i: (i, 0)),
            pl.BlockSpec((1, d), lambda i: (0, 0)),
            pl.BlockSpec((d, n), lambda i: (0, 0)),
            pl.BlockSpec((n_g, d), lambda i: (0, 0)),
        ],
        out_specs=[pl.BlockSpec((tm, n), lambda i: (i, 0)), gate_spec],
        out_shape=[jax.ShapeDtypeStruct((rows, n), BF16), gate_shape],
        compiler_params=_params(("arbitrary",)),
        name="in_proj",
    )(x2d, norm_w, w_main, w_gate)
    if not direct:
        gates = gates.reshape(n_g, bsz, seq_len).transpose(1, 0, 2)
    return proj, gates


def _mixer_body(proj_ref, gate_ref, c0_ref, n0_ref, m0_ref, s0_ref, conv0_ref,
                gpar_ref, convw_ref, anw_ref, bnw_ref, spread_ref,
                h_ref, c1_ref, n1_ref, m1_ref, s1_ref, conv1_ref,
                c_sc, n_sc, m_sc, s_sc, u_sc, qn_sc, kf_sc, v_sc,
                z_sc, st_sc, dn_sc, q_sc, og_sc, zg_sc, dc_sc, sv_sc, nc_sc, lb_sc, qu_sc,
                *, n_seq, n_chunks, n_tiles, valid, pipelined):
    L = CHUNK
    T = n_chunks * L
    n_ch = n_chunks * N_HEADS
    slots = 2 if pipelined else 1
    step = pl.program_id(0)
    scale = HEAD_DIM ** -0.5
    heads = range(N_HEADS)
    chunks = range(n_chunks)
    seqs = range(n_seq)
    if pipelined:
        j_a = lax.rem(jnp.minimum(step, pl.num_programs(0) - 2), n_tiles)
        j_b = lax.rem(step + (n_tiles - 1), n_tiles)
        has_b = step >= 1
        slot_a = step & 1
        slot_b = 1 - slot_a
    else:
        j_a = j_b = slot_a = slot_b = 0
        has_b = True
    handoff = (z_sc, st_sc, dn_sc, q_sc, og_sc, zg_sc, dc_sc, sv_sc, nc_sc, lb_sc, qu_sc)

    def load_state():
        c_sc[...] = c0_ref[...].reshape(c_sc.shape)
        s_sc[...] = s0_ref[...].reshape(s_sc.shape)
        m_sc[...] = jnp.zeros(m_sc.shape, F32)
        for s in seqs:
            for h in heads:
                n_sc[s * N_HEADS + h, 0:1, :] = n0_ref[s, h:h + 1, :]
            m_sc[s, 0:1, N_HEADS:2 * N_HEADS] = m0_ref[s]

    def load_conv_history():
        u_sc[:, 0:HIST_ROW - (CONV_TAPS - 1), :] = jnp.zeros((n_seq, HIST_ROW - (CONV_TAPS - 1), u_sc.shape[2]), F32)
        u_sc[:, HIST_ROW - (CONV_TAPS - 1):HIST_ROW, :] = conv0_ref[...]

    if pipelined:
        @pl.when(step == 0)
        def _zero_scratch():
            for ref in handoff + (c_sc, n_sc, m_sc, s_sc):
                ref[...] = jnp.zeros(ref.shape, ref.dtype)

        pl.when(j_a == 0)(load_conv_history)
        pl.when(has_b & (j_b == 0))(load_state)
    else:
        load_conv_history()
        load_state()

    lane = lax.broadcasted_iota(jnp.int32, (L, 2 * L), 1)
    tok = lax.broadcasted_iota(jnp.int32, (L, 2 * L), 0)
    src = lane & (L - 1)
    left = lane < L
    tri_incl = tok >= src
    tri_strict = tok > src
    eye_left = jnp.where(left & (tok == src), 1.0, 0.0)
    lane_row = lax.broadcasted_iota(jnp.int32, (1, GATE_LANES), 1)
    head_lanes = (lane_row >= N_HEADS) & (lane_row < 2 * N_HEADS)
    base = 4 * GROUP_W
    ch = [(c, h) for c in chunks for h in heads]
    idx = {p: i for i, p in enumerate(ch)}

    def two(x):
        return jnp.concatenate([x, x], axis=0)

    def rows(c):
        return slice(c * L, (c + 1) * L)

    def hcols(h, group=0):
        return slice(group * GROUP_W + h * HEAD_DIM, group * GROUP_W + (h + 1) * HEAD_DIM)

    ones_sq = jnp.full((HEAD_DIM, HEAD_DIM), 1.0, BF16)

    def highest(a, b, dims=None):
        if dims is None:
            return jnp.dot(a, b, precision=lax.Precision.HIGHEST, preferred_element_type=F32)
        return lax.dot_general(a, b, dims, precision=lax.Precision.HIGHEST, preferred_element_type=F32)

    def phase_a(s):
        sa = s * slots + slot_a
        hd = s * N_HEADS
        n_g = 4 * N_HEADS
        row_g = lax.broadcasted_iota(jnp.int32, (n_g, T), 0)
        pre = gate_ref[s] + gpar_ref[0]
        e = jnp.exp(-jnp.abs(pre))
        t = jnp.log1p(e)
        act = jnp.where(row_g < 4, pre,
                        jnp.where(row_g < 8, -(jnp.maximum(-pre, 0.0) + t),
                                  jnp.where(row_g < 12, -jnp.exp(gpar_ref[1]) * (jnp.maximum(pre, 0.0) + t),
                                            jnp.where(pre >= 0.0, 1.0, e) / (1.0 + e))))
        if valid < L:
            tok_g = lax.broadcasted_iota(jnp.int32, (n_g, T), 1)
            act = jnp.where(tok_g >= valid, jnp.where(row_g < 4, NEG_BIG, 0.0), act)
        r_t = lax.broadcasted_iota(jnp.int32, (T, T), 0)
        c_t = lax.broadcasted_iota(jnp.int32, (T, T), 1)
        shift = int(math.log2(L))
        within = jnp.where((r_t <= c_t) & ((r_t >> shift) == (c_t >> shift)), 1.0, 0.0)
        z_g = jnp.where((row_g >= 4) & (row_g < 12), highest(act, within), act)
        pick = jnp.where(lax.broadcasted_iota(jnp.int32, (n_g, GATE_LANES), 0)
                         == lax.broadcasted_iota(jnp.int32, (n_g, GATE_LANES), 1), 1.0, 0.0)
        z_t = highest(z_g, pick, _TN)
        z_sc[sa] = z_t
        z = [z_t[rows(c), :] for c in chunks]

        def spread(x):
            hi = x.astype(BF16)
            lo = (x - hi.astype(F32)).astype(BF16)
            return lax.dot_general(jnp.concatenate([hi, lo], axis=1), spread_ref[...], _NT,
                                   preferred_element_type=F32)

        zb = [spread(z[c]) for c in chunks]

        def zcol(c, k):
            return zb[c][:, k * GATE_LANES:(k + 1) * GATE_LANES]
        r_d = lax.broadcasted_iota(jnp.int32, (T, 2 * L), 0)
        c_d = lax.broadcasted_iota(jnp.int32, (T, 2 * L), 1)
        zt = [highest(z_g, jnp.where(r_d == c * L + (c_d & (L - 1)), 1.0, 0.0)) for c in chunks]
        q_sc[sa] = proj_ref[s, :, 0:GROUP_W]
        og_sc[sa] = proj_ref[s, :, 3 * GROUP_W:4 * GROUP_W]
        zg_sc[sa] = proj_ref[s, :, base + 3 * GROUP_W:base + 4 * GROUP_W]
        yield

        u_sc[s, HIST_ROW:HIST_ROW + T, :] = proj_ref[s, :, base:base + 3 * GROUP_W].astype(F32)
        for blk in range(3 * N_HEADS):
            cols = slice(blk * HEAD_DIM, (blk + 1) * HEAD_DIM)
            u_all = u_sc[s, :, cols]
            y = convw_ref[0:1, cols] * u_all
            for tap in range(1, CONV_TAPS):
                y = pltpu.roll(y, 1, axis=0) + convw_ref[tap:tap + 1, cols] * u_all
            y = y[HIST_ROW:, :]
            y = y * _sigmoid(y)
            kind, h = divmod(blk, N_HEADS)
            if kind == 0:
                qn_sc[hd + h] = (y * lax.rsqrt(jnp.sum(y * y, axis=-1, keepdims=True) + EPS) * scale).astype(BF16)
            elif kind == 1:
                kf_sc[hd + h] = y * lax.rsqrt(jnp.sum(y * y, axis=-1, keepdims=True) + EPS)
            else:
                v_sc[hd + h] = y
        t_valid = T if valid == L else valid
        u_sc[s, HIST_ROW - 3:HIST_ROW, :] = u_sc[s, HIST_ROW + t_valid - 3:HIST_ROW + t_valid, :]
        yield

        def a_k(c, h):
            return proj_ref[s, rows(c), hcols(h, 1)]

        qk_a = {(c, h): lax.dot_general(proj_ref[s, rows(c), hcols(h)], two(a_k(c, h)), _NT,
                                        preferred_element_type=F32) for c, h in ch}
        qk_b = {}
        for c, h in ch:
            kn = kf_sc[hd + h, rows(c), :].astype(BF16)
            qk_b[c, h] = lax.dot_general(jnp.concatenate([qn_sc[hd + h, rows(c), :], kn], axis=0), two(kn), _NT,
                                         preferred_element_type=F32)
        yield

        d_a, m_a, s_a, kw_a = {}, {}, {}, {}
        for c, h in ch:
            ig_row, b_row = zt[c][h:h + 1, :], zt[c][4 + h:5 + h, :]
            d_a[c, h] = jnp.where(tri_incl, zcol(c, 4 + h) - b_row + ig_row, NEG_BIG)
        for c, h in ch:
            m_a[c, h] = jnp.max(d_a[c, h], axis=1, keepdims=True)
            st_sc[sa, rows(c), 4 + h:5 + h] = m_a[c, h]
        yield
        for c, h in ch:
            s_a[c, h] = jnp.where(left, qk_a[c, h] * scale * jnp.exp(d_a[c, h] - m_a[c, h]), 0.0)
        for c, h in ch:
            ig_b, b_b = zcol(c, h), zcol(c, 4 + h)
            w_s = jnp.exp(b_b[L - 1:L, :] - b_b + ig_b - m_a[c, h][L - 1:L, :])
            kw = a_k(c, h).astype(F32) * scale * w_s
            dn_sc[sa, idx[c, h]:idx[c, h] + 1, :] = jnp.sum(kw, axis=0, keepdims=True)
            kw_a[c, h] = kw.astype(BF16)
        yield
        for c, h in ch:
            v = proj_ref[s, rows(c), hcols(h, 2)]
            sv1 = _dot(s_a[c, h].astype(BF16), jnp.concatenate([two(v), ones_sq], axis=1))
            sv_sc[sa * n_ch + idx[c, h]] = sv1[:, :HEAD_DIM]
            st_sc[sa, rows(c), h:h + 1] = sv1[:, HEAD_DIM:HEAD_DIM + 1]
            dc_sc[sa * n_ch + idx[c, h]] = lax.dot_general(kw_a[c, h], v, _TN, preferred_element_type=F32)
        yield

        decay, r_b = {}, {}
        for c, h in ch:
            g_row = zt[c][8 + h:9 + h, :]
            decay[c, h] = jnp.exp(jnp.where(tri_incl, zcol(c, 8 + h) - g_row, NEG_BIG))
            a2 = jnp.where(tri_strict, zcol(c, 12 + h) * decay[c, h] * qk_b[c, h][L:, :], 0.0)
            r_b[c, h] = jnp.where(left, eye_left, -a2)
        yield
        for _ in range(int(math.log2(L))):
            for p in ch:
                r = r_b[p]
                out = _dot(jnp.where(left, 0.0, r).astype(BF16), two(r.astype(BF16)))
                r_b[p] = jnp.where(left, r + out, out)
            yield
        uk = {}
        for c, h in ch:
            beta = zcol(c, 12 + h)
            rhs = jnp.concatenate([beta * v_sc[hd + h, rows(c), :],
                                   beta * jnp.exp(zcol(c, 8 + h)) * kf_sc[hd + h, rows(c), :]], axis=1).astype(BF16)
            uk[c, h] = _dot(jnp.where(left, r_b[c, h], 0.0).astype(BF16), two(rhs)).astype(BF16)
        yield
        for c, h in ch:
            g_col = zcol(c, 8 + h)
            g_last = g_col[L - 1:L, :]
            qkd = jnp.where(left, qk_b[c, h][:L, :] * decay[c, h], 0.0).astype(BF16)
            qu = _dot(qkd, two(uk[c, h]))
            kd = (kf_sc[hd + h, rows(c), :] * jnp.exp(g_last - g_col)).astype(BF16)
            ku = lax.dot_general(kd, uk[c, h], _TN, preferred_element_type=F32)
            q_eff = jnp.exp(g_col) * qn_sc[hd + h, rows(c), :].astype(F32) - qu[:, HEAD_DIM:]
            i = sa * n_ch + idx[c, h]
            qu_sc[i] = qu[:, :HEAD_DIM]
            nc_sc[i] = ku[:, :HEAD_DIM]
            lb_sc[i] = jnp.concatenate([ku[:, HEAD_DIM:], q_eff], axis=0).astype(BF16)
        yield

    def phase_b(s):
        sb = s * slots + slot_b
        hd = s * N_HEADS
        for c in chunks:
            zc = z_sc[sb, rows(c), :]
            stc = st_sc[sb, rows(c), :]
            base_i = sb * n_ch + c * N_HEADS
            c_prev = [c_sc[hd + h] for h in heads]
            s_prev = [s_sc[hd + h] for h in heads]
            n_prev = [n_sc[hd + h, 0:1, :] for h in heads]
            qc = [_dot(q_sc[sb, rows(c), hcols(h)], c_prev[h].astype(BF16)) for h in heads]
            rm = [_dot(lb_sc[base_i + h], s_prev[h].astype(BF16)) for h in heads]
            m_prev = m_sc[s, 0:1, :]
            inter = zc + m_prev
            m = jnp.maximum(inter, stc)
            m_last = jnp.where(head_lanes, m[L - 1:L, :], 0.0)
            decay0_t = jnp.exp(zc[L - 1:L, :] + m_prev - m_last)
            f_new_t = jnp.exp(stc[L - 1:L, :] - m_last)
            m_sc[s, 0:1, :] = m_last
            for h in heads:
                hl = slice(N_HEADS + h, N_HEADS + h + 1)
                c_sc[hd + h] = decay0_t[:, hl] * c_prev[h] + f_new_t[:, hl] * dc_sc[base_i + h]
                n_sc[hd + h, 0:1, :] = (decay0_t[:, hl] * n_prev[h]
                                        + f_new_t[:, hl] * dn_sc[sb, c * N_HEADS + h:c * N_HEADS + h + 1, :])
            for h in heads:
                s_sc[hd + h] = (jnp.exp(zc[L - 1:L, 8 + h:9 + h]) * s_prev[h] + nc_sc[base_i + h]
                                - rm[h][:HEAD_DIM, :])
            yield
            w_inter_t = jnp.exp(inter - m)
            w_intra_t = jnp.exp(stc - m)
            inv_cap_t = jnp.exp(-m)
            qn = [jnp.sum(q_sc[sb, rows(c), hcols(h)].astype(F32) * n_prev[h], axis=-1, keepdims=True)
                  for h in heads]
            num, cap, o_b = [], [], []
            for h in heads:
                hl = slice(N_HEADS + h, N_HEADS + h + 1)
                w_inter, w_intra = w_inter_t[:, hl], w_intra_t[:, hl]
                num.append(w_inter * qc[h] + w_intra * sv_sc[base_i + h])
                den = w_inter * qn[h] + w_intra * stc[:, h:h + 1]
                cap.append(jnp.maximum(jnp.abs(den), inv_cap_t[:, hl]))
                o_b.append(rm[h][HEAD_DIM:, :] + qu_sc[base_i + h])
            ms_a = [jnp.mean(x * x, axis=-1, keepdims=True) for x in num]
            ms_b = [jnp.mean(x * x, axis=-1, keepdims=True) for x in o_b]
            for h in heads:
                hn = num[h] * lax.rsqrt(ms_a[h] + EPS * cap[h] * cap[h]) * anw_ref[...]
                h_ref[s, rows(c), hcols(h)] = (og_sc[sb, rows(c), hcols(h)].astype(F32) * hn).astype(BF16)
            for h in heads:
                on = o_b[h] * lax.rsqrt(ms_b[h] + EPS) * bnw_ref[...]
                h_ref[s, rows(c), hcols(h, 1)] = (on * zg_sc[sb, rows(c), hcols(h)].astype(F32)).astype(BF16)
            yield

    steps_a, steps_b = 9 + int(math.log2(L)), 2 * n_chunks
    gens_a = [phase_a(s) for s in seqs]
    gens_b = [phase_b(s) for s in seqs]

    def advance(gens):
        for g in gens:
            next(g)

    if pipelined:
        done_b = 0
        for k in range(steps_a):
            while done_b < steps_b and done_b * steps_a <= k * steps_b:
                advance(gens_b)
                done_b += 1
            advance(gens_a)
        for _ in range(steps_b - done_b):
            advance(gens_b)
    else:
        for _ in range(steps_a):
            advance(gens_a)
        for _ in range(steps_b):
            advance(gens_b)
    assert all(next(g, None) is None for g in gens_a + gens_b)

    def store_state():
        c1_ref[...] = c_sc[...].reshape(c1_ref.shape)
        s1_ref[...] = s_sc[...].reshape(s1_ref.shape)
        for s in seqs:
            for h in heads:
                n1_ref[s, h:h + 1, :] = n_sc[s * N_HEADS + h, 0:1, :]
            m1_ref[s] = m_sc[s, 0:1, N_HEADS:2 * N_HEADS]

    def store_conv_history():
        conv1_ref[...] = u_sc[:, HIST_ROW - 3:HIST_ROW, :]

    if pipelined:
        pl.when(has_b & (j_b == n_tiles - 1))(store_state)
        pl.when(j_a == n_tiles - 1)(store_conv_history)
    else:
        store_state()
        store_conv_history()


def _spread_matrix():
    n = lax.broadcasted_iota(jnp.int32, (4 * N_HEADS * GATE_LANES, 2 * GATE_LANES), 0)
    k = lax.broadcasted_iota(jnp.int32, (4 * N_HEADS * GATE_LANES, 2 * GATE_LANES), 1)
    return ((k % GATE_LANES) == (n // GATE_LANES)).astype(BF16)


def _mixers(proj, gates, c0, n0, m0, s0, conv0, gate_par, conv_w, a_norm_w, b_norm_w, *, valid, tile, n_seq):
    bsz, t, n = proj.shape
    tile = min(tile, t)
    assert t % tile == 0 and tile % CHUNK == 0 and bsz % n_seq == 0
    assert valid == CHUNK or (t == CHUNK and CONV_TAPS - 1 <= valid < CHUNK)
    n_chunks = tile // CHUNK
    n_tiles = t // tile
    pipelined = n_tiles > 1
    slots = 2 if pipelined else 1
    n_ch = n_chunks * N_HEADS
    cw = 3 * GROUP_W
    n_groups = bsz // n_seq
    total = n_groups * n_tiles
    if pipelined:
        pair_a = lambda s: jnp.minimum(s, total - 1)
        pair_b = lambda s: jnp.maximum(s - 1, 0)
    else:
        pair_a = pair_b = lambda s: s
    tile_a = lambda s: (pair_a(s) // n_tiles, pair_a(s) % n_tiles)
    tile_b = lambda s: (pair_b(s) // n_tiles, pair_b(s) % n_tiles)
    st4 = lambda s: (tile_b(s)[0], 0, 0, 0)
    st3 = lambda s: (tile_b(s)[0], 0, 0)
    cst2 = lambda s: (0, 0)
    cst3 = lambda s: (0, 0, 0)
    conv_spec = pl.BlockSpec((n_seq, CONV_TAPS - 1, cw), lambda s: (tile_a(s)[0], 0, 0))
    state_specs = [
        pl.BlockSpec((n_seq, N_HEADS, HEAD_DIM, HEAD_DIM), st4),
        pl.BlockSpec((n_seq, N_HEADS, HEAD_DIM), st3),
        pl.BlockSpec((n_seq, 1, N_HEADS), st3),
        pl.BlockSpec((n_seq, N_HEADS, HEAD_DIM, HEAD_DIM), st4),
        conv_spec,
    ]
    state_shapes = [
        jax.ShapeDtypeStruct((bsz, N_HEADS, HEAD_DIM, HEAD_DIM), F32),
        jax.ShapeDtypeStruct((bsz, N_HEADS, HEAD_DIM), F32),
        jax.ShapeDtypeStruct((bsz, 1, N_HEADS), F32),
        jax.ShapeDtypeStruct((bsz, N_HEADS, HEAD_DIM, HEAD_DIM), F32),
        jax.ShapeDtypeStruct((bsz, CONV_TAPS - 1, cw), F32),
    ]
    ns, nh = n_seq * slots, n_seq * N_HEADS
    return pl.pallas_call(
        functools.partial(_mixer_body, n_seq=n_seq, n_chunks=n_chunks, n_tiles=n_tiles, valid=valid,
                          pipelined=pipelined),
        grid=(total + (1 if pipelined else 0),),
        in_specs=[
            pl.BlockSpec((n_seq, tile, n), lambda s: (*tile_a(s), 0)),
            pl.BlockSpec((n_seq, 4 * N_HEADS, tile), lambda s: (tile_a(s)[0], 0, tile_a(s)[1])),
            *state_specs,
            pl.BlockSpec((2, 4 * N_HEADS, tile), cst3),
            pl.BlockSpec((CONV_TAPS, cw), cst2),
            pl.BlockSpec((1, HEAD_DIM), cst2),
            pl.BlockSpec((1, HEAD_DIM), cst2),
            pl.BlockSpec((4 * N_HEADS * GATE_LANES, 2 * GATE_LANES), cst2),
        ],
        out_specs=[pl.BlockSpec((n_seq, tile, 2 * GROUP_W), lambda s: (*tile_b(s), 0)), *state_specs],
        out_shape=[jax.ShapeDtypeStruct((bsz, t, 2 * GROUP_W), BF16), *state_shapes],
        scratch_shapes=[
            pltpu.VMEM((nh, HEAD_DIM, HEAD_DIM), F32),
            pltpu.VMEM((nh, 8, HEAD_DIM), F32),
            pltpu.VMEM((n_seq, 8, GATE_LANES), F32),
            pltpu.VMEM((nh, HEAD_DIM, HEAD_DIM), F32),
            pltpu.VMEM((n_seq, HIST_ROW + tile, cw), F32),
            pltpu.VMEM((nh, tile, HEAD_DIM), BF16),
            pltpu.VMEM((nh, tile, HEAD_DIM), F32),
            pltpu.VMEM((nh, tile, HEAD_DIM), F32),
            pltpu.VMEM((ns, tile, GATE_LANES), F32),
            pltpu.VMEM((ns, tile, GATE_LANES), F32),
            pltpu.VMEM((ns, n_ch, HEAD_DIM), F32),
            pltpu.VMEM((ns, tile, GROUP_W), BF16),
            pltpu.VMEM((ns, tile, GROUP_W), BF16),
            pltpu.VMEM((ns, tile, GROUP_W), BF16),
            pltpu.VMEM((ns * n_ch, HEAD_DIM, HEAD_DIM), F32),
            pltpu.VMEM((ns * n_ch, CHUNK, HEAD_DIM), F32),
            pltpu.VMEM((ns * n_ch, HEAD_DIM, HEAD_DIM), F32),
            pltpu.VMEM((ns * n_ch, HEAD_DIM + CHUNK, HEAD_DIM), BF16),
            pltpu.VMEM((ns * n_ch, CHUNK, HEAD_DIM), F32),
        ],
        compiler_params=_params(("arbitrary",)),
        name="mixers",
    )(proj, gates, c0, n0, m0, s0, conv0, gate_par, conv_w, a_norm_w, b_norm_w, _spread_matrix())


def _mem_kv_body(mem_ref, nw_ref, wkv_ref, k_ref, v_ref, kb_ref, vb_ref):
    nb, n_mem, d = mem_ref.shape
    dh = d // N_XHEADS
    mn = _rms(mem_ref[...].reshape(nb * n_mem, d), nw_ref[...]).astype(BF16)
    for w0, f_ref, b_ref in ((0, k_ref, kb_ref), (d, v_ref, vb_ref)):
        y = _dot(mn, wkv_ref[:, w0:w0 + d])
        b_ref[...] = y.astype(BF16).reshape(nb, n_mem, d)
        for h in range(N_XHEADS):
            f_ref[:, :, h, :] = y[:, h * dh:(h + 1) * dh].reshape(nb, n_mem, dh)


def _mem_kv(mem, norm_w, wkv):
    bsz, n_mem, d = mem.shape
    nb = max(1, min(bsz, ROW_TILE // n_mem))
    assert bsz % nb == 0
    b3 = lambda i: (i, 0, 0)
    b4 = lambda i: (i, 0, 0, 0)
    cst = lambda i: (0, 0)
    dh = d // N_XHEADS
    return pl.pallas_call(
        _mem_kv_body,
        grid=(bsz // nb,),
        in_specs=[pl.BlockSpec((nb, n_mem, d), b3), pl.BlockSpec((1, d), cst), pl.BlockSpec((d, 2 * d), cst)],
        out_specs=[pl.BlockSpec((nb, n_mem, N_XHEADS, dh), b4)] * 2 + [pl.BlockSpec((nb, n_mem, d), b3)] * 2,
        out_shape=([jax.ShapeDtypeStruct((bsz, n_mem, N_XHEADS, dh), F32)] * 2
                   + [jax.ShapeDtypeStruct((bsz, n_mem, d), BF16)] * 2),
        compiler_params=_params(("arbitrary",)),
        name="mem_kv",
    )(mem, norm_w, wkv)


def _attn_body(h_ref, x_ref, wout_ref, nx_ref, wq_ref, mk_ref, mv_ref, wo_ref, o_ref, *kv_scratch, n_sub):
    nb, tm, d = x_ref.shape
    dh = d // N_XHEADS
    rows = nb * tm
    gr = rows // n_sub
    assert gr % tm == 0 or tm % gr == 0
    per = max(1, gr // tm)
    hcols = [slice(h * dh, (h + 1) * dh) for h in range(N_XHEADS)]

    if kv_scratch:
        k_buf, v_buf, sems = kv_scratch
        first = pl.program_id(0) * nb

        def kv_copy(which, src, dst, b, h):
            return pltpu.make_async_copy(src.at[first + b, :, h, :], dst.at[b, h], sems.at[which, b, h])

        copies = [kv_copy(w, src, dst, b, h) for w, (src, dst) in enumerate(((mk_ref, k_buf), (mv_ref, v_buf)))
                  for b in range(nb) for h in range(N_XHEADS)]
        for i, cp in enumerate(copies):
            cp.start(priority=i % 2)

    def heads_of(which, b):
        if kv_scratch:
            return [(k_buf, v_buf)[which][b, h].astype(BF16) for h in range(N_XHEADS)]
        return [(mk_ref, mv_ref)[which][b, :, c] for c in hcols]

    def group_rows(ref, g):
        if per > 1 or gr == tm:
            return ref[g * per:(g + 1) * per].reshape(gr, d)
        return ref[(g * gr) // tm, (g * gr) % tm:(g * gr) % tm + gr, :]

    groups = range(n_sub)
    x1 = [group_rows(x_ref, g) + _dot(group_rows(h_ref, g), wout_ref[...]) for g in groups]
    xn = [_rms(x, nx_ref[...]).astype(BF16) for x in x1]
    q = [_dot(x, wq_ref[...]).astype(BF16) for x in xn]
    unit_rows = min(gr, tm)
    units = [(g, slice(u * unit_rows, (u + 1) * unit_rows), (g * gr + u * unit_rows) // tm)
             for g in groups for u in range(gr // unit_rows)]
    if kv_scratch:
        for cp in copies:
            cp.wait()
    kv = {b: (heads_of(0, b), heads_of(1, b)) for b in sorted({b for _, _, b in units})}
    s = [[lax.dot_general(q[g][r, c], kh, _NT, preferred_element_type=F32) * (dh ** -0.5)
          for c, kh in zip(hcols, kv[b][0])] for g, r, b in units]
    mx = [[jnp.max(sh, axis=-1, keepdims=True) for sh in si] for si in s]
    e = [[jnp.exp(sh - mh) for sh, mh in zip(si, mi)] for si, mi in zip(s, mx)]
    den = [[jnp.sum(eh, axis=-1, keepdims=True) for eh in ei] for ei in e]
    p = [[(eh / dh_).astype(BF16) for eh, dh_ in zip(ei, di)] for ei, di in zip(e, den)]
    o = [[_dot(ph, vh).astype(BF16) for ph, vh in zip(pi, kv[b][1])] for pi, (_, _, b) in zip(p, units)]
    for g in groups:
        mine = [oi for oi, (ug, _, _) in zip(o, units) if ug == g]
        acc = x1[g]
        for h, c in enumerate(hcols):
            oh = mine[0][h] if len(mine) == 1 else jnp.concatenate([oi[h] for oi in mine], axis=0)
            acc = acc + _dot(oh, wo_ref[c, :])
        if per > 1 or gr == tm:
            o_ref[g * per:(g + 1) * per] = acc.reshape(per, tm, d)
        else:
            o_ref[(g * gr) // tm, (g * gr) % tm:(g * gr) % tm + gr, :] = acc


def _out_proj_attn(hcat, x, w_out, norm_w, wq, mk, mv, wo):
    bsz, t, d = x.shape
    tm = min(ROW_TILE, t)
    assert t % tm == 0
    nb = max(1, min(bsz, ATTN_MIN_ROWS // tm))
    assert bsz % nb == 0
    bj = lambda b, j: (b, j, 0)
    cst = lambda b, j: (0, 0)
    gather = mk.ndim == 4
    if gather:
        assert t == tm
        mem_spec = pl.BlockSpec(memory_space=pl.ANY)
        n_mem, n_h, dh = mk.shape[1:]
        scratch = [pltpu.VMEM((nb, n_h, n_mem, dh), mk.dtype), pltpu.VMEM((nb, n_h, n_mem, dh), mv.dtype),
                   pltpu.SemaphoreType.DMA((2, nb, n_h))]
    else:
        mem_spec = pl.BlockSpec((nb,) + mk.shape[1:], lambda b, j: (b, 0, 0))
        scratch = []
    return pl.pallas_call(
        functools.partial(_attn_body, n_sub=2 if tm >= 256 else 1),
        grid=(bsz // nb, t // tm),
        in_specs=[
            pl.BlockSpec((nb, tm, d), bj),
            pl.BlockSpec((nb, tm, d), bj),
            pl.BlockSpec((d, d), cst),
            pl.BlockSpec((1, d), cst),
            pl.BlockSpec((d, d), cst),
            mem_spec,
            mem_spec,
            pl.BlockSpec((d, d), cst),
        ],
        out_specs=pl.BlockSpec((nb, tm, d), bj),
        out_shape=jax.ShapeDtypeStruct((bsz, t, d), F32),
        scratch_shapes=scratch,
        compiler_params=_params(("arbitrary", "arbitrary")),
        name="out_proj_attn",
    )(hcat, x, w_out, norm_w, wq, mk, mv, wo)


def _ffn_body(x_ref, nf_ref, w1_ref, w2_ref, nfin_ref, o_ref, *, hid_tile, final_norm, n_sub):
    tm = x_ref.shape[0]
    subs = [slice(i * (tm // n_sub), (i + 1) * (tm // n_sub)) for i in range(n_sub)]
    acc = [x_ref[r, :] for r in subs]
    xn = [_rms(a, nf_ref[...]).astype(BF16) for a in acc]
    for c0 in range(0, w1_ref.shape[1], hid_tile):
        hdn = [jnp.maximum(_dot(x, w1_ref[:, c0:c0 + hid_tile]), 0.0) for x in xn]
        act = [(h * h).astype(BF16) for h in hdn]
        acc = [a + _dot(h, w2_ref[c0:c0 + hid_tile, :]) for a, h in zip(acc, act)]
    for r, a in zip(subs, acc):
        o_ref[r, :] = _rms(a, nfin_ref[...]) if final_norm else a


def _ffn(x2d, norm_w, w1, w2, final_w, *, final_norm):
    rows, d = x2d.shape
    dff = w1.shape[1]
    tm = min(ROW_TILE, rows)
    assert rows % tm == 0
    row = lambda i: (i, 0)
    cst = lambda i: (0, 0)
    return pl.pallas_call(
        functools.partial(_ffn_body, hid_tile=1024, final_norm=final_norm, n_sub=2 if tm >= 512 else 1),
        grid=(rows // tm,),
        in_specs=[
            pl.BlockSpec((tm, d), row),
            pl.BlockSpec((1, d), cst),
            pl.BlockSpec((d, dff), cst, pipeline_mode=pl.Buffered(1)),
            pl.BlockSpec((dff, d), cst, pipeline_mode=pl.Buffered(1)),
            pl.BlockSpec((1, d), cst),
        ],
        out_specs=pl.BlockSpec((tm, d), row),
        out_shape=jax.ShapeDtypeStruct((rows, d), F32),
        compiler_params=_params(("arbitrary",)),
        name="ffn",
    )(x2d, norm_w, w1, w2, final_w)


def _split_w_in(w_in):
    w = GROUP_W
    g0 = 4 * w
    b0 = g0 + 2 * N_HEADS
    g1 = b0 + 4 * w
    main = jnp.concatenate([w_in[:, :g0], w_in[:, b0:g1]], axis=1).astype(BF16)
    gate = jnp.concatenate([w_in[:, g0:b0], w_in[:, g1:g1 + 2 * N_HEADS]], axis=1)
    return main, gate.T.astype(BF16)


def _gate_params(igate_b, fgate_b, a_log, dt_bias, width):
    zeros = jnp.zeros((N_HEADS,), F32)
    bias = jnp.concatenate([igate_b, fgate_b, dt_bias, zeros]).astype(F32)
    alog = jnp.concatenate([zeros, zeros, a_log.astype(F32), zeros])
    return jnp.broadcast_to(jnp.stack([bias, alog])[:, :, None], (2, 4 * N_HEADS, width))


def _trunk_layer(x, mem_k, mem_v, c0, n0, m0, s0, conv0, *, valid, w_main, w_gate, gate_vecs, lw, final_w,
                 final_norm):
    bsz, t, d = x.shape
    tp = CHUNK if valid < CHUNK else t
    xp = x if tp == t else jnp.pad(x, ((0, 0), (0, tp - t), (0, 0)))
    proj, gates = _in_proj(xp.reshape(bsz * tp, d), lw["norm_mix_w"], w_main, w_gate, tp)
    hcat, c1, n1, m1, s1, conv1 = _mixers(
        proj.reshape(bsz, tp, -1), gates, c0, n0, m0.reshape(bsz, 1, N_HEADS), s0, conv0,
        _gate_params(*gate_vecs, width=min(MIXER_TILE, tp)), lw["gdn_conv_w"], lw["mlstm_norm_w"],
        lw["gdn_norm_w"], valid=valid, tile=MIXER_TILE, n_seq=MIXER_SEQS if tp <= MIXER_TILE else 1)
    x2 = _out_proj_attn(hcat, x, lw["w_out"], lw["norm_x_w"], lw["wq_x"], mem_k, mem_v, lw["wo_x"])
    y = _ffn(x2.reshape(bsz * t, d), lw["norm_ffn_w"], lw["w_ff1"], lw["w_ff2"], final_w,
             final_norm=final_norm)
    return y.reshape(bsz, t, d), c1, n1, m1.reshape(bsz, N_HEADS), s1, conv1


def kernel(x_prompt, x_sample, state_mlstm_C, state_mlstm_n, state_mlstm_m, state_gdn_S, state_gdn_conv, cache_mem_k, cache_mem_v, mem_prompt, norm_mix_w, w_in, mlstm_igate_b, mlstm_fgate_b, mlstm_norm_w, gdn_conv_w, gdn_A_log, gdn_dt_bias, gdn_norm_w, w_out, norm_x_w, norm_mem_w, wq_x, wk_x, wv_x, wo_x, norm_ffn_w, w_ff1, w_ff2, norm_final_w):
    depth = w_in.shape[0]
    bp = x_prompt.shape[0]
    ts = x_sample.shape[1]
    row = lambda a: a.reshape(1, -1).astype(F32)
    hp, hs = x_prompt, x_sample
    outs_p = [[] for _ in range(7)]
    outs_s = [[] for _ in range(5)]
    for l in range(depth):
        w_main, w_gate = _split_w_in(w_in[l])
        gate_vecs = (mlstm_igate_b[l], mlstm_fgate_b[l], gdn_A_log[l], gdn_dt_bias[l])
        lw = dict(
            norm_mix_w=row(norm_mix_w[l]), gdn_conv_w=gdn_conv_w[l].astype(F32),
            mlstm_norm_w=row(mlstm_norm_w[l]), gdn_norm_w=row(gdn_norm_w[l]),
            w_out=w_out[l].astype(BF16), norm_x_w=row(norm_x_w[l]), wq_x=wq_x[l].astype(BF16),
            wo_x=wo_x[l].astype(BF16), norm_ffn_w=row(norm_ffn_w[l]),
            w_ff1=w_ff1[l].astype(BF16), w_ff2=w_ff2[l].astype(BF16))
        common = dict(w_main=w_main, w_gate=w_gate, gate_vecs=gate_vecs, lw=lw, final_w=row(norm_final_w),
                      final_norm=(l == depth - 1))
        wkv = jnp.concatenate([wk_x[l], wv_x[l]], axis=1).astype(BF16)
        mk, mv, mk_b, mv_b = _mem_kv(mem_prompt, row(norm_mem_w[l]), wkv)
        hp, c1, n1, m1, s1, cv1 = _trunk_layer(
            hp, mk_b, mv_b,
            jnp.zeros((bp, N_HEADS, HEAD_DIM, HEAD_DIM), F32), jnp.zeros((bp, N_HEADS, HEAD_DIM), F32),
            jnp.zeros((bp, N_HEADS), F32), jnp.zeros((bp, N_HEADS, HEAD_DIM, HEAD_DIM), F32),
            jnp.zeros((bp, CONV_TAPS - 1, 3 * GROUP_W), F32), valid=CHUNK, **common)
        for acc, val in zip(outs_p, (c1, n1, m1, s1, cv1, mk, mv)):
            acc.append(val)
        hs, c2, n2, m2, s2, cv2 = _trunk_layer(
            hs, cache_mem_k[l], cache_mem_v[l],
            state_mlstm_C[l], state_mlstm_n[l], state_mlstm_m[l], state_gdn_S[l], state_gdn_conv[l],
            valid=ts, **common)
        for acc, val in zip(outs_s, (c2, n2, m2, s2, cv2)):
            acc.append(val)
    return (hp, hs, *[jnp.stack(a) for a in outs_p], *[jnp.stack(a) for a in outs_s])
```
